```python
import math
import jax, jax.numpy as jnp
from jax import lax
import numpy as np

D_MODEL = 2048
BATCH = 8
SEQ = 2048
DEPTH = 2
DEC_BATCH = 16
DEC_SEQ = 64
PAST_LEN = 2048

CHUNK = 64
Q_BLOCK = 128
MLA_HEADS = 6
MLA_Q_RANK = 512
MLA_KV_RANK = 256
MLA_NOPE = 128
MLA_ROPE = 64
MLA_V = 128
ROPE_THETA = 10000.0
DIFF_HEADS = 4
DIFF_QK = 64
DIFF_V = 2 * DIFF_QK
BAND_HEADS = 6
BAND_DIM = 128
BAND_PREV_CHUNKS = 8
BAND_REL_CLIP = 256
T5_BUCKETS = 32
T5_MAX_DIST = 128
D_FF = 5632
CONV_W = 3
LN_EPS = 1e-5
RMS_EPS = 1e-6

IN_SIZES = (MLA_Q_RANK, MLA_KV_RANK, MLA_ROPE,
            DIFF_HEADS * 2 * DIFF_QK, DIFF_HEADS * 2 * DIFF_QK, DIFF_HEADS * DIFF_V,
            BAND_HEADS * BAND_DIM, BAND_HEADS * BAND_DIM, BAND_HEADS * BAND_DIM)
IN_COLS = sum(IN_SIZES)
MIX_WIDTH = MLA_HEADS * MLA_V + DIFF_HEADS * DIFF_V + BAND_HEADS * BAND_DIM
DEEPNORM_ALPHA = (2 * DEPTH) ** 0.25
DEEPNORM_BETA = (8 * DEPTH) ** -0.25

kernel_name = 'hybrid_streaming_encoder_step'


def layer_norm(x, g, b):
    xf = x.astype(jnp.float32)
    mu = jnp.mean(xf, -1, keepdims=True)
    var = jnp.mean(jnp.square(xf - mu), -1, keepdims=True)
    return ((xf - mu) * lax.rsqrt(var + LN_EPS) * g + b).astype(x.dtype)


def rms_norm(x, g):
    xf = x.astype(jnp.float32)
    return (xf * lax.rsqrt(jnp.mean(xf * xf, -1, keepdims=True) + RMS_EPS) * g).astype(x.dtype)


def rope(x, pos):
    half = x.shape[-1] // 2
    inv = ROPE_THETA ** (-jnp.arange(half, dtype=jnp.float32) / half)
    ang = pos.astype(jnp.float32)[:, None] * inv
    shape = (pos.shape[0],) + (1,) * (x.ndim - 3) + (half,)
    cos = jnp.cos(ang).reshape(shape)
    sin = jnp.sin(ang).reshape(shape)
    xf = x.astype(jnp.float32)
    x1, x2 = xf[..., :half], xf[..., half:]
    return jnp.concatenate([x1 * cos - x2 * sin, x1 * sin + x2 * cos], -1).astype(x.dtype)


def chunk_causal_mask(q_pos, k_pos):
    return (q_pos[:, None] // CHUNK) >= (k_pos[None, :] // CHUNK)


def t5_bucket(rel):
    half = T5_BUCKETS // 2
    exact = half // 2
    n = jnp.abs(rel)
    nf = jnp.maximum(n, 1).astype(jnp.float32)
    large = exact + (jnp.log(nf / exact) / math.log(T5_MAX_DIST / exact) * (half - exact)).astype(jnp.int32)
    large = jnp.minimum(large, half - 1)
    return jnp.where(rel > 0, half, 0) + jnp.where(n < exact, n, large)


def over_query_blocks(fn, q_pos, *qs):
    lq = q_pos.shape[0]
    if lq <= Q_BLOCK or lq % Q_BLOCK:
        return fn(q_pos, *qs)
    nb = lq // Q_BLOCK
    qp = q_pos.reshape(nb, Q_BLOCK)
    qb = [jnp.moveaxis(q.reshape((q.shape[0], nb, Q_BLOCK) + q.shape[2:]), 1, 0) for q in qs]
    out = lax.map(lambda a: fn(*a), (qp, *qb))
    out = jnp.moveaxis(out, 0, 1)
    return out.reshape((out.shape[0], lq) + out.shape[3:])


def mla_mixer(c_q, c_kv_raw, k_rope_raw, pos, past_ckv, past_krope, q_norm_g, w_uq, kv_norm_g, w_ukv):
    B, L, _ = c_q.shape
    q = (rms_norm(c_q, q_norm_g) @ w_uq).reshape(B, L, MLA_HEADS, MLA_NOPE + MLA_ROPE)
    q_nope = q[..., :MLA_NOPE]
    q_rope = rope(q[..., MLA_NOPE:], pos)
    ckv = rms_norm(c_kv_raw, kv_norm_g)
    krope = rope(k_rope_raw, pos)
    if past_ckv is None:
        ckv_all, krope_all, k_pos = ckv, krope, pos
    else:
        ckv_all = jnp.concatenate([past_ckv, ckv], 1)
        krope_all = jnp.concatenate([past_krope, krope], 1)
        k_pos = jnp.concatenate([jnp.arange(past_ckv.shape[1], dtype=jnp.int32), pos])
    lk = ckv_all.shape[1]
    kv = (ckv_all @ w_ukv).reshape(B, lk, MLA_HEADS, MLA_NOPE + MLA_V)
    k_nope, v = kv[..., :MLA_NOPE], kv[..., MLA_NOPE:]
    scale = (MLA_NOPE + MLA_ROPE) ** -0.5

    def attend(qp, qn, qr):
        s = (jnp.einsum('bqhd,bkhd->bhqk', qn, k_nope)
             + jnp.einsum('bqhr,bkr->bhqk', qr, krope_all)).astype(jnp.float32) * scale
        s = jnp.where(chunk_causal_mask(qp, k_pos), s, -jnp.inf)
        p = jax.nn.softmax(s, -1).astype(v.dtype)
        return jnp.einsum('bhqk,bkhd->bqhd', p, v)

    o = over_query_blocks(attend, pos, q_nope, q_rope)
    return o.reshape(B, L, MLA_HEADS * MLA_V), ckv, krope


def diff_mixer(d_q, d_k, d_v, pos, past_k, past_v, t5_table, lq1, lk1, lq2, lk2, subln_g, layer_idx):
    B, L, _ = d_q.shape
    q = d_q.reshape(B, L, DIFF_HEADS, 2, DIFF_QK)
    k_rows = d_k.reshape(B, L, DIFF_HEADS, 2 * DIFF_QK)
    v_rows = d_v.reshape(B, L, DIFF_HEADS, DIFF_V)
    if past_k is None:
        k_all, v_all, k_pos = k_rows, v_rows, pos
    else:
        k_all = jnp.concatenate([past_k, k_rows], 1)
        v_all = jnp.concatenate([past_v, v_rows], 1)
        k_pos = jnp.concatenate([jnp.arange(past_k.shape[1], dtype=jnp.int32), pos])
    lk = k_all.shape[1]
    k_all2 = k_all.reshape(B, lk, DIFF_HEADS, 2, DIFF_QK)
    lam_init = 0.8 - 0.6 * math.exp(-0.3 * layer_idx)
    lam = (jnp.exp(jnp.sum(lq1.astype(jnp.float32) * lk1.astype(jnp.float32)))
           - jnp.exp(jnp.sum(lq2.astype(jnp.float32) * lk2.astype(jnp.float32))) + lam_init)
    scale = DIFF_QK ** -0.5

    def attend(qp, qq):
        bias = t5_table[t5_bucket(k_pos[None, :] - qp[:, None])]
        bias = jnp.transpose(bias, (2, 0, 1)).astype(jnp.float32)
        s = jnp.einsum('bqhcd,bkhcd->bchqk', qq, k_all2).astype(jnp.float32) * scale + bias
        s = jnp.where(chunk_causal_mask(qp, k_pos), s, -jnp.inf)
        p = jax.nn.softmax(s, -1)
        a = p[:, 0] - lam * p[:, 1]
        return jnp.einsum('bhqk,bkhd->bqhd', a.astype(v_all.dtype), v_all)

    o = over_query_blocks(attend, pos, q)
    o = rms_norm(o, subln_g) * (1.0 - lam_init)
    return o.reshape(B, L, DIFF_HEADS * DIFF_V), k_rows, v_rows


def band_attend(q, k, v, q_pos, k_pos, rel_table):
    rel = jnp.clip(q_pos[:, :, None] - k_pos[:, None, :], -BAND_REL_CLIP, BAND_REL_CLIP) + BAND_REL_CLIP
    bias = jnp.transpose(rel_table[:, rel], (1, 0, 2, 3)).astype(jnp.float32)
    qc = (q_pos // CHUNK)[:, :, None]
    kc = (k_pos // CHUNK)[:, None, :]
    mask = (k_pos[:, None, :] >= 0) & (kc <= qc) & (kc >= qc - BAND_PREV_CHUNKS)
    s = jnp.einsum('bnqhd,bnkhd->bnhqk', q, k).astype(jnp.float32) * (BAND_DIM ** -0.5) + bias[None]
    s = jnp.where(mask[:, None], s, -jnp.inf)
    p = jax.nn.softmax(s, -1).astype(v.dtype)
    return jnp.einsum('bnhqk,bnkhd->bnqhd', p, v)


def band_mixer(b_q, b_k, b_v, pos, past_k, past_v, rel_table):
    B, L, _ = b_q.shape
    q = b_q.reshape(B, L, BAND_HEADS, BAND_DIM)
    k = b_k.reshape(B, L, BAND_HEADS, BAND_DIM)
    v = b_v.reshape(B, L, BAND_HEADS, BAND_DIM)
    band_rows = BAND_PREV_CHUNKS * CHUNK
    if past_k is None:
        nc = L // CHUNK
        idx = jnp.arange(nc)[:, None] + jnp.arange(BAND_PREV_CHUNKS + 1)[None, :]

        def gather_band(t):
            tc = t.reshape(B, nc, CHUNK, BAND_HEADS, BAND_DIM)
            tp = jnp.pad(tc, ((0, 0), (BAND_PREV_CHUNKS, 0), (0, 0), (0, 0), (0, 0)))
            return tp[:, idx].reshape(B, nc, (BAND_PREV_CHUNKS + 1) * CHUNK, BAND_HEADS, BAND_DIM)

        k_pos = (((idx - BAND_PREV_CHUNKS) * CHUNK)[:, :, None]
                 + jnp.arange(CHUNK, dtype=jnp.int32)[None, None, :]).reshape(nc, -1)
        o = band_attend(q.reshape(B, nc, CHUNK, BAND_HEADS, BAND_DIM), gather_band(k), gather_band(v),
                        pos.reshape(nc, CHUNK), k_pos, rel_table)
        keep = min(band_rows, L)
        new_k, new_v = k[:, L - keep:], v[:, L - keep:]
    else:
        w = past_k.shape[1]
        k_all = jnp.concatenate([past_k, k], 1)[:, None]
        v_all = jnp.concatenate([past_v, v], 1)[:, None]
        k_pos = jnp.concatenate([pos[0] - w + jnp.arange(w, dtype=jnp.int32), pos])[None]
        o = band_attend(q[:, None], k_all, v_all, pos[None], k_pos, rel_table)
        new_k, new_v = k, v
    return o.reshape(B, L, BAND_HEADS * BAND_DIM), new_k, new_v


def conv_ffn(x, prev, w_gate, w_up, conv_w, conv_b, w_down):
    B, L, _ = x.shape
    g = x @ w_gate
    u = x @ w_up
    if prev is None:
        prev = jnp.zeros((B, CONV_W - 1, D_FF), g.dtype)
    gp = jnp.concatenate([prev, g], 1)
    gc = conv_b + conv_w[0] * gp[:, 0:L]
    for j in range(1, CONV_W):
        gc = gc + conv_w[j] * gp[:, j:j + L]
    h = jax.nn.silu(gc) * u
    return h @ w_down, gp[:, L:]


def trunk_layer(x, pos, past, prm, l):
    if past is None:
        past = (None,) * 7
    p_ckv, p_krope, p_dk, p_dv, p_bk, p_bv, p_conv = past
    h = x @ prm['w_in'][l]
    offs = np.cumsum(IN_SIZES)[:-1].tolist()
    c_q, c_kv, k_rope, d_q, d_k, d_v, b_q, b_k, b_v = jnp.split(h, offs, axis=-1)
    o_a, ckv, krope = mla_mixer(c_q, c_kv, k_rope, pos, p_ckv, p_krope, prm['mla_q_norm'][l],
                                prm['mla_w_uq'][l], prm['mla_kv_norm'][l], prm['mla_w_ukv'][l])
    o_b, dk, dv = diff_mixer(d_q, d_k, d_v, pos, p_dk, p_dv, prm['t5_table'], prm['diff_lq1'][l],
                             prm['diff_lk1'][l], prm['diff_lq2'][l], prm['diff_lk2'][l],
                             prm['diff_subln'][l], l)
    o_c, bk, bv = band_mixer(b_q, b_k, b_v, pos, p_bk, p_bv, prm['band_rel_table'][l])
    mix = jnp.concatenate([o_a, o_b, o_c], -1) @ prm['w_o'][l]
    x = layer_norm(DEEPNORM_ALPHA * x + mix, prm['ln1_g'][l], prm['ln1_b'][l])
    f, conv_state = conv_ffn(x, p_conv, prm['ffn_w_gate'][l], prm['ffn_w_up'][l],
                             prm['ffn_conv_w'][l], prm['ffn_conv_b'][l], prm['ffn_w_down'][l])
    x = layer_norm(DEEPNORM_ALPHA * x + f, prm['ln2_g'][l], prm['ln2_b'][l])
    return x, (ckv, krope, dk, dv, bk, bv, conv_state)


def setup_inputs(seed: int = 0) -> dict:
    key = jax.random.key(seed)
    ks = jax.random.split(key, 32)
    nrm = lambda k, shape, s: jax.random.normal(k, shape, jnp.float32) * s
    cw = min(BAND_PREV_CHUNKS * CHUNK, PAST_LEN)
    return {
        'x_prompt': nrm(ks[0], (BATCH, SEQ, D_MODEL), 1.0),
        'x_sample': nrm(ks[1], (DEC_BATCH, DEC_SEQ, D_MODEL), 1.0),
        'cache_mla_ckv': nrm(ks[2], (DEPTH, DEC_BATCH, PAST_LEN, MLA_KV_RANK), 1.0),
        'cache_mla_krope': nrm(ks[3], (DEPTH, DEC_BATCH, PAST_LEN, MLA_ROPE), 1.0),
        'cache_diff_k': nrm(ks[4], (DEPTH, DEC_BATCH, PAST_LEN, DIFF_HEADS, 2 * DIFF_QK), 1.0),
        'cache_diff_v': nrm(ks[5], (DEPTH, DEC_BATCH, PAST_LEN, DIFF_HEADS, DIFF_V), 1.0),
        'cache_band_k': nrm(ks[6], (DEPTH, DEC_BATCH, cw, BAND_HEADS, BAND_DIM), 1.0),
        'cache_band_v': nrm(ks[7], (DEPTH, DEC_BATCH, cw, BAND_HEADS, BAND_DIM), 1.0),
        'state_ffn_conv': nrm(ks[8], (DEPTH, DEC_BATCH, CONV_W - 1, D_FF), 1.0),
        't5_table': nrm(ks[9], (T5_BUCKETS, DIFF_HEADS), 0.5),
        'w_in': nrm(ks[10], (DEPTH, D_MODEL, IN_COLS), D_MODEL ** -0.5),
        'mla_q_norm': 1.0 + nrm(ks[11], (DEPTH, MLA_Q_RANK), 0.02),
        'mla_w_uq': nrm(ks[12], (DEPTH, MLA_Q_RANK, MLA_HEADS * (MLA_NOPE + MLA_ROPE)), MLA_Q_RANK ** -0.5),
        'mla_kv_norm': 1.0 + nrm(ks[13], (DEPTH, MLA_KV_RANK), 0.02),
        'mla_w_ukv': nrm(ks[14], (DEPTH, MLA_KV_RANK, MLA_HEADS * (MLA_NOPE + MLA_V)), MLA_KV_RANK ** -0.5),
        'diff_lq1': nrm(ks[15], (DEPTH, DIFF_QK), 0.1),
        'diff_lk1': nrm(ks[16], (DEPTH, DIFF_QK), 0.1),
        'diff_lq2': nrm(ks[17], (DEPTH, DIFF_QK), 0.1),
        'diff_lk2': nrm(ks[18], (DEPTH, DIFF_QK), 0.1),
        'diff_subln': 1.0 + nrm(ks[19], (DEPTH, DIFF_V), 0.02),
        'band_rel_table': nrm(ks[20], (DEPTH, BAND_HEADS, 2 * BAND_REL_CLIP + 1), 0.5),
        'w_o': nrm(ks[21], (DEPTH, MIX_WIDTH, D_MODEL), MIX_WIDTH ** -0.5 * DEEPNORM_BETA),
        'ln1_g': 1.0 + nrm(ks[22], (DEPTH, D_MODEL), 0.02),
        'ln1_b': nrm(ks[23], (DEPTH, D_MODEL), 0.02),
        'ffn_w_gate': nrm(ks[24], (DEPTH, D_MODEL, D_FF), D_MODEL ** -0.5),
        'ffn_w_up': nrm(ks[25], (DEPTH, D_MODEL, D_FF), D_MODEL ** -0.5 * DEEPNORM_BETA),
        'ffn_conv_w': nrm(ks[26], (DEPTH, CONV_W, D_FF), CONV_W ** -0.5),
        'ffn_conv_b': nrm(ks[27], (DEPTH, D_FF), 0.01),
        'ffn_w_down': nrm(ks[28], (DEPTH, D_FF, D_MODEL), D_FF ** -0.5 * DEEPNORM_BETA),
        'ln2_g': 1.0 + nrm(ks[29], (DEPTH, D_MODEL), 0.02),
        'ln2_b': nrm(ks[30], (DEPTH, D_MODEL), 0.02),
    }


def reference(x_prompt, x_sample, cache_mla_ckv, cache_mla_krope, cache_diff_k, cache_diff_v,
              cache_band_k, cache_band_v, state_ffn_conv, t5_table, w_in, mla_q_norm, mla_w_uq,
              mla_kv_norm, mla_w_ukv, diff_lq1, diff_lk1, diff_lq2, diff_lk2, diff_subln,
              band_rel_table, w_o, ln1_g, ln1_b, ffn_w_gate, ffn_w_up, ffn_conv_w, ffn_conv_b,
              ffn_w_down, ln2_g, ln2_b):
    prm = {'t5_table': t5_table, 'w_in': w_in, 'mla_q_norm': mla_q_norm, 'mla_w_uq': mla_w_uq,
           'mla_kv_norm': mla_kv_norm, 'mla_w_ukv': mla_w_ukv, 'diff_lq1': diff_lq1,
           'diff_lk1': diff_lk1, 'diff_lq2': diff_lq2, 'diff_lk2': diff_lk2,
           'diff_subln': diff_subln, 'band_rel_table': band_rel_table, 'w_o': w_o,
           'ln1_g': ln1_g, 'ln1_b': ln1_b, 'ffn_w_gate': ffn_w_gate, 'ffn_w_up': ffn_w_up,
           'ffn_conv_w': ffn_conv_w, 'ffn_conv_b': ffn_conv_b, 'ffn_w_down': ffn_w_down,
           'ln2_g': ln2_g, 'ln2_b': ln2_b}
    past_len = cache_mla_ckv.shape[2]
    pos_p = jnp.arange(x_prompt.shape[1], dtype=jnp.int32)
    pos_s = past_len + jnp.arange(x_sample.shape[1], dtype=jnp.int32)
    y_prompt, y_sample = x_prompt, x_sample
    states_p, states_s = [], []
    for l in range(DEPTH):
        y_prompt, st_p = trunk_layer(y_prompt, pos_p, None, prm, l)
        past = (cache_mla_ckv[l], cache_mla_krope[l], cache_diff_k[l], cache_diff_v[l],
                cache_band_k[l], cache_band_v[l], state_ffn_conv[l])
        y_sample, st_s = trunk_layer(y_sample, pos_s, past, prm, l)
        states_p.append(st_p)
        states_s.append(st_s)
    p_ckv, p_krope, p_dk, p_dv, p_bk, p_bv, p_conv = [jnp.stack(t) for t in zip(*states_p)]
    s_ckv, s_krope, s_dk, s_dv, s_bk, s_bv, s_conv = [jnp.stack(t) for t in zip(*states_s)]
    return (y_prompt, y_sample, p_ckv, p_krope, p_dk, p_dv, p_bk, p_bv, p_conv,
            s_ckv, s_krope, s_dk, s_dv, s_bk, s_bv, s_conv)
```

```python
import functools
import math

import jax
import jax.numpy as jnp
import numpy as np
from jax import lax
from jax.experimental import pallas as pl
from jax.experimental.pallas import tpu as pltpu

F32 = jnp.float32
BF16 = jnp.bfloat16

D_MODEL = 2048
DEPTH = 2
CHUNK = 64
MLA_HEADS = 6
MLA_Q_RANK = 512
MLA_KV_RANK = 256
MLA_NOPE = 128
MLA_ROPE = 64
MLA_V = 128
ROPE_THETA = 10000.0
DIFF_HEADS = 4
DIFF_QK = 64
DIFF_V = 128
BAND_HEADS = 6
BAND_DIM = 128
BAND_PREV_CHUNKS = 8
BAND_REL_CLIP = 256
T5_BUCKETS = 32
T5_MAX_DIST = 128
D_FF = 5632
CONV_W = 3
LN_EPS = 1e-5
RMS_EPS = 1e-6
DEEPNORM_ALPHA = (2 * DEPTH) ** 0.25

NEG = -1e30
MLA_QK = 256
MLA_SCALE = (MLA_NOPE + MLA_ROPE) ** -0.5
DIFF_SCALE = DIFF_QK ** -0.5
BAND_SCALE = BAND_DIM ** -0.5
BAND_ROWS = BAND_PREV_CHUNKS * CHUNK
ATT_T = 256
VMEM_LIMIT = 56 * 1024 * 1024

_O_CQ, _O_CKV, _O_DQ, _O_DK, _O_DV = 0, 512, 768, 1280, 1792
_O_BQ, _O_BK, _O_BV, _O_KR, _O_KRS, _IN_P = 2304, 3072, 3840, 4608, 4736, 4864
_O_MIX_DIFF = MLA_HEADS * MLA_V
_O_MIX_BAND = _O_MIX_DIFF + DIFF_HEADS * DIFF_V


def _cparams(sem):
    return pltpu.CompilerParams(dimension_semantics=sem, vmem_limit_bytes=VMEM_LIMIT)


def _dot(a, b):
    return jnp.dot(a, b, preferred_element_type=F32)


def _dot_nt(a, b):
    return lax.dot_general(a, b, (((1,), (1,)), ((), ())), preferred_element_type=F32)


def _rms(x, g):
    return x * lax.rsqrt(jnp.mean(x * x, axis=-1, keepdims=True) + RMS_EPS) * g


def _layer_norm(x, g, b):
    mu = jnp.mean(x, axis=-1, keepdims=True)
    xc = x - mu
    var = jnp.mean(xc * xc, axis=-1, keepdims=True)
    return xc * lax.rsqrt(var + LN_EPS) * g + b


def _resident(shape):
    nd = len(shape)
    return pl.BlockSpec(shape, lambda *_: (0,) * nd, pipeline_mode=pl.Buffered(1))


def _proj_kernel(x_ref, win_ref, wuq_ref, wukv_ref, qn_ref, kvn_ref, cos_ref, sin_ref,
                 ckv_o, krope_o, dk_o, dv_o, bk_o, bv_o,
                 dq_b, dk_b, dv_b, bq_b, bk_b, bv_b, qf_b, kf_b, v_b):
    xb = x_ref[...].astype(BF16)
    cos = cos_ref[...]
    sin = sin_ref[...]

    def seg(off, n):
        return _dot(xb, win_ref[:, off:off + n])

    cq = _rms(seg(_O_CQ, MLA_Q_RANK), qn_ref[...]).astype(BF16)
    nh = MLA_HEADS * 128
    for h in range(MLA_HEADS):
        lo = h * 128
        qn = _dot(cq, wuq_ref[:, lo:lo + 128])
        qr = (_dot(cq, wuq_ref[:, nh + lo:nh + lo + 128]) * cos
              + _dot(cq, wuq_ref[:, 2 * nh + lo:2 * nh + lo + 128]) * sin)
        qf_b[h, :, 0:128] = (qn * MLA_SCALE).astype(BF16)
        qf_b[h, :, 128:256] = (qr * MLA_SCALE).astype(BF16)

    ckv = _rms(seg(_O_CKV, MLA_KV_RANK), kvn_ref[...])
    ckv_o[...] = ckv
    ckvb = ckv.astype(BF16)
    kr = seg(_O_KR, 128) * cos + seg(_O_KRS, 128) * sin
    krope_o[...] = kr[:, 0:MLA_ROPE]
    krb = kr.astype(BF16)
    for h in range(MLA_HEADS):
        lo = h * 128
        kf_b[h, :, 0:128] = _dot(ckvb, wukv_ref[:, lo:lo + 128]).astype(BF16)
        kf_b[h, :, 128:256] = krb
    v_b[...] = _dot(ckvb, wukv_ref[:, nh:2 * nh]).astype(BF16)

    dq_b[...] = (seg(_O_DQ, 512) * DIFF_SCALE).astype(BF16)
    dk = seg(_O_DK, 512)
    dk_o[...] = dk
    dk_b[...] = dk.astype(BF16)
    dv = seg(_O_DV, 512)
    dv_o[...] = dv
    dv_b[...] = dv.astype(BF16)
    bq_b[...] = (seg(_O_BQ, 768) * BAND_SCALE).astype(BF16)
    bk = seg(_O_BK, 768)
    bk_o[...] = bk
    bk_b[...] = bk.astype(BF16)
    bv = seg(_O_BV, 768)
    bv_o[...] = bv
    bv_b[...] = bv.astype(BF16)


def _proj(x2, lw, cos_t, sin_t, tm):
    m = x2.shape[0]
    period = cos_t.shape[0] // tm
    row = lambda n: pl.BlockSpec((tm, n), lambda t: (t, 0))
    tab = pl.BlockSpec((tm, 128), lambda t: (t % period, 0))
    head = pl.BlockSpec((MLA_HEADS, tm, MLA_QK), lambda t: (0, t, 0))
    f32 = lambda n: jax.ShapeDtypeStruct((m, n), F32)
    b16 = lambda n: jax.ShapeDtypeStruct((m, n), BF16)
    hb16 = jax.ShapeDtypeStruct((MLA_HEADS, m, MLA_QK), BF16)
    return pl.pallas_call(
        _proj_kernel,
        grid=(m // tm,),
        in_specs=[row(D_MODEL), _resident(lw['w_in'].shape), _resident(lw['w_uq'].shape),
                  _resident(lw['w_ukv'].shape), _resident((1, MLA_Q_RANK)),
                  _resident((1, MLA_KV_RANK)), tab, tab],
        out_specs=[row(256), row(64), row(512), row(512), row(768), row(768),
                   row(512), row(512), row(512), row(768), row(768), row(768),
                   head, head, row(768)],
        out_shape=[f32(256), f32(64), f32(512), f32(512), f32(768), f32(768),
                   b16(512), b16(512), b16(512), b16(768), b16(768), b16(768),
                   hb16, hb16, b16(768)],
        compiler_params=_cparams(("arbitrary",)),
        name="proj",
    )(x2, lw['w_in'], lw['w_uq'], lw['w_ukv'], lw['q_norm'], lw['kv_norm'], cos_t, sin_t)


def _kvup_kernel(ckv_ref, kr_ref, wukv_ref, kf_b, v_b):
    ckvb = ckv_ref[...].astype(BF16)
    krb = kr_ref[...].astype(BF16)
    zeros = jnp.zeros((krb.shape[0], MLA_QK - MLA_NOPE - MLA_ROPE), BF16)
    nh = MLA_HEADS * 128
    for h in range(MLA_HEADS):
        lo = h * 128
        kf_b[h, :, 0:128] = _dot(ckvb, wukv_ref[:, lo:lo + 128]).astype(BF16)
        kf_b[h, :, 128:192] = krb
        kf_b[h, :, 192:256] = zeros
    v_b[...] = _dot(ckvb, wukv_ref[:, nh:2 * nh]).astype(BF16)


def _kvup(ckv2, kr2, w_ukv, tm=512):
    m = ckv2.shape[0]
    return pl.pallas_call(
        _kvup_kernel,
        grid=(m // tm,),
        in_specs=[pl.BlockSpec((tm, MLA_KV_RANK), lambda t: (t, 0)),
                  pl.BlockSpec((tm, MLA_ROPE), lambda t: (t, 0)),
                  _resident(w_ukv.shape)],
        out_specs=[pl.BlockSpec((MLA_HEADS, tm, MLA_QK), lambda t: (0, t, 0)),
                   pl.BlockSpec((tm, 768), lambda t: (t, 0))],
        out_shape=[jax.ShapeDtypeStruct((MLA_HEADS, m, MLA_QK), BF16),
                   jax.ShapeDtypeStruct((m, 768), BF16)],
        compiler_params=_cparams(("arbitrary",)),
        name="kvup",
    )(ckv2, kr2, w_ukv)


def _chunk_mask(t):
    r = lax.broadcasted_iota(jnp.int32, (t, t), 0) // CHUNK
    c = lax.broadcasted_iota(jnp.int32, (t, t), 1) // CHUNK
    return r >= c


def _online_step(carry, s, vb):
    m, l, acc = carry
    m_new = jnp.maximum(m, jnp.max(s, axis=-1, keepdims=True))
    a = jnp.exp(m - m_new)
    p = jnp.exp(s - m_new)
    l = a * l + jnp.sum(p, axis=-1, keepdims=True)
    acc = a * acc + _dot(p.astype(BF16), vb)
    return m_new, l, acc


def _online_init(t, dv):
    return (jnp.full((t, 1), NEG, F32), jnp.zeros((t, 1), F32), jnp.zeros((t, dv), F32))


def _softmax_pv(blocks):
    m = None
    for s, _ in blocks:
        bm = jnp.max(s, axis=-1, keepdims=True)
        m = bm if m is None else jnp.maximum(m, bm)
    l = None
    acc = None
    for s, vb in blocks:
        p = jnp.exp(s - m)
        bl = jnp.sum(p, axis=-1, keepdims=True)
        ba = _dot(p.astype(BF16), vb)
        l = bl if l is None else l + bl
        acc = ba if acc is None else acc + ba
    return acc / l


def _mla_prompt_kernel(q_ref, k_ref, v_ref, o_ref):
    t = ATT_T
    i = pl.program_id(2)
    q = q_ref[0]

    def block(j):
        kb = k_ref[0, pl.ds(pl.multiple_of(j * t, t), t), :]
        vb = v_ref[pl.ds(pl.multiple_of(j * t, t), t), :]
        return _dot_nt(q, kb), vb

    def body(j, carry):
        s, vb = block(j)
        return _online_step(carry, s, vb)

    carry = lax.fori_loop(0, i, body, _online_init(t, MLA_V))
    s, vb = block(i)
    s = jnp.where(_chunk_mask(t), s, NEG)
    _, l, acc = _online_step(carry, s, vb)
    o_ref[...] = (acc / l).astype(BF16)


def _mla_prompt(qf, kf, v, nb, seq):
    t = ATT_T
    nq = seq // t
    return pl.pallas_call(
        _mla_prompt_kernel,
        grid=(nb, MLA_HEADS, nq),
        in_specs=[pl.BlockSpec((1, t, MLA_QK), lambda b, h, i: (h, b * nq + i, 0)),
                  pl.BlockSpec((1, seq, MLA_QK), lambda b, h, i: (h, b, 0)),
                  pl.BlockSpec((seq, MLA_V), lambda b, h, i: (b, h))],
        out_specs=pl.BlockSpec((t, MLA_V), lambda b, h, i: (b * nq + i, h)),
        out_shape=jax.ShapeDtypeStruct((nb * seq, MLA_HEADS * MLA_V), BF16),
        compiler_params=_cparams(("arbitrary",) * 3),
        name="mla_prompt",
    )(qf, kf, v)


def _mla_sample_kernel(q_ref, kp_ref, vp_ref, kn_ref, vn_ref, o_ref):
    q = q_ref[0]
    o = _softmax_pv([(_dot_nt(q, kp_ref[0]), vp_ref[...]),
                     (_dot_nt(q, kn_ref[0]), vn_ref[...])])
    o_ref[...] = o.astype(BF16)


def _mla_sample(qf, kf_past, v_past, kf_new, v_new, nb, seq, past):
    return pl.pallas_call(
        _mla_sample_kernel,
        grid=(nb, MLA_HEADS),
        in_specs=[pl.BlockSpec((1, seq, MLA_QK), lambda b, h: (h, b, 0)),
                  pl.BlockSpec((1, past, MLA_QK), lambda b, h: (h, b, 0)),
                  pl.BlockSpec((past, MLA_V), lambda b, h: (b, h)),
                  pl.BlockSpec((1, seq, MLA_QK), lambda b, h: (h, b, 0)),
                  pl.BlockSpec((seq, MLA_V), lambda b, h: (b, h))],
        out_specs=pl.BlockSpec((seq, MLA_V), lambda b, h: (b, h)),
        out_shape=jax.ShapeDtypeStruct((nb * seq, MLA_HEADS * MLA_V), BF16),
        compiler_params=_cparams(("arbitrary",) * 2),
        name="mla_sample",
    )(qf, kf_past, v_past, kf_new, v_new)


def _diff_lambda(lam_ref, lam_init):
    v = lam_ref[...]
    a = jnp.sum(v[0:1] * v[1:2], axis=-1, keepdims=True)
    b = jnp.sum(v[2:3] * v[3:4], axis=-1, keepdims=True)
    return jnp.exp(a) - jnp.exp(b) + lam_init


def _split_q(q):
    lane = lax.broadcasted_iota(jnp.int32, q.shape, 1)
    qf = q.astype(F32)
    return (jnp.where(lane < DIFF_QK, qf, 0.0).astype(BF16),
            jnp.where(lane >= DIFF_QK, qf, 0.0).astype(BF16))


def _diff_finish(o0, o1, lam, g, lam_init):
    o = o0 - lam * o1
    return (_rms(o, g) * (1.0 - lam_init)).astype(BF16)


def _diff_prompt_kernel(q_ref, k_ref, v_ref, bias_ref, lam_ref, g_ref, o_ref, *, lam_init):
    t = ATT_T
    i = pl.program_id(2)
    qa, qb = _split_q(q_ref[...])

    def block(j):
        kb = k_ref[pl.ds(pl.multiple_of(j * t, t), t), :]
        vb = v_ref[pl.ds(pl.multiple_of(j * t, t), t), :]
        bias = bias_ref[0, i - j]
        return _dot_nt(qa, kb) + bias, _dot_nt(qb, kb) + bias, vb

    def body(j, carry):
        c0, c1 = carry
        s0, s1, vb = block(j)
        return _online_step(c0, s0, vb), _online_step(c1, s1, vb)

    init = _online_init(t, DIFF_V)
    c0, c1 = lax.fori_loop(0, i, body, (init, init))
    s0, s1, vb = block(i)
    mask = _chunk_mask(t)
    _, l0, a0 = _online_step(c0, jnp.where(mask, s0, NEG), vb)
    _, l1, a1 = _online_step(c1, jnp.where(mask, s1, NEG), vb)
    lam = _diff_lambda(lam_ref, lam_init)
    o_ref[...] = _diff_finish(a0 / l0, a1 / l1, lam, g_ref[...], lam_init)


def _diff_prompt(dq, dk, dv, bias, lam_vecs, subln, nb, seq, lam_init):
    t = ATT_T
    nq = seq // t
    return pl.pallas_call(
        functools.partial(_diff_prompt_kernel, lam_init=lam_init),
        grid=(nb, DIFF_HEADS, nq),
        in_specs=[pl.BlockSpec((t, 128), lambda b, h, i: (b * nq + i, h)),
                  pl.BlockSpec((seq, 128), lambda b, h, i: (b, h)),
                  pl.BlockSpec((seq, 128), lambda b, h, i: (b, h)),
                  pl.BlockSpec((1, nq, t, t), lambda b, h, i: (h, 0, 0, 0)),
                  pl.BlockSpec((4, DIFF_QK), lambda b, h, i: (0, 0)),
                  pl.BlockSpec((1, DIFF_V), lambda b, h, i: (0, 0))],
        out_specs=pl.BlockSpec((t, DIFF_V), lambda b, h, i: (b * nq + i, h)),
        out_shape=jax.ShapeDtypeStruct((nb * seq, DIFF_HEADS * DIFF_V), BF16),
        compiler_params=_cparams(("arbitrary",) * 3),
        name="diff_prompt",
    )(dq, dk, dv, bias, lam_vecs, subln)


def _diff_sample_kernel(q_ref, kp_ref, vp_ref, kn_ref, vn_ref, bias_ref, lam_ref, g_ref,
                        o_ref, *, lam_init, past):
    qa, qb = _split_q(q_ref[...])
    kp = kp_ref[...].astype(BF16)
    vp = vp_ref[...].astype(BF16)
    kn = kn_ref[...]
    vn = vn_ref[...]
    bias_p = bias_ref[0, :, 0:past]
    bias_n = bias_ref[0, :, past:]
    o0 = _softmax_pv([(_dot_nt(qa, kp) + bias_p, vp), (_dot_nt(qa, kn) + bias_n, vn)])
    o1 = _softmax_pv([(_dot_nt(qb, kp) + bias_p, vp), (_dot_nt(qb, kn) + bias_n, vn)])
    lam = _diff_lambda(lam_ref, lam_init)
    o_ref[...] = _diff_finish(o0, o1, lam, g_ref[...], lam_init)


def _diff_sample(dq, dk_new, dv_new, k_past, v_past, bias, lam_vecs, subln, nb, seq, past,
                 lam_init):
    return pl.pallas_call(
        functools.partial(_diff_sample_kernel, lam_init=lam_init, past=past),
        grid=(nb, DIFF_HEADS),
        in_specs=[pl.BlockSpec((seq, 128), lambda b, h: (b, h)),
                  pl.BlockSpec((past, 128), lambda b, h: (b, h)),
                  pl.BlockSpec((past, 128), lambda b, h: (b, h)),
                  pl.BlockSpec((seq, 128), lambda b, h: (b, h)),
                  pl.BlockSpec((seq, 128), lambda b, h: (b, h)),
                  pl.BlockSpec((1, seq, past + seq), lambda b, h: (h, 0, 0)),
                  pl.BlockSpec((4, DIFF_QK), lambda b, h: (0, 0)),
                  pl.BlockSpec((1, DIFF_V), lambda b, h: (0, 0))],
        out_specs=pl.BlockSpec((seq, DIFF_V), lambda b, h: (b, h)),
        out_shape=jax.ShapeDtypeStruct((nb * seq, DIFF_HEADS * DIFF_V), BF16),
        compiler_params=_cparams(("arbitrary",) * 2),
        name="diff_sample",
    )(dq, k_past, v_past, dk_new, dv_new, bias, lam_vecs, subln)


def _band_prompt_kernel(q_ref, k_ref, v_ref, bias_ref, o_ref):
    t = ATT_T
    full = BAND_ROWS + t
    i = pl.program_id(2)
    q = q_ref[...]

    def attend(start, width):
        kw = k_ref[pl.ds(start, width), :]
        vw = v_ref[pl.ds(start, width), :]
        s = _dot_nt(q, kw) + bias_ref[0, :, full - width:full]
        o_ref[...] = _softmax_pv([(s, vw)]).astype(BF16)

    for first in range(BAND_ROWS // t):
        @pl.when(i == first)
        def _(first=first):
            attend(0, (first + 1) * t)

    @pl.when(i >= BAND_ROWS // t)
    def _():
        attend(pl.multiple_of(i * t - BAND_ROWS, t), full)


def _band_prompt(bq, bk, bv, bias, nb, seq):
    t = ATT_T
    nq = seq // t
    return pl.pallas_call(
        _band_prompt_kernel,
        grid=(nb, BAND_HEADS, nq),
        in_specs=[pl.BlockSpec((t, BAND_DIM), lambda b, h, i: (b * nq + i, h)),
                  pl.BlockSpec((seq, BAND_DIM), lambda b, h, i: (b, h)),
                  pl.BlockSpec((seq, BAND_DIM), lambda b, h, i: (b, h)),
                  pl.BlockSpec((1, t, BAND_ROWS + t), lambda b, h, i: (h, 0, 0))],
        out_specs=pl.BlockSpec((t, BAND_DIM), lambda b, h, i: (b * nq + i, h)),
        out_shape=jax.ShapeDtypeStruct((nb * seq, BAND_HEADS * BAND_DIM), BF16),
        compiler_params=_cparams(("arbitrary",) * 3),
        name="band_prompt",
    )(bq, bk, bv, bias)


def _band_sample_kernel(q_ref, kp_ref, vp_ref, kn_ref, vn_ref, bias_ref, o_ref, *, past):
    q = q_ref[...]
    kp = kp_ref[...].astype(BF16)
    vp = vp_ref[...].astype(BF16)
    o = _softmax_pv([(_dot_nt(q, kp) + bias_ref[0, :, 0:past], vp),
                     (_dot_nt(q, kn_ref[...]) + bias_ref[0, :, past:], vn_ref[...])])
    o_ref[...] = o.astype(BF16)


def _band_sample(bq, bk_new, bv_new, k_past, v_past, bias, nb, seq, past):
    return pl.pallas_call(
        functools.partial(_band_sample_kernel, past=past),
        grid=(nb, BAND_HEADS),
        in_specs=[pl.BlockSpec((seq, BAND_DIM), lambda b, h: (b, h)),
                  pl.BlockSpec((past, BAND_DIM), lambda b, h: (b, h)),
                  pl.BlockSpec((past, BAND_DIM), lambda b, h: (b, h)),
                  pl.BlockSpec((seq, BAND_DIM), lambda b, h: (b, h)),
                  pl.BlockSpec((seq, BAND_DIM), lambda b, h: (b, h)),
                  pl.BlockSpec((1, seq, past + seq), lambda b, h: (h, 0, 0))],
        out_specs=pl.BlockSpec((seq, BAND_DIM), lambda b, h: (b, h)),
        out_shape=jax.ShapeDtypeStruct((nb * seq, BAND_HEADS * BAND_DIM), BF16),
        compiler_params=_cparams(("arbitrary",) * 2),
        name="band_sample",
    )(bq, k_past, v_past, bk_new, bv_new, bias)


def _oproj_kernel(a_ref, b_ref, c_ref, x_ref, wo_ref, g_ref, beta_ref, o_ref):
    mix = (_dot(a_ref[...], wo_ref[0:_O_MIX_DIFF, :])
           + _dot(b_ref[...], wo_ref[_O_MIX_DIFF:_O_MIX_BAND, :])
           + _dot(c_ref[...], wo_ref[_O_MIX_BAND:, :]))
    o_ref[...] = _layer_norm(DEEPNORM_ALPHA * x_ref[...] + mix, g_ref[...], beta_ref[...])


def _oproj_ln(o_a, o_b, o_c, x2, w_o, g, beta, tm):
    m = x2.shape[0]
    row = lambda n: pl.BlockSpec((tm, n), lambda t: (t, 0))
    return pl.pallas_call(
        _oproj_kernel,
        grid=(m // tm,),
        in_specs=[row(o_a.shape[1]), row(o_b.shape[1]), row(o_c.shape[1]), row(D_MODEL),
                  _resident(w_o.shape), _resident((1, D_MODEL)), _resident((1, D_MODEL))],
        out_specs=row(D_MODEL),
        out_shape=jax.ShapeDtypeStruct((m, D_MODEL), F32),
        compiler_params=_cparams(("arbitrary",)),
        name="oproj_ln",
    )(o_a, o_b, o_c, x2, w_o, g, beta)


def _ffn_kernel(x_ref, prev_ref, wg_ref, wu_ref, wd_ref, cw_ref, cb_ref, g_ref, beta_ref,
                o_ref, st_ref, xb_ref, acc_ref, carry_ref, *, tf, seg, tiles_per_seq):
    i = pl.program_id(0)
    j = pl.program_id(1)
    nj = pl.num_programs(1)
    tm = x_ref.shape[0]
    col = pl.ds(pl.multiple_of(j * tf, tf), tf)

    @pl.when(j == 0)
    def _():
        xb_ref[...] = x_ref[...].astype(BF16)
        acc_ref[...] = jnp.zeros_like(acc_ref)

    xb = xb_ref[...]
    g = _dot(xb, wg_ref[...])
    u = _dot(xb, wu_ref[...])
    cw = cw_ref[...]
    cb = cb_ref[...]
    rows = lax.broadcasted_iota(jnp.int32, (seg, tf), 0)
    seq_start = (i % tiles_per_seq) == 0

    hs = []
    for s in range(tm // seg):
        gs = g[s * seg:(s + 1) * seg]
        state = prev_ref[s]
        if tiles_per_seq > 1:
            state = jnp.where(seq_start, state, carry_ref[:, col])
        pm2, pm1 = state[0:1], state[1:2]
        gm1 = jnp.where(rows == 0, pm1, pltpu.roll(gs, 1, 0))
        gm2 = jnp.where(rows == 0, pm2, jnp.where(rows == 1, pm1, pltpu.roll(gs, 2, 0)))
        gc = cb + cw[0:1] * gm2 + cw[1:2] * gm1 + cw[2:3] * gs
        hs.append(gc * (1.0 / (1.0 + jnp.exp(-gc))) * u[s * seg:(s + 1) * seg])
        last = gs[seg - 2:seg]
        st_ref[s, :, col] = last
        if tiles_per_seq > 1:
            carry_ref[:, col] = last
    h = hs[0] if len(hs) == 1 else jnp.concatenate(hs, axis=0)
    acc_ref[...] += _dot(h.astype(BF16), wd_ref[...])

    @pl.when(j == nj - 1)
    def _():
        o_ref[...] = _layer_norm(DEEPNORM_ALPHA * x_ref[...] + acc_ref[...],
                                 g_ref[...], beta_ref[...])


def _ffn(x2, prev, lw, seq, tm, tf=512):
    m = x2.shape[0]
    nseq = m // seq
    seg = min(seq, tm)
    spt = tm // seg
    tps = seq // seg
    st_idx = lambda i, j: (i // tps, 0, 0)
    kern = functools.partial(_ffn_kernel, tf=tf, seg=seg, tiles_per_seq=tps)
    return pl.pallas_call(
        kern,
        grid=(m // tm, D_FF // tf),
        in_specs=[pl.BlockSpec((tm, D_MODEL), lambda i, j: (i, 0)),
                  pl.BlockSpec((spt, CONV_W - 1, tf), lambda i, j: (i // tps, 0, j)),
                  pl.BlockSpec((D_MODEL, tf), lambda i, j: (0, j)),
                  pl.BlockSpec((D_MODEL, tf), lambda i, j: (0, j)),
                  pl.BlockSpec((tf, D_MODEL), lambda i, j: (j, 0)),
                  pl.BlockSpec((CONV_W, tf), lambda i, j: (0, j)),
                  pl.BlockSpec((1, tf), lambda i, j: (0, j)),
                  pl.BlockSpec((1, D_MODEL), lambda i, j: (0, 0)),
                  pl.BlockSpec((1, D_MODEL), lambda i, j: (0, 0))],
        out_specs=[pl.BlockSpec((tm, D_MODEL), lambda i, j: (i, 0)),
                   pl.BlockSpec((spt, CONV_W - 1, D_FF), st_idx)],
        out_shape=[jax.ShapeDtypeStruct((m, D_MODEL), F32),
                   jax.ShapeDtypeStruct((nseq, CONV_W - 1, D_FF), F32)],
        scratch_shapes=[pltpu.VMEM((tm, D_MODEL), BF16),
                        pltpu.VMEM((tm, D_MODEL), F32),
                        pltpu.VMEM((CONV_W - 1, D_FF), F32)],
        compiler_params=_cparams(("arbitrary", "arbitrary")),
        name="ffn",
    )(x2, prev, lw['w_gate'], lw['w_up'], lw['w_down'], lw['conv_w'], lw['conv_b'],
      lw['ln2_g'], lw['ln2_b'])


def _prep_weights(w_in, mla_w_uq, mla_w_ukv):
    half = MLA_ROPE // 2
    swap = lambda w: jnp.concatenate([w[..., half:], w[..., :half]], axis=-1)
    pad = lambda w: jnp.concatenate([w, jnp.zeros_like(w)], axis=-1)
    offs = np.cumsum((512, 256, 64, 512, 512, 512, 768, 768, 768))[:-1].tolist()
    c_q, c_kv, k_r, d_q, d_k, d_v, b_q, b_k, b_v = jnp.split(w_in, offs, axis=-1)
    w_in_p = jnp.concatenate([c_q, c_kv, d_q, d_k, d_v, b_q, b_k, b_v, pad(k_r), pad(swap(k_r))],
                             axis=-1).astype(BF16)
    uq = mla_w_uq.reshape(DEPTH, MLA_Q_RANK, MLA_HEADS, MLA_NOPE + MLA_ROPE)
    nope, rope = uq[..., :MLA_NOPE], uq[..., MLA_NOPE:]
    flat = lambda w: w.reshape(DEPTH, w.shape[1], -1)
    w_uq_p = jnp.concatenate([flat(nope), flat(pad(rope)), flat(pad(swap(rope)))],
                             axis=-1).astype(BF16)
    ukv = mla_w_ukv.reshape(DEPTH, MLA_KV_RANK, MLA_HEADS, MLA_NOPE + MLA_V)
    w_ukv_p = jnp.concatenate([flat(ukv[..., :MLA_NOPE]), flat(ukv[..., MLA_NOPE:])],
                              axis=-1).astype(BF16)
    return w_in_p, w_uq_p, w_ukv_p


def _rope_tables(pos):
    half = MLA_ROPE // 2
    inv = ROPE_THETA ** (-jnp.arange(half, dtype=F32) / half)
    ang = pos.astype(F32)[:, None] * inv
    cos, sin = jnp.cos(ang), jnp.sin(ang)
    z = jnp.zeros((pos.shape[0], 128 - MLA_ROPE), F32)
    return jnp.concatenate([cos, cos, z], -1), jnp.concatenate([-sin, sin, z], -1)


def _t5_bucket(rel):
    half = T5_BUCKETS // 2
    exact = half // 2
    n = jnp.abs(rel)
    nf = jnp.maximum(n, 1).astype(F32)
    large = exact + (jnp.log(nf / exact) / math.log(T5_MAX_DIST / exact)
                     * (half - exact)).astype(jnp.int32)
    large = jnp.minimum(large, half - 1)
    return jnp.where(rel > 0, half, 0) + jnp.where(n < exact, n, large)


def _t5_bias(t5_table, rel):
    return jnp.moveaxis(t5_table[_t5_bucket(rel)], -1, 0).astype(F32)


def _band_bias(rel_table, t):
    r = jnp.arange(t, dtype=jnp.int32)[:, None] + BAND_ROWS
    c = jnp.arange(BAND_ROWS + t, dtype=jnp.int32)[None, :]
    rel = jnp.clip(r - c, -BAND_REL_CLIP, BAND_REL_CLIP) + BAND_REL_CLIP
    qc, kc = r // CHUNK, c // CHUNK
    ok = (kc <= qc) & (kc >= qc - BAND_PREV_CHUNKS)
    return jnp.where(ok[None], rel_table[:, rel].astype(F32), NEG)


def _layer(x2, nb, seq, lw, tabs, past, lam_init, tm_proj, tm_mid, tm_ffn):
    cos_t, sin_t = tabs['rope']
    (ckv, krope, dk, dv, bk, bv, dq_b, dk_b, dv_b, bq_b, bk_b, bv_b, qf, kf, v_b) = _proj(
        x2, lw, cos_t, sin_t, tm_proj)
    if past is None:
        o_a = _mla_prompt(qf, kf, v_b, nb, seq)
        o_b = _diff_prompt(dq_b, dk_b, dv_b, tabs['t5'], lw['lam'], lw['subln'], nb, seq,
                           lam_init)
        o_c = _band_prompt(bq_b, bk_b, bv_b, tabs['band'], nb, seq)
        prev = jnp.zeros((nb, CONV_W - 1, D_FF), F32)
    else:
        p_ckv, p_krope, p_dk, p_dv, p_bk, p_bv, prev = past
        plen = p_ckv.shape[1]
        kf_p, v_p = _kvup(p_ckv.reshape(nb * plen, MLA_KV_RANK),
                          p_krope.reshape(nb * plen, MLA_ROPE), lw['w_ukv'])
        o_a = _mla_sample(qf, kf_p, v_p, kf, v_b, nb, seq, plen)
        o_b = _diff_sample(dq_b, dk_b, dv_b, p_dk.reshape(nb * plen, -1),
                           p_dv.reshape(nb * plen, -1), tabs['t5'], lw['lam'], lw['subln'],
                           nb, seq, plen, lam_init)
        blen = p_bk.shape[1]
        o_c = _band_sample(bq_b, bk_b, bv_b, p_bk.reshape(nb * blen, -1),
                           p_bv.reshape(nb * blen, -1), tabs['band'], nb, seq, blen)
    x1 = _oproj_ln(o_a, o_b, o_c, x2, lw['w_o'], lw['ln1_g'], lw['ln1_b'], tm_mid)
    x_out, conv_state = _ffn(x1, prev, lw, seq, tm_ffn)
    keep = min(BAND_ROWS, seq)
    bk3 = bk.reshape(nb, seq, BAND_HEADS, BAND_DIM)[:, seq - keep:]
    bv3 = bv.reshape(nb, seq, BAND_HEADS, BAND_DIM)[:, seq - keep:]
    states = (ckv.reshape(nb, seq, MLA_KV_RANK), krope.reshape(nb, seq, MLA_ROPE),
              dk.reshape(nb, seq, DIFF_HEADS, 2 * DIFF_QK),
              dv.reshape(nb, seq, DIFF_HEADS, DIFF_V), bk3, bv3, conv_state)
    return x_out, states


def kernel(x_prompt, x_sample, cache_mla_ckv, cache_mla_krope, cache_diff_k, cache_diff_v, cache_band_k, cache_band_v, state_ffn_conv, t5_table, w_in, mla_q_norm, mla_w_uq, mla_kv_norm, mla_w_ukv, diff_lq1, diff_lk1, diff_lq2, diff_lk2, diff_subln, band_rel_table, w_o, ln1_g, ln1_b, ffn_w_gate, ffn_w_up, ffn_conv_w, ffn_conv_b, ffn_w_down, ln2_g, ln2_b):
    nb_p, seq_p, _ = x_prompt.shape
    nb_s, seq_s, _ = x_sample.shape
    past_len = cache_mla_ckv.shape[2]
    assert seq_p % ATT_T == 0 and seq_s == CHUNK and past_len % CHUNK == 0
    assert cache_band_k.shape[2] == BAND_ROWS

    w_in_p, w_uq_p, w_ukv_p = _prep_weights(w_in, mla_w_uq, mla_w_ukv)
    w_o_b = w_o.astype(BF16)
    w_gate_b, w_up_b, w_down_b = (w.astype(BF16) for w in (ffn_w_gate, ffn_w_up, ffn_w_down))

    t = ATT_T
    nq = seq_p // t
    pos_p = jnp.arange(seq_p, dtype=jnp.int32)
    pos_s = past_len + jnp.arange(seq_s, dtype=jnp.int32)
    tm_s = nb_s * seq_s
    rope_s = tuple(jnp.tile(tb, (nb_s, 1)) for tb in _rope_tables(pos_s))
    ar = jnp.arange(t, dtype=jnp.int32)
    rel_p = (ar[None, None, :] - ar[None, :, None]
             - t * jnp.arange(nq, dtype=jnp.int32)[:, None, None])
    k_pos_s = jnp.arange(past_len + seq_s, dtype=jnp.int32)
    rel_s = k_pos_s[None, :] - pos_s[:, None]
    tabs_p = {'rope': _rope_tables(pos_p), 't5': _t5_bias(t5_table, rel_p)}
    tabs_s = {'rope': rope_s, 't5': _t5_bias(t5_table, rel_s)}

    y_p = x_prompt.reshape(nb_p * seq_p, D_MODEL)
    y_s = x_sample.reshape(tm_s, D_MODEL)
    states_p, states_s = [], []
    for l in range(DEPTH):
        lam_init = 0.8 - 0.6 * math.exp(-0.3 * l)
        lw = {'w_in': w_in_p[l], 'w_uq': w_uq_p[l], 'w_ukv': w_ukv_p[l],
              'q_norm': mla_q_norm[l][None], 'kv_norm': mla_kv_norm[l][None],
              'lam': jnp.stack([diff_lq1[l], diff_lk1[l], diff_lq2[l], diff_lk2[l]]),
              'subln': diff_subln[l][None], 'w_o': w_o_b[l],
              'ln1_g': ln1_g[l][None], 'ln1_b': ln1_b[l][None],
              'w_gate': w_gate_b[l], 'w_up': w_up_b[l], 'w_down': w_down_b[l],
              'conv_w': ffn_conv_w[l], 'conv_b': ffn_conv_b[l][None],
              'ln2_g': ln2_g[l][None], 'ln2_b': ln2_b[l][None]}
        band_p = _band_bias(band_rel_table[l], t)
        band_s = _band_bias(band_rel_table[l], seq_s)
        y_p, st_p = _layer(y_p, nb_p, seq_p, lw, dict(tabs_p, band=band_p), None, lam_init,
                           tm_proj=256, tm_mid=512, tm_ffn=512)
        past = (cache_mla_ckv[l], cache_mla_krope[l], cache_diff_k[l], cache_diff_v[l],
                cache_band_k[l], cache_band_v[l], state_ffn_conv[l])
        y_s, st_s = _layer(y_s, nb_s, seq_s, lw, dict(tabs_s, band=band_s), past, lam_init,
                           tm_proj=256, tm_mid=512, tm_ffn=tm_s)
        states_p.append(st_p)
        states_s.append(st_s)
    p_states = [jnp.stack(ts) for ts in zip(*states_p)]
    s_states = [jnp.stack(ts) for ts in zip(*states_s)]
    return (y_p.reshape(nb_p, seq_p, D_MODEL), y_s.reshape(nb_s, seq_s, D_MODEL),
            *p_states, *s_states)
```

```python
import functools
import math

import jax
import jax.numpy as jnp
import numpy as np
from jax import lax
from jax.experimental import pallas as pl
from jax.experimental.pallas import tpu as pltpu

F32 = jnp.float32
BF16 = jnp.bfloat16

D_MODEL = 2048
DEPTH = 2
CHUNK = 64
MLA_HEADS = 6
MLA_Q_RANK = 512
MLA_KV_RANK = 256
MLA_NOPE = 128
MLA_ROPE = 64
MLA_V = 128
ROPE_THETA = 10000.0
DIFF_HEADS = 4
DIFF_QK = 64
DIFF_V = 128
BAND_HEADS = 6
BAND_DIM = 128
BAND_PREV_CHUNKS = 8
BAND_REL_CLIP = 256
T5_BUCKETS = 32
T5_MAX_DIST = 128
D_FF = 5632
CONV_W = 3
LN_EPS = 1e-5
RMS_EPS = 1e-6
DEEPNORM_ALPHA = (2 * DEPTH) ** 0.25

NEG = -1e30
MLA_QK = 256
MLA_SCALE = (MLA_NOPE + MLA_ROPE) ** -0.5
DIFF_SCALE = DIFF_QK ** -0.5
BAND_SCALE = BAND_DIM ** -0.5
BAND_ROWS = BAND_PREV_CHUNKS * CHUNK
ATT_T = 256
VMEM_LIMIT = 56 * 1024 * 1024

_O_CQ, _O_CKV, _O_DQ, _O_DK, _O_DV = 0, 512, 768, 1280, 1792
_O_BQ, _O_BK, _O_BV, _O_KR, _O_KRS, _IN_P = 2304, 3072, 3840, 4608, 4736, 4864
_O_MIX_DIFF = MLA_HEADS * MLA_V
_O_MIX_BAND = _O_MIX_DIFF + DIFF_HEADS * DIFF_V


def _cparams(sem):
    return pltpu.CompilerParams(dimension_semantics=sem, vmem_limit_bytes=VMEM_LIMIT)


def _dot(a, b):
    return jnp.dot(a, b, preferred_element_type=F32)


def _dot_nt(a, b):
    return lax.dot_general(a, b, (((1,), (1,)), ((), ())), preferred_element_type=F32)


def _rms(x, g):
    return x * lax.rsqrt(jnp.mean(x * x, axis=-1, keepdims=True) + RMS_EPS) * g


def _layer_norm(x, g, b):
    mu = jnp.mean(x, axis=-1, keepdims=True)
    xc = x - mu
    var = jnp.mean(xc * xc, axis=-1, keepdims=True)
    return xc * lax.rsqrt(var + LN_EPS) * g + b


def _resident(shape):
    nd = len(shape)
    return pl.BlockSpec(shape, lambda *_: (0,) * nd, pipeline_mode=pl.Buffered(1))


def _proj_kernel(x_ref, win_ref, wuq_ref, wukv_ref, qn_ref, kvn_ref, cos_ref, sin_ref,
                 ckv_o, krope_o, dk_o, dv_o, bk_o, bv_o,
                 dq_b, dk_b, dv_b, bq_b, bk_b, bv_b, qf_b, kf_b, v_b):
    xb = x_ref[...].astype(BF16)
    cos = cos_ref[...]
    sin = sin_ref[...]

    def seg(off, n):
        return _dot(xb, win_ref[:, off:off + n])

    cq = _rms(seg(_O_CQ, MLA_Q_RANK), qn_ref[...]).astype(BF16)
    nh = MLA_HEADS * 128
    for h in range(MLA_HEADS):
        lo = h * 128
        qn = _dot(cq, wuq_ref[:, lo:lo + 128])
        qr = (_dot(cq, wuq_ref[:, nh + lo:nh + lo + 128]) * cos
              + _dot(cq, wuq_ref[:, 2 * nh + lo:2 * nh + lo + 128]) * sin)
        qf_b[h, :, 0:128] = (qn * MLA_SCALE).astype(BF16)
        qf_b[h, :, 128:256] = (qr * MLA_SCALE).astype(BF16)

    ckv = _rms(seg(_O_CKV, MLA_KV_RANK), kvn_ref[...])
    ckv_o[...] = ckv
    ckvb = ckv.astype(BF16)
    kr = seg(_O_KR, 128) * cos + seg(_O_KRS, 128) * sin
    krope_o[...] = kr[:, 0:MLA_ROPE]
    krb = kr.astype(BF16)
    for h in range(MLA_HEADS):
        lo = h * 128
        kf_b[h, :, 0:128] = _dot(ckvb, wukv_ref[:, lo:lo + 128]).astype(BF16)
        kf_b[h, :, 128:256] = krb
    v_b[...] = _dot(ckvb, wukv_ref[:, nh:2 * nh]).astype(BF16)

    dq_b[...] = (seg(_O_DQ, 512) * DIFF_SCALE).astype(BF16)
    dk = seg(_O_DK, 512)
    dk_o[...] = dk
    dk_b[...] = dk.astype(BF16)
    dv = seg(_O_DV, 512)
    dv_o[...] = dv
    dv_b[...] = dv.astype(BF16)
    bq_b[...] = (seg(_O_BQ, 768) * BAND_SCALE).astype(BF16)
    bk = seg(_O_BK, 768)
    bk_o[...] = bk
    bk_b[...] = bk.astype(BF16)
    bv = seg(_O_BV, 768)
    bv_o[...] = bv
    bv_b[...] = bv.astype(BF16)


def _proj(x2, lw, cos_t, sin_t, tm):
    m = x2.shape[0]
    period = cos_t.shape[0] // tm
    row = lambda n: pl.BlockSpec((tm, n), lambda t: (t, 0))
    tab = pl.BlockSpec((tm, 128), lambda t: (t % period, 0))
    head = pl.BlockSpec((MLA_HEADS, tm, MLA_QK), lambda t: (0, t, 0))
    f32 = lambda n: jax.ShapeDtypeStruct((m, n), F32)
    b16 = lambda n: jax.ShapeDtypeStruct((m, n), BF16)
    hb16 = jax.ShapeDtypeStruct((MLA_HEADS, m, MLA_QK), BF16)
    return pl.pallas_call(
        _proj_kernel,
        grid=(m // tm,),
        in_specs=[row(D_MODEL), _resident(lw['w_in'].shape), _resident(lw['w_uq'].shape),
                  _resident(lw['w_ukv'].shape), _resident((1, MLA_Q_RANK)),
                  _resident((1, MLA_KV_RANK)), tab, tab],
        out_specs=[row(256), row(64), row(512), row(512), row(768), row(768),
                   row(512), row(512), row(512), row(768), row(768), row(768),
                   head, head, row(768)],
        out_shape=[f32(256), f32(64), f32(512), f32(512), f32(768), f32(768),
                   b16(512), b16(512), b16(512), b16(768), b16(768), b16(768),
                   hb16, hb16, b16(768)],
        compiler_params=_cparams(("arbitrary",)),
        name="proj",
    )(x2, lw['w_in'], lw['w_uq'], lw['w_ukv'], lw['q_norm'], lw['kv_norm'], cos_t, sin_t)


def _kvup_kernel(ckv_ref, kr_ref, wukv_ref, kf_b, v_b):
    ckvb = ckv_ref[...].astype(BF16)
    krb = kr_ref[...].astype(BF16)
    zeros = jnp.zeros((krb.shape[0], MLA_QK - MLA_NOPE - MLA_ROPE), BF16)
    nh = MLA_HEADS * 128
    for h in range(MLA_HEADS):
        lo = h * 128
        kf_b[h, :, 0:128] = _dot(ckvb, wukv_ref[:, lo:lo + 128]).astype(BF16)
        kf_b[h, :, 128:192] = krb
        kf_b[h, :, 192:256] = zeros
    v_b[...] = _dot(ckvb, wukv_ref[:, nh:2 * nh]).astype(BF16)


def _kvup(ckv2, kr2, w_ukv, tm=512):
    m = ckv2.shape[0]
    return pl.pallas_call(
        _kvup_kernel,
        grid=(m // tm,),
        in_specs=[pl.BlockSpec((tm, MLA_KV_RANK), lambda t: (t, 0)),
                  pl.BlockSpec((tm, MLA_ROPE), lambda t: (t, 0)),
                  _resident(w_ukv.shape)],
        out_specs=[pl.BlockSpec((MLA_HEADS, tm, MLA_QK), lambda t: (0, t, 0)),
                   pl.BlockSpec((tm, 768), lambda t: (t, 0))],
        out_shape=[jax.ShapeDtypeStruct((MLA_HEADS, m, MLA_QK), BF16),
                   jax.ShapeDtypeStruct((m, 768), BF16)],
        compiler_params=_cparams(("arbitrary",)),
        name="kvup",
    )(ckv2, kr2, w_ukv)


def _chunk_mask(t):
    r = lax.broadcasted_iota(jnp.int32, (t, t), 0) // CHUNK
    c = lax.broadcasted_iota(jnp.int32, (t, t), 1) // CHUNK
    return r >= c


def _softmax_pv(blocks):
    m = None
    for s, _ in blocks:
        bm = jnp.max(s, axis=-1, keepdims=True)
        m = bm if m is None else jnp.maximum(m, bm)
    l = None
    acc = None
    for s, vb in blocks:
        p = jnp.exp(s - m)
        bl = jnp.sum(p, axis=-1, keepdims=True)
        ba = _dot(p.astype(BF16), vb)
        l = bl if l is None else l + bl
        acc = ba if acc is None else acc + ba
    return acc / l


def _causal_attend(qs, k, v_ref, n, t, bias=None):
    mask = _chunk_mask(t)
    outs = []
    for q in qs:
        s = _dot_nt(q, k)
        if bias is not None:
            s = s + bias
        blocks = [(jnp.where(mask, s[:, n - t:], NEG), v_ref[n - t:n, :])]
        if n > t:
            blocks.append((s[:, :n - t], v_ref[0:n - t, :]))
        outs.append(_softmax_pv(blocks))
    return outs


def _mla_prompt_kernel(q_ref, k_ref, v_ref, o_ref, *, nq):
    t = ATT_T
    i = pl.program_id(2)
    for tile in range(nq):
        @pl.when(i == tile)
        def _(n=(tile + 1) * t):
            (o,) = _causal_attend([q_ref[0]], k_ref[0, 0:n, :], v_ref, n, t)
            o_ref[...] = o.astype(BF16)


def _mla_prompt(qf, kf, v, nb, seq):
    t = ATT_T
    nq = seq // t
    return pl.pallas_call(
        functools.partial(_mla_prompt_kernel, nq=nq),
        grid=(nb, MLA_HEADS, nq),
        in_specs=[pl.BlockSpec((1, t, MLA_QK), lambda b, h, i: (h, b * nq + i, 0)),
                  pl.BlockSpec((1, seq, MLA_QK), lambda b, h, i: (h, b, 0)),
                  pl.BlockSpec((seq, MLA_V), lambda b, h, i: (b, h))],
        out_specs=pl.BlockSpec((t, MLA_V), lambda b, h, i: (b * nq + i, h)),
        out_shape=jax.ShapeDtypeStruct((nb * seq, MLA_HEADS * MLA_V), BF16),
        compiler_params=_cparams(("arbitrary",) * 3),
        name="mla_prompt",
    )(qf, kf, v)


def _mla_sample_kernel(q_ref, kp_ref, vp_ref, kn_ref, vn_ref, o_ref):
    q = q_ref[0]
    o = _softmax_pv([(_dot_nt(q, kp_ref[0]), vp_ref[...]),
                     (_dot_nt(q, kn_ref[0]), vn_ref[...])])
    o_ref[...] = o.astype(BF16)


def _mla_sample(qf, kf_past, v_past, kf_new, v_new, nb, seq, past):
    return pl.pallas_call(
        _mla_sample_kernel,
        grid=(nb, MLA_HEADS),
        in_specs=[pl.BlockSpec((1, seq, MLA_QK), lambda b, h: (h, b, 0)),
                  pl.BlockSpec((1, past, MLA_QK), lambda b, h: (h, b, 0)),
                  pl.BlockSpec((past, MLA_V), lambda b, h: (b, h)),
                  pl.BlockSpec((1, seq, MLA_QK), lambda b, h: (h, b, 0)),
                  pl.BlockSpec((seq, MLA_V), lambda b, h: (b, h))],
        out_specs=pl.BlockSpec((seq, MLA_V), lambda b, h: (b, h)),
        out_shape=jax.ShapeDtypeStruct((nb * seq, MLA_HEADS * MLA_V), BF16),
        compiler_params=_cparams(("arbitrary",) * 2),
        name="mla_sample",
    )(qf, kf_past, v_past, kf_new, v_new)


def _diff_lambda(lam_ref, lam_init):
    v = lam_ref[...]
    a = jnp.sum(v[0:1] * v[1:2], axis=-1, keepdims=True)
    b = jnp.sum(v[2:3] * v[3:4], axis=-1, keepdims=True)
    return jnp.exp(a) - jnp.exp(b) + lam_init


def _split_q(q):
    lane = lax.broadcasted_iota(jnp.int32, q.shape, 1)
    qf = q.astype(F32)
    return (jnp.where(lane < DIFF_QK, qf, 0.0).astype(BF16),
            jnp.where(lane >= DIFF_QK, qf, 0.0).astype(BF16))


def _diff_finish(o0, o1, lam, g, lam_init):
    o = o0 - lam * o1
    return (_rms(o, g) * (1.0 - lam_init)).astype(BF16)


def _diff_prompt_kernel(q_ref, k_ref, v_ref, bias_ref, lam_ref, g_ref, o_ref, *, lam_init, nq):
    t = ATT_T
    seq = nq * t
    i = pl.program_id(2)
    for tile in range(nq):
        @pl.when(i == tile)
        def _(n=(tile + 1) * t):
            o0, o1 = _causal_attend(_split_q(q_ref[...]), k_ref[0:n, :], v_ref, n, t,
                                    bias=bias_ref[0, :, seq - n:seq])
            lam = _diff_lambda(lam_ref, lam_init)
            o_ref[...] = _diff_finish(o0, o1, lam, g_ref[...], lam_init)


def _diff_prompt(dq, dk, dv, bias, lam_vecs, subln, nb, seq, lam_init):
    t = ATT_T
    nq = seq // t
    return pl.pallas_call(
        functools.partial(_diff_prompt_kernel, lam_init=lam_init, nq=nq),
        grid=(nb, DIFF_HEADS, nq),
        in_specs=[pl.BlockSpec((t, 128), lambda b, h, i: (b * nq + i, h)),
                  pl.BlockSpec((seq, 128), lambda b, h, i: (b, h)),
                  pl.BlockSpec((seq, 128), lambda b, h, i: (b, h)),
                  pl.BlockSpec((1, t, seq), lambda b, h, i: (h, 0, 0)),
                  pl.BlockSpec((4, DIFF_QK), lambda b, h, i: (0, 0)),
                  pl.BlockSpec((1, DIFF_V), lambda b, h, i: (0, 0))],
        out_specs=pl.BlockSpec((t, DIFF_V), lambda b, h, i: (b * nq + i, h)),
        out_shape=jax.ShapeDtypeStruct((nb * seq, DIFF_HEADS * DIFF_V), BF16),
        compiler_params=_cparams(("arbitrary",) * 3),
        name="diff_prompt",
    )(dq, dk, dv, bias, lam_vecs, subln)


def _diff_sample_kernel(q_ref, kp_ref, vp_ref, kn_ref, vn_ref, bias_ref, lam_ref, g_ref,
                        o_ref, *, lam_init, past):
    qa, qb = _split_q(q_ref[...])
    kp = kp_ref[...].astype(BF16)
    vp = vp_ref[...].astype(BF16)
    kn = kn_ref[...]
    vn = vn_ref[...]
    bias_p = bias_ref[0, :, 0:past]
    bias_n = bias_ref[0, :, past:]
    o0 = _softmax_pv([(_dot_nt(qa, kp) + bias_p, vp), (_dot_nt(qa, kn) + bias_n, vn)])
    o1 = _softmax_pv([(_dot_nt(qb, kp) + bias_p, vp), (_dot_nt(qb, kn) + bias_n, vn)])
    lam = _diff_lambda(lam_ref, lam_init)
    o_ref[...] = _diff_finish(o0, o1, lam, g_ref[...], lam_init)


def _diff_sample(dq, dk_new, dv_new, k_past, v_past, bias, lam_vecs, subln, nb, seq, past,
                 lam_init):
    return pl.pallas_call(
        functools.partial(_diff_sample_kernel, lam_init=lam_init, past=past),
        grid=(nb, DIFF_HEADS),
        in_specs=[pl.BlockSpec((seq, 128), lambda b, h: (b, h)),
                  pl.BlockSpec((past, 128), lambda b, h: (b, h)),
                  pl.BlockSpec((past, 128), lambda b, h: (b, h)),
                  pl.BlockSpec((seq, 128), lambda b, h: (b, h)),
                  pl.BlockSpec((seq, 128), lambda b, h: (b, h)),
                  pl.BlockSpec((1, seq, past + seq), lambda b, h: (h, 0, 0)),
                  pl.BlockSpec((4, DIFF_QK), lambda b, h: (0, 0)),
                  pl.BlockSpec((1, DIFF_V), lambda b, h: (0, 0))],
        out_specs=pl.BlockSpec((seq, DIFF_V), lambda b, h: (b, h)),
        out_shape=jax.ShapeDtypeStruct((nb * seq, DIFF_HEADS * DIFF_V), BF16),
        compiler_params=_cparams(("arbitrary",) * 2),
        name="diff_sample",
    )(dq, k_past, v_past, dk_new, dv_new, bias, lam_vecs, subln)


def _band_prompt_kernel(q_ref, k_ref, v_ref, bias_ref, o_ref):
    t = ATT_T
    full = BAND_ROWS + t
    i = pl.program_id(2)
    q = q_ref[...]

    def attend(start, width):
        kw = k_ref[pl.ds(start, width), :]
        vw = v_ref[pl.ds(start, width), :]
        s = _dot_nt(q, kw) + bias_ref[0, :, full - width:full]
        o_ref[...] = _softmax_pv([(s, vw)]).astype(BF16)

    for first in range(BAND_ROWS // t):
        @pl.when(i == first)
        def _(first=first):
            attend(0, (first + 1) * t)

    @pl.when(i >= BAND_ROWS // t)
    def _():
        attend(pl.multiple_of(i * t - BAND_ROWS, t), full)


def _band_prompt(bq, bk, bv, bias, nb, seq):
    t = ATT_T
    nq = seq // t
    return pl.pallas_call(
        _band_prompt_kernel,
        grid=(nb, BAND_HEADS, nq),
        in_specs=[pl.BlockSpec((t, BAND_DIM), lambda b, h, i: (b * nq + i, h)),
                  pl.BlockSpec((seq, BAND_DIM), lambda b, h, i: (b, h)),
                  pl.BlockSpec((seq, BAND_DIM), lambda b, h, i: (b, h)),
                  pl.BlockSpec((1, t, BAND_ROWS + t), lambda b, h, i: (h, 0, 0))],
        out_specs=pl.BlockSpec((t, BAND_DIM), lambda b, h, i: (b * nq + i, h)),
        out_shape=jax.ShapeDtypeStruct((nb * seq, BAND_HEADS * BAND_DIM), BF16),
        compiler_params=_cparams(("arbitrary",) * 3),
        name="band_prompt",
    )(bq, bk, bv, bias)


def _band_sample_kernel(q_ref, kp_ref, vp_ref, kn_ref, vn_ref, bias_ref, o_ref, *, past):
    q = q_ref[...]
    kp = kp_ref[...].astype(BF16)
    vp = vp_ref[...].astype(BF16)
    o = _softmax_pv([(_dot_nt(q, kp) + bias_ref[0, :, 0:past], vp),
                     (_dot_nt(q, kn_ref[...]) + bias_ref[0, :, past:], vn_ref[...])])
    o_ref[...] = o.astype(BF16)


def _band_sample(bq, bk_new, bv_new, k_past, v_past, bias, nb, seq, past):
    return pl.pallas_call(
        functools.partial(_band_sample_kernel, past=past),
        grid=(nb, BAND_HEADS),
        in_specs=[pl.BlockSpec((seq, BAND_DIM), lambda b, h: (b, h)),
                  pl.BlockSpec((past, BAND_DIM), lambda b, h: (b, h)),
                  pl.BlockSpec((past, BAND_DIM), lambda b, h: (b, h)),
                  pl.BlockSpec((seq, BAND_DIM), lambda b, h: (b, h)),
                  pl.BlockSpec((seq, BAND_DIM), lambda b, h: (b, h)),
                  pl.BlockSpec((1, seq, past + seq), lambda b, h: (h, 0, 0))],
        out_specs=pl.BlockSpec((seq, BAND_DIM), lambda b, h: (b, h)),
        out_shape=jax.ShapeDtypeStruct((nb * seq, BAND_HEADS * BAND_DIM), BF16),
        compiler_params=_cparams(("arbitrary",) * 2),
        name="band_sample",
    )(bq, k_past, v_past, bk_new, bv_new, bias)


def _oproj_kernel(a_ref, b_ref, c_ref, x_ref, wo_ref, g_ref, beta_ref, o_ref):
    mix = (_dot(a_ref[...], wo_ref[0:_O_MIX_DIFF, :])
           + _dot(b_ref[...], wo_ref[_O_MIX_DIFF:_O_MIX_BAND, :])
           + _dot(c_ref[...], wo_ref[_O_MIX_BAND:, :]))
    o_ref[...] = _layer_norm(DEEPNORM_ALPHA * x_ref[...] + mix, g_ref[...], beta_ref[...])


def _oproj_ln(o_a, o_b, o_c, x2, w_o, g, beta, tm):
    m = x2.shape[0]
    row = lambda n: pl.BlockSpec((tm, n), lambda t: (t, 0))
    return pl.pallas_call(
        _oproj_kernel,
        grid=(m // tm,),
        in_specs=[row(o_a.shape[1]), row(o_b.shape[1]), row(o_c.shape[1]), row(D_MODEL),
                  _resident(w_o.shape), _resident((1, D_MODEL)), _resident((1, D_MODEL))],
        out_specs=row(D_MODEL),
        out_shape=jax.ShapeDtypeStruct((m, D_MODEL), F32),
        compiler_params=_cparams(("arbitrary",)),
        name="oproj_ln",
    )(o_a, o_b, o_c, x2, w_o, g, beta)


def _ffn_kernel(x_ref, prev_ref, wg_ref, wu_ref, wd_ref, cw_ref, cb_ref, g_ref, beta_ref,
                o_ref, st_ref, xb_ref, acc_ref, carry_ref, *, tf, seg, tiles_per_seq):
    i = pl.program_id(0)
    j = pl.program_id(1)
    nj = pl.num_programs(1)
    tm = x_ref.shape[0]
    col = pl.ds(pl.multiple_of(j * tf, tf), tf)

    @pl.when(j == 0)
    def _():
        xb_ref[...] = x_ref[...].astype(BF16)
        acc_ref[...] = jnp.zeros_like(acc_ref)

    xb = xb_ref[...]
    g = _dot(xb, wg_ref[...])
    u = _dot(xb, wu_ref[...])
    cw = cw_ref[...]
    cb = cb_ref[...]
    rows = lax.broadcasted_iota(jnp.int32, (seg, tf), 0)
    seq_start = (i % tiles_per_seq) == 0

    hs = []
    for s in range(tm // seg):
        gs = g[s * seg:(s + 1) * seg]
        state = prev_ref[s]
        if tiles_per_seq > 1:
            state = jnp.where(seq_start, state, carry_ref[:, col])
        pm2, pm1 = state[0:1], state[1:2]
        gm1 = jnp.where(rows == 0, pm1, pltpu.roll(gs, 1, 0))
        gm2 = jnp.where(rows == 0, pm2, jnp.where(rows == 1, pm1, pltpu.roll(gs, 2, 0)))
        gc = cb + cw[0:1] * gm2 + cw[1:2] * gm1 + cw[2:3] * gs
        hs.append(gc * (1.0 / (1.0 + jnp.exp(-gc))) * u[s * seg:(s + 1) * seg])
        last = gs[seg - 2:seg]
        st_ref[s, :, col] = last
        if tiles_per_seq > 1:
            carry_ref[:, col] = last
    h = hs[0] if len(hs) == 1 else jnp.concatenate(hs, axis=0)
    acc_ref[...] += _dot(h.astype(BF16), wd_ref[...])

    @pl.when(j == nj - 1)
    def _():
        o_ref[...] = _layer_norm(DEEPNORM_ALPHA * x_ref[...] + acc_ref[...],
                                 g_ref[...], beta_ref[...])


def _ffn(x2, prev, lw, seq, tm, tf=512):
    m = x2.shape[0]
    nseq = m // seq
    seg = min(seq, tm)
    spt = tm // seg
    tps = seq // seg
    st_idx = lambda i, j: (i // tps, 0, 0)
    kern = functools.partial(_ffn_kernel, tf=tf, seg=seg, tiles_per_seq=tps)
    return pl.pallas_call(
        kern,
        grid=(m // tm, D_FF // tf),
        in_specs=[pl.BlockSpec((tm, D_MODEL), lambda i, j: (i, 0)),
                  pl.BlockSpec((spt, CONV_W - 1, tf), lambda i, j: (i // tps, 0, j)),
                  pl.BlockSpec((D_MODEL, tf), lambda i, j: (0, j)),
                  pl.BlockSpec((D_MODEL, tf), lambda i, j: (0, j)),
                  pl.BlockSpec((tf, D_MODEL), lambda i, j: (j, 0)),
                  pl.BlockSpec((CONV_W, tf), lambda i, j: (0, j)),
                  pl.BlockSpec((1, tf), lambda i, j: (0, j)),
                  pl.BlockSpec((1, D_MODEL), lambda i, j: (0, 0)),
                  pl.BlockSpec((1, D_MODEL), lambda i, j: (0, 0))],
        out_specs=[pl.BlockSpec((tm, D_MODEL), lambda i, j: (i, 0)),
                   pl.BlockSpec((spt, CONV_W - 1, D_FF), st_idx)],
        out_shape=[jax.ShapeDtypeStruct((m, D_MODEL), F32),
                   jax.ShapeDtypeStruct((nseq, CONV_W - 1, D_FF), F32)],
        scratch_shapes=[pltpu.VMEM((tm, D_MODEL), BF16),
                        pltpu.VMEM((tm, D_MODEL), F32),
                        pltpu.VMEM((CONV_W - 1, D_FF), F32)],
        compiler_params=_cparams(("arbitrary", "arbitrary")),
        name="ffn",
    )(x2, prev, lw['w_gate'], lw['w_up'], lw['w_down'], lw['conv_w'], lw['conv_b'],
      lw['ln2_g'], lw['ln2_b'])


def _prep_weights(w_in, mla_w_uq, mla_w_ukv):
    half = MLA_ROPE // 2
    swap = lambda w: jnp.concatenate([w[..., half:], w[..., :half]], axis=-1)
    pad = lambda w: jnp.concatenate([w, jnp.zeros_like(w)], axis=-1)
    offs = np.cumsum((512, 256, 64, 512, 512, 512, 768, 768, 768))[:-1].tolist()
    c_q, c_kv, k_r, d_q, d_k, d_v, b_q, b_k, b_v = jnp.split(w_in, offs, axis=-1)
    w_in_p = jnp.concatenate([c_q, c_kv, d_q, d_k, d_v, b_q, b_k, b_v, pad(k_r), pad(swap(k_r))],
                             axis=-1).astype(BF16)
    uq = mla_w_uq.reshape(DEPTH, MLA_Q_RANK, MLA_HEADS, MLA_NOPE + MLA_ROPE)
    nope, rope = uq[..., :MLA_NOPE], uq[..., MLA_NOPE:]
    flat = lambda w: w.reshape(DEPTH, w.shape[1], -1)
    w_uq_p = jnp.concatenate([flat(nope), flat(pad(rope)), flat(pad(swap(rope)))],
                             axis=-1).astype(BF16)
    ukv = mla_w_ukv.reshape(DEPTH, MLA_KV_RANK, MLA_HEADS, MLA_NOPE + MLA_V)
    w_ukv_p = jnp.concatenate([flat(ukv[..., :MLA_NOPE]), flat(ukv[..., MLA_NOPE:])],
                              axis=-1).astype(BF16)
    return w_in_p, w_uq_p, w_ukv_p


def _rope_tables(pos):
    half = MLA_ROPE // 2
    inv = ROPE_THETA ** (-jnp.arange(half, dtype=F32) / half)
    ang = pos.astype(F32)[:, None] * inv
    cos, sin = jnp.cos(ang), jnp.sin(ang)
    z = jnp.zeros((pos.shape[0], 128 - MLA_ROPE), F32)
    return jnp.concatenate([cos, cos, z], -1), jnp.concatenate([-sin, sin, z], -1)


def _t5_bucket(rel):
    half = T5_BUCKETS // 2
    exact = half // 2
    n = jnp.abs(rel)
    nf = jnp.maximum(n, 1).astype(F32)
    large = exact + (jnp.log(nf / exact) / math.log(T5_MAX_DIST / exact)
                     * (half - exact)).astype(jnp.int32)
    large = jnp.minimum(large, half - 1)
    return jnp.where(rel > 0, half, 0) + jnp.where(n < exact, n, large)


def _toeplitz(fn, rows, cols, shift):
    p = rows + cols
    k = jnp.arange(p, dtype=jnp.int32)
    u = fn(jnp.where(k < cols, k, k - p) + shift).astype(F32)
    flat = jnp.tile(u, (1, rows))[:, :rows * (p - 1)]
    return flat.reshape(u.shape[0], rows, p - 1)[:, :, :cols]


def _t5_bias(t5_table, rows, cols, shift):
    return _toeplitz(lambda rel: t5_table[_t5_bucket(rel)].T, rows, cols, shift)


def _band_bias(rel_table, t):
    def fn(d):
        rel = jnp.clip(BAND_ROWS - d, -BAND_REL_CLIP, BAND_REL_CLIP) + BAND_REL_CLIP
        return rel_table[:, rel]
    bias = _toeplitz(fn, t, BAND_ROWS + t, 0)
    qc = (jnp.arange(t, dtype=jnp.int32)[:, None] + BAND_ROWS) // CHUNK
    kc = jnp.arange(BAND_ROWS + t, dtype=jnp.int32)[None, :] // CHUNK
    ok = (kc <= qc) & (kc >= qc - BAND_PREV_CHUNKS)
    return jnp.where(ok[None], bias, NEG)


def _layer(x2, nb, seq, lw, tabs, past, lam_init, tm_proj, tm_mid, tm_ffn):
    cos_t, sin_t = tabs['rope']
    (ckv, krope, dk, dv, bk, bv, dq_b, dk_b, dv_b, bq_b, bk_b, bv_b, qf, kf, v_b) = _proj(
        x2, lw, cos_t, sin_t, tm_proj)
    if past is None:
        o_a = _mla_prompt(qf, kf, v_b, nb, seq)
        o_b = _diff_prompt(dq_b, dk_b, dv_b, tabs['t5'], lw['lam'], lw['subln'], nb, seq,
                           lam_init)
        o_c = _band_prompt(bq_b, bk_b, bv_b, tabs['band'], nb, seq)
        prev = jnp.zeros((nb, CONV_W - 1, D_FF), F32)
    else:
        p_ckv, p_krope, p_dk, p_dv, p_bk, p_bv, prev = past
        plen = p_ckv.shape[1]
        kf_p, v_p = _kvup(p_ckv.reshape(nb * plen, MLA_KV_RANK),
                          p_krope.reshape(nb * plen, MLA_ROPE), lw['w_ukv'])
        o_a = _mla_sample(qf, kf_p, v_p, kf, v_b, nb, seq, plen)
        o_b = _diff_sample(dq_b, dk_b, dv_b, p_dk.reshape(nb * plen, -1),
                           p_dv.reshape(nb * plen, -1), tabs['t5'], lw['lam'], lw['subln'],
                           nb, seq, plen, lam_init)
        blen = p_bk.shape[1]
        o_c = _band_sample(bq_b, bk_b, bv_b, p_bk.reshape(nb * blen, -1),
                           p_bv.reshape(nb * blen, -1), tabs['band'], nb, seq, blen)
    x1 = _oproj_ln(o_a, o_b, o_c, x2, lw['w_o'], lw['ln1_g'], lw['ln1_b'], tm_mid)
    x_out, conv_state = _ffn(x1, prev, lw, seq, tm_ffn)
    keep = min(BAND_ROWS, seq)
    bk3 = bk.reshape(nb, seq, BAND_HEADS, BAND_DIM)[:, seq - keep:]
    bv3 = bv.reshape(nb, seq, BAND_HEADS, BAND_DIM)[:, seq - keep:]
    states = (ckv.reshape(nb, seq, MLA_KV_RANK), krope.reshape(nb, seq, MLA_ROPE),
              dk.reshape(nb, seq, DIFF_HEADS, 2 * DIFF_QK),
              dv.reshape(nb, seq, DIFF_HEADS, DIFF_V), bk3, bv3, conv_state)
    return x_out, states


def kernel(x_prompt, x_sample, cache_mla_ckv, cache_mla_krope, cache_diff_k, cache_diff_v, cache_band_k, cache_band_v, state_ffn_conv, t5_table, w_in, mla_q_norm, mla_w_uq, mla_kv_norm, mla_w_ukv, diff_lq1, diff_lk1, diff_lq2, diff_lk2, diff_subln, band_rel_table, w_o, ln1_g, ln1_b, ffn_w_gate, ffn_w_up, ffn_conv_w, ffn_conv_b, ffn_w_down, ln2_g, ln2_b):
    nb_p, seq_p, _ = x_prompt.shape
    nb_s, seq_s, _ = x_sample.shape
    past_len = cache_mla_ckv.shape[2]
    assert seq_p % ATT_T == 0 and seq_s == CHUNK and past_len % CHUNK == 0
    assert cache_band_k.shape[2] == BAND_ROWS

    w_in_p, w_uq_p, w_ukv_p = _prep_weights(w_in, mla_w_uq, mla_w_ukv)
    w_o_b = w_o.astype(BF16)
    w_gate_b, w_up_b, w_down_b = (w.astype(BF16) for w in (ffn_w_gate, ffn_w_up, ffn_w_down))

    t = ATT_T
    nq = seq_p // t
    pos_p = jnp.arange(seq_p, dtype=jnp.int32)
    pos_s = past_len + jnp.arange(seq_s, dtype=jnp.int32)
    tm_s = nb_s * seq_s
    rope_s = tuple(jnp.tile(tb, (nb_s, 1)) for tb in _rope_tables(pos_s))
    tabs_p = {'rope': _rope_tables(pos_p), 't5': _t5_bias(t5_table, t, seq_p, t - seq_p)}
    tabs_s = {'rope': rope_s, 't5': _t5_bias(t5_table, seq_s, past_len + seq_s, -past_len)}

    y_p = x_prompt.reshape(nb_p * seq_p, D_MODEL)
    y_s = x_sample.reshape(tm_s, D_MODEL)
    states_p, states_s = [], []
    for l in range(DEPTH):
        lam_init = 0.8 - 0.6 * math.exp(-0.3 * l)
        lw = {'w_in': w_in_p[l], 'w_uq': w_uq_p[l], 'w_ukv': w_ukv_p[l],
              'q_norm': mla_q_norm[l][None], 'kv_norm': mla_kv_norm[l][None],
              'lam': jnp.stack([diff_lq1[l], diff_lk1[l], diff_lq2[l], diff_lk2[l]]),
              'subln': diff_subln[l][None], 'w_o': w_o_b[l],
              'ln1_g': ln1_g[l][None], 'ln1_b': ln1_b[l][None],
              'w_gate': w_gate_b[l], 'w_up': w_up_b[l], 'w_down': w_down_b[l],
              'conv_w': ffn_conv_w[l], 'conv_b': ffn_conv_b[l][None],
              'ln2_g': ln2_g[l][None], 'ln2_b': ln2_b[l][None]}
        band_p = _band_bias(band_rel_table[l], t)
        band_s = _band_bias(band_rel_table[l], seq_s)
        y_p, st_p = _layer(y_p, nb_p, seq_p, lw, dict(tabs_p, band=band_p), None, lam_init,
                           tm_proj=256, tm_mid=512, tm_ffn=512)
        past = (cache_mla_ckv[l], cache_mla_krope[l], cache_diff_k[l], cache_diff_v[l],
                cache_band_k[l], cache_band_v[l], state_ffn_conv[l])
        y_s, st_s = _layer(y_s, nb_s, seq_s, lw, dict(tabs_s, band=band_s), past, lam_init,
                           tm_proj=256, tm_mid=512, tm_ffn=tm_s)
        states_p.append(st_p)
        states_s.append(st_s)
    p_states = [jnp.stack(ts) for ts in zip(*states_p)]
    s_states = [jnp.stack(ts) for ts in zip(*states_s)]
    return (y_p.reshape(nb_p, seq_p, D_MODEL), y_s.reshape(nb_s, seq_s, D_MODEL),
            *p_states, *s_states)
```

```python
import functools
import math

import jax
import jax.numpy as jnp
import numpy as np
from jax import lax
from jax.experimental import pallas as pl
from jax.experimental.pallas import tpu as pltpu

F32 = jnp.float32
BF16 = jnp.bfloat16

D_MODEL = 2048
DEPTH = 2
CHUNK = 64
MLA_HEADS = 6
MLA_Q_RANK = 512
MLA_KV_RANK = 256
MLA_NOPE = 128
MLA_ROPE = 64
MLA_V = 128
ROPE_THETA = 10000.0
DIFF_HEADS = 4
DIFF_QK = 64
DIFF_V = 128
BAND_HEADS = 6
BAND_DIM = 128
BAND_PREV_CHUNKS = 8
BAND_REL_CLIP = 256
T5_BUCKETS = 32
T5_MAX_DIST = 128
D_FF = 5632
CONV_W = 3
LN_EPS = 1e-5
RMS_EPS = 1e-6
DEEPNORM_ALPHA = (2 * DEPTH) ** 0.25

NEG = -1e30
MLA_QK = 256
MLA_SCALE = (MLA_NOPE + MLA_ROPE) ** -0.5
DIFF_SCALE = DIFF_QK ** -0.5
BAND_SCALE = BAND_DIM ** -0.5
BAND_ROWS = BAND_PREV_CHUNKS * CHUNK
ATT_T = 256
VMEM_LIMIT = 56 * 1024 * 1024

_O_CQ, _O_CKV, _O_DQ, _O_DK, _O_DV = 0, 512, 768, 1280, 1792
_O_BQ, _O_BK, _O_BV, _O_KR, _O_KRS, _IN_P = 2304, 3072, 3840, 4608, 4736, 4864
_O_MIX_DIFF = MLA_HEADS * MLA_V
_O_MIX_BAND = _O_MIX_DIFF + DIFF_HEADS * DIFF_V
_N_STATE = 6


def _cparams(sem):
    return pltpu.CompilerParams(dimension_semantics=sem, vmem_limit_bytes=VMEM_LIMIT)


def _dot(a, b):
    return jnp.dot(a, b, preferred_element_type=F32)


def _dot_nt(a, b):
    return lax.dot_general(a, b, (((1,), (1,)), ((), ())), preferred_element_type=F32)


def _rms(x, g):
    return x * lax.rsqrt(jnp.mean(x * x, axis=-1, keepdims=True) + RMS_EPS) * g


def _layer_norm(x, g, b):
    mu = jnp.mean(x, axis=-1, keepdims=True)
    xc = x - mu
    var = jnp.mean(xc * xc, axis=-1, keepdims=True)
    return xc * lax.rsqrt(var + LN_EPS) * g + b


def _layer_block(arr, l):
    zeros = (0,) * (arr.ndim - 1)
    return pl.BlockSpec((None,) + arr.shape[1:], lambda *_: (l,) + zeros,
                        pipeline_mode=pl.Buffered(1))


def _heads_to_rows(o_ref, val, heads):
    rows = val.shape[0]
    for h in range(heads):
        o_ref[pl.ds(h, rows, stride=heads), :] = val[:, h * 128:(h + 1) * 128]


def _head_rows(ref, h, rows, heads):
    return ref[pl.ds(h, rows, stride=heads), :]


def _proj_kernel(*refs, n_alias, tiles_per_seq, band_from):
    (x_ref, win_ref, wuq_ref, wukv_ref, qn_ref, kvn_ref, cos_ref, sin_ref) = refs[:8]
    (ckv_o, krope_o, dk_o, dv_o, bk_o, bv_o,
     dq_b, dk_b, dv_b, bq_b, bk_b, bv_b, qf_b, kf_b, v_b) = refs[8 + n_alias:]
    xb = x_ref[...].astype(BF16)
    cos = cos_ref[...]
    sin = sin_ref[...]

    def seg(off, n):
        return _dot(xb, win_ref[:, off:off + n])

    cq = _rms(seg(_O_CQ, MLA_Q_RANK), qn_ref[...]).astype(BF16)
    nh = MLA_HEADS * 128
    for h in range(MLA_HEADS):
        lo = h * 128
        qn = _dot(cq, wuq_ref[:, lo:lo + 128])
        qr = (_dot(cq, wuq_ref[:, nh + lo:nh + lo + 128]) * cos
              + _dot(cq, wuq_ref[:, 2 * nh + lo:2 * nh + lo + 128]) * sin)
        qf_b[h, :, 0:128] = (qn * MLA_SCALE).astype(BF16)
        qf_b[h, :, 128:256] = (qr * MLA_SCALE).astype(BF16)

    ckv = _rms(seg(_O_CKV, MLA_KV_RANK), kvn_ref[...])
    ckv_o[...] = ckv
    ckvb = ckv.astype(BF16)
    kr = seg(_O_KR, 128) * cos + seg(_O_KRS, 128) * sin
    krope_o[...] = kr[:, 0:MLA_ROPE]
    krb = kr.astype(BF16)
    for h in range(MLA_HEADS):
        lo = h * 128
        kf_b[h, :, 0:128] = _dot(ckvb, wukv_ref[:, lo:lo + 128]).astype(BF16)
        kf_b[h, :, 128:256] = krb
    v_b[...] = _dot(ckvb, wukv_ref[:, nh:2 * nh]).astype(BF16)

    dq_b[...] = (seg(_O_DQ, 512) * DIFF_SCALE).astype(BF16)
    dk = seg(_O_DK, 512)
    _heads_to_rows(dk_o, dk, DIFF_HEADS)
    dk_b[...] = dk.astype(BF16)
    dv = seg(_O_DV, 512)
    _heads_to_rows(dv_o, dv, DIFF_HEADS)
    dv_b[...] = dv.astype(BF16)
    bq_b[...] = (seg(_O_BQ, 768) * BAND_SCALE).astype(BF16)
    bk = seg(_O_BK, 768)
    bk_b[...] = bk.astype(BF16)
    bv = seg(_O_BV, 768)
    bv_b[...] = bv.astype(BF16)

    def band_state():
        _heads_to_rows(bk_o, bk, BAND_HEADS)
        _heads_to_rows(bv_o, bv, BAND_HEADS)

    if band_from == 0:
        band_state()
    else:
        pl.when(pl.program_id(0) % tiles_per_seq >= band_from)(band_state)


def _proj(x2, wts, l, cos_t, sin_t, tm, seq, state_bufs):
    m = x2.shape[0]
    period = cos_t.shape[0] // tm
    keep = min(BAND_ROWS, seq)
    if seq > tm:
        tps, kt = seq // tm, keep // tm
        band_from = tps - kt
        band_idx = lambda t: (l, (t // tps) * kt + jnp.maximum(t % tps - band_from, 0), 0)
    else:
        assert keep == seq
        tps, band_from = 1, 0
        band_idx = lambda t: (l, t, 0)
    row = lambda n: pl.BlockSpec((tm, n), lambda t: (t, 0))
    srow = lambda n, k=1: pl.BlockSpec((None, tm * k, n), lambda t: (l, t, 0))
    band = pl.BlockSpec((None, tm * BAND_HEADS, 128), band_idx)
    tab = pl.BlockSpec((tm, 128), lambda t: (t % period, 0))
    head = pl.BlockSpec((MLA_HEADS, tm, MLA_QK), lambda t: (0, t, 0))
    st = lambda rows, n: jax.ShapeDtypeStruct((DEPTH, rows, n), F32)
    b16 = lambda n: jax.ShapeDtypeStruct((m, n), BF16)
    hb16 = jax.ShapeDtypeStruct((MLA_HEADS, m, MLA_QK), BF16)
    n_band = (m // seq) * keep * BAND_HEADS
    n_alias = 0 if state_bufs is None else _N_STATE
    kern = functools.partial(_proj_kernel, n_alias=n_alias, tiles_per_seq=tps,
                             band_from=band_from)
    ins = [x2, wts['w_in'], wts['w_uq'], wts['w_ukv'], wts['q_norm'], wts['kv_norm'],
           cos_t, sin_t]
    in_specs = [row(D_MODEL), _layer_block(wts['w_in'], l), _layer_block(wts['w_uq'], l),
                _layer_block(wts['w_ukv'], l), _layer_block(wts['q_norm'], l),
                _layer_block(wts['kv_norm'], l), tab, tab]
    if n_alias:
        ins += list(state_bufs)
        in_specs += [pl.BlockSpec(memory_space=pl.ANY)] * n_alias
    outs = pl.pallas_call(
        kern,
        grid=(m // tm,),
        in_specs=in_specs,
        out_specs=[srow(256), srow(64), srow(128, DIFF_HEADS), srow(128, DIFF_HEADS), band, band,
                   row(512), row(512), row(512), row(768), row(768), row(768),
                   head, head, row(768)],
        out_shape=[st(m, 256), st(m, 64), st(m * DIFF_HEADS, 128), st(m * DIFF_HEADS, 128),
                   st(n_band, 128), st(n_band, 128),
                   b16(512), b16(512), b16(512), b16(768), b16(768), b16(768),
                   hb16, hb16, b16(768)],
        input_output_aliases={8 + k: k for k in range(n_alias)},
        compiler_params=_cparams(("arbitrary",)),
        name="proj",
    )(*ins)
    return outs[:_N_STATE], outs[_N_STATE:]


def _kvup_kernel(ckv_ref, kr_ref, wukv_ref, kf_b, v_b):
    ckvb = ckv_ref[...].astype(BF16)
    krb = kr_ref[...].astype(BF16)
    zeros = jnp.zeros((krb.shape[0], MLA_QK - MLA_NOPE - MLA_ROPE), BF16)
    nh = MLA_HEADS * 128
    for h in range(MLA_HEADS):
        lo = h * 128
        kf_b[h, :, 0:128] = _dot(ckvb, wukv_ref[:, lo:lo + 128]).astype(BF16)
        kf_b[h, :, 128:192] = krb
        kf_b[h, :, 192:256] = zeros
    v_b[...] = _dot(ckvb, wukv_ref[:, nh:2 * nh]).astype(BF16)


def _kvup(ckv3, kr3, w_ukv, l, tm=512):
    m = ckv3.shape[1]
    return pl.pallas_call(
        _kvup_kernel,
        grid=(m // tm,),
        in_specs=[pl.BlockSpec((None, tm, MLA_KV_RANK), lambda t: (l, t, 0)),
                  pl.BlockSpec((None, tm, MLA_ROPE), lambda t: (l, t, 0)),
                  _layer_block(w_ukv, l)],
        out_specs=[pl.BlockSpec((MLA_HEADS, tm, MLA_QK), lambda t: (0, t, 0)),
                   pl.BlockSpec((tm, 768), lambda t: (t, 0))],
        out_shape=[jax.ShapeDtypeStruct((MLA_HEADS, m, MLA_QK), BF16),
                   jax.ShapeDtypeStruct((m, 768), BF16)],
        compiler_params=_cparams(("arbitrary",)),
        name="kvup",
    )(ckv3, kr3, w_ukv)


def _chunk_mask(t):
    r = lax.broadcasted_iota(jnp.int32, (t, t), 0) // CHUNK
    c = lax.broadcasted_iota(jnp.int32, (t, t), 1) // CHUNK
    return r >= c


def _softmax_pv(blocks):
    m = None
    for s, _ in blocks:
        bm = jnp.max(s, axis=-1, keepdims=True)
        m = bm if m is None else jnp.maximum(m, bm)
    l = None
    acc = None
    for s, vb in blocks:
        p = jnp.exp(s - m)
        bl = jnp.sum(p, axis=-1, keepdims=True)
        ba = _dot(p.astype(BF16), vb)
        l = bl if l is None else l + bl
        acc = ba if acc is None else acc + ba
    return acc / l


def _causal_attend(qs, k, v_ref, n, t, bias=None):
    mask = _chunk_mask(t)
    outs = []
    for q in qs:
        s = _dot_nt(q, k)
        if bias is not None:
            s = s + bias
        blocks = [(jnp.where(mask, s[:, n - t:], NEG), v_ref[n - t:n, :])]
        if n > t:
            blocks.append((s[:, :n - t], v_ref[0:n - t, :]))
        outs.append(_softmax_pv(blocks))
    return outs


def _mla_prompt_kernel(q_ref, k_ref, v_ref, o_ref, *, nq):
    t = ATT_T
    for tile in range(nq):
        n = (tile + 1) * t
        (o,) = _causal_attend([q_ref[0, n - t:n, :]], k_ref[0, 0:n, :], v_ref, n, t)
        o_ref[n - t:n, :] = o.astype(BF16)


def _mla_prompt(qf, kf, v, nb, seq):
    return pl.pallas_call(
        functools.partial(_mla_prompt_kernel, nq=seq // ATT_T),
        grid=(nb, MLA_HEADS),
        in_specs=[pl.BlockSpec((1, seq, MLA_QK), lambda b, h: (h, b, 0)),
                  pl.BlockSpec((1, seq, MLA_QK), lambda b, h: (h, b, 0)),
                  pl.BlockSpec((seq, MLA_V), lambda b, h: (b, h))],
        out_specs=pl.BlockSpec((seq, MLA_V), lambda b, h: (b, h)),
        out_shape=jax.ShapeDtypeStruct((nb * seq, MLA_HEADS * MLA_V), BF16),
        compiler_params=_cparams(("arbitrary",) * 2),
        name="mla_prompt",
    )(qf, kf, v)


def _mla_sample_kernel(q_ref, kp_ref, vp_ref, kn_ref, vn_ref, o_ref):
    q = q_ref[0]
    o = _softmax_pv([(_dot_nt(q, kp_ref[0]), vp_ref[...]),
                     (_dot_nt(q, kn_ref[0]), vn_ref[...])])
    o_ref[...] = o.astype(BF16)


def _mla_sample(qf, kf_past, v_past, kf_new, v_new, nb, seq, past):
    return pl.pallas_call(
        _mla_sample_kernel,
        grid=(nb, MLA_HEADS),
        in_specs=[pl.BlockSpec((1, seq, MLA_QK), lambda b, h: (h, b, 0)),
                  pl.BlockSpec((1, past, MLA_QK), lambda b, h: (h, b, 0)),
                  pl.BlockSpec((past, MLA_V), lambda b, h: (b, h)),
                  pl.BlockSpec((1, seq, MLA_QK), lambda b, h: (h, b, 0)),
                  pl.BlockSpec((seq, MLA_V), lambda b, h: (b, h))],
        out_specs=pl.BlockSpec((seq, MLA_V), lambda b, h: (b, h)),
        out_shape=jax.ShapeDtypeStruct((nb * seq, MLA_HEADS * MLA_V), BF16),
        compiler_params=_cparams(("arbitrary",) * 2),
        name="mla_sample",
    )(qf, kf_past, v_past, kf_new, v_new)


def _diff_lambda(lam_ref, lam_init):
    v = lam_ref[...]
    a = jnp.sum(v[0:1] * v[1:2], axis=-1, keepdims=True)
    b = jnp.sum(v[2:3] * v[3:4], axis=-1, keepdims=True)
    return jnp.exp(a) - jnp.exp(b) + lam_init


def _split_q(q):
    lane = lax.broadcasted_iota(jnp.int32, q.shape, 1)
    qf = q.astype(F32)
    return (jnp.where(lane < DIFF_QK, qf, 0.0).astype(BF16),
            jnp.where(lane >= DIFF_QK, qf, 0.0).astype(BF16))


def _diff_finish(o0, o1, lam, g, lam_init):
    o = o0 - lam * o1
    return (_rms(o, g) * (1.0 - lam_init)).astype(BF16)


def _diff_prompt_kernel(q_ref, k_ref, v_ref, bias_ref, lam_ref, g_ref, o_ref, *, lam_init, nq):
    t = ATT_T
    seq = nq * t
    lam = _diff_lambda(lam_ref, lam_init)
    for tile in range(nq):
        n = (tile + 1) * t
        o0, o1 = _causal_attend(_split_q(q_ref[n - t:n, :]), k_ref[0:n, :], v_ref, n, t,
                                bias=bias_ref[0, :, seq - n:seq])
        o_ref[n - t:n, :] = _diff_finish(o0, o1, lam, g_ref[...], lam_init)


def _diff_prompt(dq, dk, dv, bias, wts, l, nb, seq, lam_init):
    t = ATT_T
    return pl.pallas_call(
        functools.partial(_diff_prompt_kernel, lam_init=lam_init, nq=seq // t),
        grid=(nb, DIFF_HEADS),
        in_specs=[pl.BlockSpec((seq, 128), lambda b, h: (b, h)),
                  pl.BlockSpec((seq, 128), lambda b, h: (b, h)),
                  pl.BlockSpec((seq, 128), lambda b, h: (b, h)),
                  pl.BlockSpec((1, t, seq), lambda b, h: (h, 0, 0)),
                  _layer_block(wts['lam'], l), _layer_block(wts['subln'], l)],
        out_specs=pl.BlockSpec((seq, DIFF_V), lambda b, h: (b, h)),
        out_shape=jax.ShapeDtypeStruct((nb * seq, DIFF_HEADS * DIFF_V), BF16),
        compiler_params=_cparams(("arbitrary",) * 2),
        name="diff_prompt",
    )(dq, dk, dv, bias, wts['lam'], wts['subln'])


def _diff_sample_kernel(q_ref, kp_ref, vp_ref, kn_ref, vn_ref, bias_ref, lam_ref, g_ref,
                        o_ref, *, lam_init, past):
    lam = _diff_lambda(lam_ref, lam_init)
    for h in range(DIFF_HEADS):
        cols = slice(h * 128, (h + 1) * 128)
        kp = _head_rows(kp_ref, h, past, DIFF_HEADS).astype(BF16)
        vp = _head_rows(vp_ref, h, past, DIFF_HEADS).astype(BF16)
        kn = kn_ref[:, cols]
        vn = vn_ref[:, cols]
        bias_p = bias_ref[h, :, 0:past]
        bias_n = bias_ref[h, :, past:]
        outs = [_softmax_pv([(_dot_nt(q, kp) + bias_p, vp), (_dot_nt(q, kn) + bias_n, vn)])
                for q in _split_q(q_ref[:, cols])]
        o_ref[:, cols] = _diff_finish(outs[0], outs[1], lam, g_ref[...], lam_init)


def _diff_sample(dq, dk_new, dv_new, k_past, v_past, bias, wts, l, nb, seq, past, lam_init):
    width = DIFF_HEADS * 128
    new = pl.BlockSpec((seq, width), lambda b: (b, 0))
    cache = pl.BlockSpec((None, None, past * DIFF_HEADS, 128), lambda b: (l, b, 0, 0))
    return pl.pallas_call(
        functools.partial(_diff_sample_kernel, lam_init=lam_init, past=past),
        grid=(nb,),
        in_specs=[new, cache, cache, new, new,
                  pl.BlockSpec(bias.shape, lambda b: (0, 0, 0)),
                  _layer_block(wts['lam'], l), _layer_block(wts['subln'], l)],
        out_specs=new,
        out_shape=jax.ShapeDtypeStruct((nb * seq, width), BF16),
        compiler_params=_cparams(("arbitrary",)),
        name="diff_sample",
    )(dq, k_past, v_past, dk_new, dv_new, bias, wts['lam'], wts['subln'])


def _band_prompt_kernel(q_ref, k_ref, v_ref, bias_ref, o_ref, *, nq):
    t = ATT_T
    full = BAND_ROWS + t
    for tile in range(nq):
        start = max(0, tile * t - BAND_ROWS)
        width = (tile + 1) * t - start
        s = (_dot_nt(q_ref[tile * t:(tile + 1) * t, :], k_ref[start:start + width, :])
             + bias_ref[0, :, full - width:full])
        o = _softmax_pv([(s, v_ref[start:start + width, :])])
        o_ref[tile * t:(tile + 1) * t, :] = o.astype(BF16)


def _band_prompt(bq, bk, bv, bias, nb, seq):
    t = ATT_T
    blk = pl.BlockSpec((seq, BAND_DIM), lambda b, h: (b, h))
    return pl.pallas_call(
        functools.partial(_band_prompt_kernel, nq=seq // t),
        grid=(nb, BAND_HEADS),
        in_specs=[blk, blk, blk,
                  pl.BlockSpec((1, t, BAND_ROWS + t), lambda b, h: (h, 0, 0))],
        out_specs=blk,
        out_shape=jax.ShapeDtypeStruct((nb * seq, BAND_HEADS * BAND_DIM), BF16),
        compiler_params=_cparams(("arbitrary",) * 2),
        name="band_prompt",
    )(bq, bk, bv, bias)


def _band_sample_kernel(q_ref, kp_ref, vp_ref, kn_ref, vn_ref, bias_ref, o_ref, *, past):
    for h in range(BAND_HEADS):
        cols = slice(h * 128, (h + 1) * 128)
        kp = _head_rows(kp_ref, h, past, BAND_HEADS).astype(BF16)
        vp = _head_rows(vp_ref, h, past, BAND_HEADS).astype(BF16)
        q = q_ref[:, cols]
        o = _softmax_pv([(_dot_nt(q, kp) + bias_ref[h, :, 0:past], vp),
                         (_dot_nt(q, kn_ref[:, cols]) + bias_ref[h, :, past:], vn_ref[:, cols])])
        o_ref[:, cols] = o.astype(BF16)


def _band_sample(bq, bk_new, bv_new, k_past, v_past, bias, l, nb, seq, past):
    width = BAND_HEADS * BAND_DIM
    new = pl.BlockSpec((seq, width), lambda b: (b, 0))
    cache = pl.BlockSpec((None, None, past * BAND_HEADS, 128), lambda b: (l, b, 0, 0))
    return pl.pallas_call(
        functools.partial(_band_sample_kernel, past=past),
        grid=(nb,),
        in_specs=[new, cache, cache, new, new, pl.BlockSpec(bias.shape, lambda b: (0, 0, 0))],
        out_specs=new,
        out_shape=jax.ShapeDtypeStruct((nb * seq, width), BF16),
        compiler_params=_cparams(("arbitrary",)),
        name="band_sample",
    )(bq, k_past, v_past, bk_new, bv_new, bias)


def _oproj_kernel(a_ref, b_ref, c_ref, x_ref, wo_ref, g_ref, beta_ref, o_ref):
    mix = (_dot(a_ref[...], wo_ref[0:_O_MIX_DIFF, :])
           + _dot(b_ref[...], wo_ref[_O_MIX_DIFF:_O_MIX_BAND, :])
           + _dot(c_ref[...], wo_ref[_O_MIX_BAND:, :]))
    o_ref[...] = _layer_norm(DEEPNORM_ALPHA * x_ref[...] + mix, g_ref[...], beta_ref[...])


def _oproj_ln(o_a, o_b, o_c, x2, wts, l, tm):
    m = x2.shape[0]
    row = lambda n: pl.BlockSpec((tm, n), lambda t: (t, 0))
    return pl.pallas_call(
        _oproj_kernel,
        grid=(m // tm,),
        in_specs=[row(o_a.shape[1]), row(o_b.shape[1]), row(o_c.shape[1]), row(D_MODEL),
                  _layer_block(wts['w_o'], l), _layer_block(wts['ln1_g'], l),
                  _layer_block(wts['ln1_b'], l)],
        out_specs=row(D_MODEL),
        out_shape=jax.ShapeDtypeStruct((m, D_MODEL), F32),
        compiler_params=_cparams(("arbitrary",)),
        name="oproj_ln",
    )(o_a, o_b, o_c, x2, wts['w_o'], wts['ln1_g'], wts['ln1_b'])


def _ffn_kernel(x_ref, prev_ref, wg_ref, wu_ref, wd_ref, cw_ref, cb_ref, g_ref, beta_ref,
                o_ref, st_ref, xb_ref, acc_ref, carry_ref, *, tf, seg, tiles_per_seq):
    i = pl.program_id(0)
    j = pl.program_id(1)
    nj = pl.num_programs(1)
    tm = x_ref.shape[0]
    col = pl.ds(pl.multiple_of(j * tf, tf), tf)

    @pl.when(j == 0)
    def _():
        xb_ref[...] = x_ref[...].astype(BF16)
        acc_ref[...] = jnp.zeros_like(acc_ref)

    xb = xb_ref[...]
    g = _dot(xb, wg_ref[...])
    u = _dot(xb, wu_ref[...])
    cw = cw_ref[...]
    cb = cb_ref[...]
    rows = lax.broadcasted_iota(jnp.int32, (seg, tf), 0)
    seq_start = (i % tiles_per_seq) == 0

    hs = []
    for s in range(tm // seg):
        gs = g[s * seg:(s + 1) * seg]
        state = prev_ref[s]
        if tiles_per_seq > 1:
            state = jnp.where(seq_start, state, carry_ref[:, col])
        pm2, pm1 = state[0:1], state[1:2]
        gm1 = jnp.where(rows == 0, pm1, pltpu.roll(gs, 1, 0))
        gm2 = jnp.where(rows == 0, pm2, jnp.where(rows == 1, pm1, pltpu.roll(gs, 2, 0)))
        gc = cb + cw[0:1] * gm2 + cw[1:2] * gm1 + cw[2:3] * gs
        hs.append(gc * (1.0 / (1.0 + jnp.exp(-gc))) * u[s * seg:(s + 1) * seg])
        last = gs[seg - 2:seg]
        st_ref[s, :, col] = last
        if tiles_per_seq > 1:
            carry_ref[:, col] = last
    h = hs[0] if len(hs) == 1 else jnp.concatenate(hs, axis=0)
    acc_ref[...] += _dot(h.astype(BF16), wd_ref[...])

    @pl.when(j == nj - 1)
    def _():
        o_ref[...] = _layer_norm(DEEPNORM_ALPHA * x_ref[...] + acc_ref[...],
                                 g_ref[...], beta_ref[...])


def _ffn(x2, prev, lp, wts, l, seq, tm, tf=512):
    m = x2.shape[0]
    nseq = m // seq
    seg = min(seq, tm)
    spt = tm // seg
    tps = seq // seg
    kern = functools.partial(_ffn_kernel, tf=tf, seg=seg, tiles_per_seq=tps)
    vec = lambda n: pl.BlockSpec((None, n, tf), lambda i, j: (l, 0, j))
    ln = lambda arr: pl.BlockSpec((None, 1, D_MODEL), lambda i, j: (l, 0, 0))
    return pl.pallas_call(
        kern,
        grid=(m // tm, D_FF // tf),
        in_specs=[pl.BlockSpec((tm, D_MODEL), lambda i, j: (i, 0)),
                  pl.BlockSpec((None, spt, CONV_W - 1, tf), lambda i, j: (lp, i // tps, 0, j)),
                  pl.BlockSpec((None, D_MODEL, tf), lambda i, j: (l, 0, j)),
                  pl.BlockSpec((None, D_MODEL, tf), lambda i, j: (l, 0, j)),
                  pl.BlockSpec((None, tf, D_MODEL), lambda i, j: (l, j, 0)),
                  vec(CONV_W), vec(1), ln(wts['ln2_g']), ln(wts['ln2_b'])],
        out_specs=[pl.BlockSpec((tm, D_MODEL), lambda i, j: (i, 0)),
                   pl.BlockSpec((spt, CONV_W - 1, D_FF), lambda i, j: (i // tps, 0, 0))],
        out_shape=[jax.ShapeDtypeStruct((m, D_MODEL), F32),
                   jax.ShapeDtypeStruct((nseq, CONV_W - 1, D_FF), F32)],
        scratch_shapes=[pltpu.VMEM((tm, D_MODEL), BF16),
                        pltpu.VMEM((tm, D_MODEL), F32),
                        pltpu.VMEM((CONV_W - 1, D_FF), F32)],
        compiler_params=_cparams(("arbitrary", "arbitrary")),
        name="ffn",
    )(x2, prev, wts['w_gate'], wts['w_up'], wts['w_down'], wts['conv_w'], wts['conv_b'],
      wts['ln2_g'], wts['ln2_b'])


def _prep_weights(w_in, mla_w_uq, mla_w_ukv):
    half = MLA_ROPE // 2
    swap = lambda w: jnp.concatenate([w[..., half:], w[..., :half]], axis=-1)
    pad = lambda w: jnp.concatenate([w, jnp.zeros_like(w)], axis=-1)
    offs = np.cumsum((512, 256, 64, 512, 512, 512, 768, 768, 768))[:-1].tolist()
    c_q, c_kv, k_r, d_q, d_k, d_v, b_q, b_k, b_v = jnp.split(w_in, offs, axis=-1)
    w_in_p = jnp.concatenate([c_q, c_kv, d_q, d_k, d_v, b_q, b_k, b_v, pad(k_r), pad(swap(k_r))],
                             axis=-1).astype(BF16)
    uq = mla_w_uq.reshape(DEPTH, MLA_Q_RANK, MLA_HEADS, MLA_NOPE + MLA_ROPE)
    nope, rope = uq[..., :MLA_NOPE], uq[..., MLA_NOPE:]
    flat = lambda w: w.reshape(DEPTH, w.shape[1], -1)
    w_uq_p = jnp.concatenate([flat(nope), flat(pad(rope)), flat(pad(swap(rope)))],
                             axis=-1).astype(BF16)
    ukv = mla_w_ukv.reshape(DEPTH, MLA_KV_RANK, MLA_HEADS, MLA_NOPE + MLA_V)
    w_ukv_p = jnp.concatenate([flat(ukv[..., :MLA_NOPE]), flat(ukv[..., MLA_NOPE:])],
                              axis=-1).astype(BF16)
    return w_in_p, w_uq_p, w_ukv_p


def _rope_tables(pos):
    half = MLA_ROPE // 2
    inv = ROPE_THETA ** (-jnp.arange(half, dtype=F32) / half)
    ang = pos.astype(F32)[:, None] * inv
    cos, sin = jnp.cos(ang), jnp.sin(ang)
    z = jnp.zeros((pos.shape[0], 128 - MLA_ROPE), F32)
    return jnp.concatenate([cos, cos, z], -1), jnp.concatenate([-sin, sin, z], -1)


def _t5_bucket(rel):
    half = T5_BUCKETS // 2
    exact = half // 2
    n = jnp.abs(rel)
    nf = jnp.maximum(n, 1).astype(F32)
    large = exact + (jnp.log(nf / exact) / math.log(T5_MAX_DIST / exact)
                     * (half - exact)).astype(jnp.int32)
    large = jnp.minimum(large, half - 1)
    return jnp.where(rel > 0, half, 0) + jnp.where(n < exact, n, large)


def _toeplitz(fn, rows, cols, shift):
    p = rows + cols
    k = jnp.arange(p, dtype=jnp.int32)
    u = fn(jnp.where(k < cols, k, k - p) + shift).astype(F32)
    flat = jnp.tile(u, (1, rows))[:, :rows * (p - 1)]
    return flat.reshape(u.shape[0], rows, p - 1)[:, :, :cols]


def _t5_bias(t5_table, rows, cols, shift):
    return _toeplitz(lambda rel: t5_table[_t5_bucket(rel)].T, rows, cols, shift)


def _band_bias(rel_table, t):
    def fn(d):
        rel = jnp.clip(BAND_ROWS - d, -BAND_REL_CLIP, BAND_REL_CLIP) + BAND_REL_CLIP
        return rel_table[:, rel]
    bias = _toeplitz(fn, t, BAND_ROWS + t, 0)
    qc = (jnp.arange(t, dtype=jnp.int32)[:, None] + BAND_ROWS) // CHUNK
    kc = jnp.arange(BAND_ROWS + t, dtype=jnp.int32)[None, :] // CHUNK
    ok = (kc <= qc) & (kc >= qc - BAND_PREV_CHUNKS)
    return jnp.where(ok[None], bias, NEG)


def _layer(x2, nb, seq, wts, l, tabs, caches, state_bufs, lam_init, tm_proj, tm_mid, tm_ffn):
    cos_t, sin_t = tabs['rope']
    state_bufs, (dq_b, dk_b, dv_b, bq_b, bk_b, bv_b, qf, kf, v_b) = _proj(
        x2, wts, l, cos_t, sin_t, tm_proj, seq, state_bufs)
    if caches is None:
        o_a = _mla_prompt(qf, kf, v_b, nb, seq)
        o_b = _diff_prompt(dq_b, dk_b, dv_b, tabs['t5'], wts, l, nb, seq, lam_init)
        o_c = _band_prompt(bq_b, bk_b, bv_b, tabs['band'], nb, seq)
        prev, lp = jnp.zeros((1, nb, CONV_W - 1, D_FF), F32), 0
    else:
        c_ckv, c_krope, c_dk, c_dv, c_bk, c_bv, prev = caches
        plen = c_ckv.shape[1] // nb
        kf_p, v_p = _kvup(c_ckv, c_krope, wts['w_ukv'], l)
        o_a = _mla_sample(qf, kf_p, v_p, kf, v_b, nb, seq, plen)
        o_b = _diff_sample(dq_b, dk_b, dv_b, c_dk, c_dv, tabs['t5'], wts, l, nb, seq, plen,
                           lam_init)
        o_c = _band_sample(bq_b, bk_b, bv_b, c_bk, c_bv, tabs['band'], l, nb, seq,
                           c_bk.shape[2] // BAND_HEADS)
        lp = l
    x1 = _oproj_ln(o_a, o_b, o_c, x2, wts, l, tm_mid)
    x_out, conv_state = _ffn(x1, prev, lp, wts, l, seq, tm_ffn)
    return x_out, state_bufs, conv_state


def _final_states(bufs, conv, nb, seq):
    ckv, krope, dk, dv, bk, bv = bufs
    keep = min(BAND_ROWS, seq)
    return (ckv.reshape(DEPTH, nb, seq, MLA_KV_RANK), krope.reshape(DEPTH, nb, seq, MLA_ROPE),
            dk.reshape(DEPTH, nb, seq, DIFF_HEADS, 2 * DIFF_QK),
            dv.reshape(DEPTH, nb, seq, DIFF_HEADS, DIFF_V),
            bk.reshape(DEPTH, nb, keep, BAND_HEADS, BAND_DIM),
            bv.reshape(DEPTH, nb, keep, BAND_HEADS, BAND_DIM), jnp.stack(conv))


def kernel(x_prompt, x_sample, cache_mla_ckv, cache_mla_krope, cache_diff_k, cache_diff_v, cache_band_k, cache_band_v, state_ffn_conv, t5_table, w_in, mla_q_norm, mla_w_uq, mla_kv_norm, mla_w_ukv, diff_lq1, diff_lk1, diff_lq2, diff_lk2, diff_subln, band_rel_table, w_o, ln1_g, ln1_b, ffn_w_gate, ffn_w_up, ffn_conv_w, ffn_conv_b, ffn_w_down, ln2_g, ln2_b):
    nb_p, seq_p, _ = x_prompt.shape
    nb_s, seq_s, _ = x_sample.shape
    past_len = cache_mla_ckv.shape[2]
    band_len = cache_band_k.shape[2]
    assert seq_p % ATT_T == 0 and seq_s == CHUNK and past_len % CHUNK == 0
    assert band_len == BAND_ROWS

    w_in_p, w_uq_p, w_ukv_p = _prep_weights(w_in, mla_w_uq, mla_w_ukv)
    vec = lambda a: a.reshape(DEPTH, 1, -1)
    wts = {'w_in': w_in_p, 'w_uq': w_uq_p, 'w_ukv': w_ukv_p,
           'q_norm': vec(mla_q_norm), 'kv_norm': vec(mla_kv_norm),
           'lam': jnp.stack([diff_lq1, diff_lk1, diff_lq2, diff_lk2], axis=1),
           'subln': vec(diff_subln), 'w_o': w_o.astype(BF16),
           'ln1_g': vec(ln1_g), 'ln1_b': vec(ln1_b),
           'w_gate': ffn_w_gate.astype(BF16), 'w_up': ffn_w_up.astype(BF16),
           'w_down': ffn_w_down.astype(BF16), 'conv_w': ffn_conv_w, 'conv_b': vec(ffn_conv_b),
           'ln2_g': vec(ln2_g), 'ln2_b': vec(ln2_b)}
    caches = (cache_mla_ckv.reshape(DEPTH, nb_s * past_len, MLA_KV_RANK),
              cache_mla_krope.reshape(DEPTH, nb_s * past_len, MLA_ROPE),
              cache_diff_k.reshape(DEPTH, nb_s, past_len * DIFF_HEADS, 128),
              cache_diff_v.reshape(DEPTH, nb_s, past_len * DIFF_HEADS, 128),
              cache_band_k.reshape(DEPTH, nb_s, band_len * BAND_HEADS, 128),
              cache_band_v.reshape(DEPTH, nb_s, band_len * BAND_HEADS, 128),
              state_ffn_conv)

    t = ATT_T
    pos_p = jnp.arange(seq_p, dtype=jnp.int32)
    pos_s = past_len + jnp.arange(seq_s, dtype=jnp.int32)
    tm_s = nb_s * seq_s
    rope_s = tuple(jnp.tile(tb, (nb_s, 1)) for tb in _rope_tables(pos_s))
    tabs_p = {'rope': _rope_tables(pos_p), 't5': _t5_bias(t5_table, t, seq_p, t - seq_p)}
    tabs_s = {'rope': rope_s, 't5': _t5_bias(t5_table, seq_s, past_len + seq_s, -past_len)}

    y_p = x_prompt.reshape(nb_p * seq_p, D_MODEL)
    y_s = x_sample.reshape(tm_s, D_MODEL)
    bufs_p = bufs_s = None
    conv_p, conv_s = [], []
    for l in range(DEPTH):
        lam_init = 0.8 - 0.6 * math.exp(-0.3 * l)
        band_p = _band_bias(band_rel_table[l], t)
        band_s = _band_bias(band_rel_table[l], seq_s)
        y_p, bufs_p, cs = _layer(y_p, nb_p, seq_p, wts, l, dict(tabs_p, band=band_p), None,
                                 bufs_p, lam_init, tm_proj=256, tm_mid=512, tm_ffn=512)
        conv_p.append(cs)
        y_s, bufs_s, cs = _layer(y_s, nb_s, seq_s, wts, l, dict(tabs_s, band=band_s), caches,
                                 bufs_s, lam_init, tm_proj=256, tm_mid=512, tm_ffn=tm_s)
        conv_s.append(cs)
    return (y_p.reshape(nb_p, seq_p, D_MODEL), y_s.reshape(nb_s, seq_s, D_MODEL),
            *_final_states(bufs_p, conv_p, nb_p, seq_p),
            *_final_states(bufs_s, conv_s, nb_s, seq_s))
```

```python
import functools
import math

import jax
import jax.numpy as jnp
import numpy as np
from jax import lax
from jax.experimental import pallas as pl
from jax.experimental.pallas import tpu as pltpu

F32 = jnp.float32
BF16 = jnp.bfloat16

D_MODEL = 2048
DEPTH = 2
CHUNK = 64
MLA_HEADS = 6
MLA_Q_RANK = 512
MLA_KV_RANK = 256
MLA_NOPE = 128
MLA_ROPE = 64
MLA_V = 128
ROPE_THETA = 10000.0
DIFF_HEADS = 4
DIFF_QK = 64
DIFF_V = 128
BAND_HEADS = 6
BAND_DIM = 128
BAND_PREV_CHUNKS = 8
BAND_REL_CLIP = 256
T5_BUCKETS = 32
T5_MAX_DIST = 128
D_FF = 5632
CONV_W = 3
LN_EPS = 1e-5
RMS_EPS = 1e-6
DEEPNORM_ALPHA = (2 * DEPTH) ** 0.25

NEG = -1e30
MLA_QK = 256
MLA_SCALE = (MLA_NOPE + MLA_ROPE) ** -0.5
DIFF_SCALE = DIFF_QK ** -0.5
BAND_SCALE = BAND_DIM ** -0.5
BAND_ROWS = BAND_PREV_CHUNKS * CHUNK
ATT_T = 256
VMEM_LIMIT = 56 * 1024 * 1024

_O_CQ, _O_CKV, _O_DQ, _O_DK, _O_DV = 0, 512, 768, 1280, 1792
_O_BQ, _O_BK, _O_BV, _O_KR, _O_KRS, _IN_P = 2304, 3072, 3840, 4608, 4736, 4864
_O_MIX_DIFF = MLA_HEADS * MLA_V
_O_MIX_BAND = _O_MIX_DIFF + DIFF_HEADS * DIFF_V
_N_STATE = 6


def _cparams(sem):
    return pltpu.CompilerParams(dimension_semantics=sem, vmem_limit_bytes=VMEM_LIMIT)


def _dot(a, b):
    return jnp.dot(a, b, preferred_element_type=F32)


def _dot_nt(a, b):
    return lax.dot_general(a, b, (((1,), (1,)), ((), ())), preferred_element_type=F32)


def _rms(x, g):
    return x * lax.rsqrt(jnp.mean(x * x, axis=-1, keepdims=True) + RMS_EPS) * g


def _layer_norm(x, g, b):
    mu = jnp.mean(x, axis=-1, keepdims=True)
    xc = x - mu
    var = jnp.mean(xc * xc, axis=-1, keepdims=True)
    return xc * lax.rsqrt(var + LN_EPS) * g + b


def _layer_block(arr, l):
    zeros = (0,) * (arr.ndim - 1)
    return pl.BlockSpec((None,) + arr.shape[1:], lambda *_: (l,) + zeros,
                        pipeline_mode=pl.Buffered(1))


def _heads_to_rows(o_ref, val, heads):
    rows = val.shape[0]
    for h in range(heads):
        o_ref[pl.ds(h, rows, stride=heads), :] = val[:, h * 128:(h + 1) * 128]


def _head_rows(ref, h, rows, heads):
    return ref[pl.ds(h, rows, stride=heads), :]


def _proj_kernel(*refs, n_alias, tiles_per_seq, band_from):
    (x_ref, win_ref, wuq_ref, wukv_ref, qn_ref, kvn_ref, cos_ref, sin_ref) = refs[:8]
    (ckv_o, krope_o, dk_o, dv_o, bk_o, bv_o,
     dq_b, dk_b, dv_b, bq_b, bk_b, bv_b, qf_b, kf_b, v_b) = refs[8 + n_alias:]
    xb = x_ref[...].astype(BF16)
    cos = cos_ref[...]
    sin = sin_ref[...]

    def seg(off, n):
        return _dot(xb, win_ref[:, off:off + n])

    cq = _rms(seg(_O_CQ, MLA_Q_RANK), qn_ref[...]).astype(BF16)
    nh = MLA_HEADS * 128
    q = _dot(cq, wuq_ref[...])
    for h in range(MLA_HEADS):
        lo = h * 128
        qr = q[:, nh + lo:nh + lo + 128] * cos + q[:, 2 * nh + lo:2 * nh + lo + 128] * sin
        qf_b[h, :, 0:128] = (q[:, lo:lo + 128] * MLA_SCALE).astype(BF16)
        qf_b[h, :, 128:256] = (qr * MLA_SCALE).astype(BF16)

    ckv = _rms(seg(_O_CKV, MLA_KV_RANK), kvn_ref[...])
    ckv_o[...] = ckv
    ckvb = ckv.astype(BF16)
    krs = seg(_O_KR, 256)
    kr = krs[:, 0:128] * cos + krs[:, 128:256] * sin
    krope_o[...] = kr[:, 0:MLA_ROPE]
    krb = kr.astype(BF16)
    kv = _dot(ckvb, wukv_ref[...])
    for h in range(MLA_HEADS):
        lo = h * 128
        kf_b[h, :, 0:128] = kv[:, lo:lo + 128].astype(BF16)
        kf_b[h, :, 128:256] = krb
    v_b[...] = kv[:, nh:2 * nh].astype(BF16)

    dq_b[...] = (seg(_O_DQ, 512) * DIFF_SCALE).astype(BF16)
    dk = seg(_O_DK, 512)
    _heads_to_rows(dk_o, dk, DIFF_HEADS)
    dk_b[...] = dk.astype(BF16)
    dv = seg(_O_DV, 512)
    _heads_to_rows(dv_o, dv, DIFF_HEADS)
    dv_b[...] = dv.astype(BF16)
    bq_b[...] = (seg(_O_BQ, 768) * BAND_SCALE).astype(BF16)
    bk = seg(_O_BK, 768)
    bk_b[...] = bk.astype(BF16)
    bv = seg(_O_BV, 768)
    bv_b[...] = bv.astype(BF16)

    def band_state():
        _heads_to_rows(bk_o, bk, BAND_HEADS)
        _heads_to_rows(bv_o, bv, BAND_HEADS)

    if band_from == 0:
        band_state()
    else:
        pl.when(pl.program_id(0) % tiles_per_seq >= band_from)(band_state)


def _proj(x2, wts, l, cos_t, sin_t, tm, seq, state_bufs):
    m = x2.shape[0]
    period = cos_t.shape[0] // tm
    keep = min(BAND_ROWS, seq)
    if seq > tm:
        tps, kt = seq // tm, keep // tm
        band_from = tps - kt
        band_idx = lambda t: (l, (t // tps) * kt + jnp.maximum(t % tps - band_from, 0), 0)
    else:
        assert keep == seq
        tps, band_from = 1, 0
        band_idx = lambda t: (l, t, 0)
    row = lambda n: pl.BlockSpec((tm, n), lambda t: (t, 0))
    srow = lambda n, k=1: pl.BlockSpec((None, tm * k, n), lambda t: (l, t, 0))
    band = pl.BlockSpec((None, tm * BAND_HEADS, 128), band_idx)
    tab = pl.BlockSpec((tm, 128), lambda t: (t % period, 0))
    head = pl.BlockSpec((MLA_HEADS, tm, MLA_QK), lambda t: (0, t, 0))
    st = lambda rows, n: jax.ShapeDtypeStruct((DEPTH, rows, n), F32)
    b16 = lambda n: jax.ShapeDtypeStruct((m, n), BF16)
    hb16 = jax.ShapeDtypeStruct((MLA_HEADS, m, MLA_QK), BF16)
    n_band = (m // seq) * keep * BAND_HEADS
    n_alias = 0 if state_bufs is None else _N_STATE
    kern = functools.partial(_proj_kernel, n_alias=n_alias, tiles_per_seq=tps,
                             band_from=band_from)
    ins = [x2, wts['w_in'], wts['w_uq'], wts['w_ukv'], wts['q_norm'], wts['kv_norm'],
           cos_t, sin_t]
    in_specs = [row(D_MODEL), _layer_block(wts['w_in'], l), _layer_block(wts['w_uq'], l),
                _layer_block(wts['w_ukv'], l), _layer_block(wts['q_norm'], l),
                _layer_block(wts['kv_norm'], l), tab, tab]
    if n_alias:
        ins += list(state_bufs)
        in_specs += [pl.BlockSpec(memory_space=pl.ANY)] * n_alias
    outs = pl.pallas_call(
        kern,
        grid=(m // tm,),
        in_specs=in_specs,
        out_specs=[srow(256), srow(64), srow(128, DIFF_HEADS), srow(128, DIFF_HEADS), band, band,
                   row(512), row(512), row(512), row(768), row(768), row(768),
                   head, head, row(768)],
        out_shape=[st(m, 256), st(m, 64), st(m * DIFF_HEADS, 128), st(m * DIFF_HEADS, 128),
                   st(n_band, 128), st(n_band, 128),
                   b16(512), b16(512), b16(512), b16(768), b16(768), b16(768),
                   hb16, hb16, b16(768)],
        input_output_aliases={8 + k: k for k in range(n_alias)},
        compiler_params=_cparams(("arbitrary",)),
        name="proj",
    )(*ins)
    return outs[:_N_STATE], outs[_N_STATE:]


def _chunk_mask(t):
    r = lax.broadcasted_iota(jnp.int32, (t, t), 0) // CHUNK
    c = lax.broadcasted_iota(jnp.int32, (t, t), 1) // CHUNK
    return r >= c


def _softmax_pv(blocks):
    m = None
    for s, _ in blocks:
        bm = jnp.max(s, axis=-1, keepdims=True)
        m = bm if m is None else jnp.maximum(m, bm)
    l = None
    acc = None
    for s, vb in blocks:
        p = jnp.exp(s - m)
        bl = jnp.sum(p, axis=-1, keepdims=True)
        ba = _dot(p.astype(BF16), vb)
        l = bl if l is None else l + bl
        acc = ba if acc is None else acc + ba
    return acc / l


def _causal_attend(qs, k, v_ref, n, t, bias=None):
    mask = _chunk_mask(t)
    outs = []
    for q in qs:
        s = _dot_nt(q, k)
        if bias is not None:
            s = s + bias
        blocks = [(jnp.where(mask, s[:, n - t:], NEG), v_ref[n - t:n, :])]
        if n > t:
            blocks.append((s[:, :n - t], v_ref[0:n - t, :]))
        outs.append(_softmax_pv(blocks))
    return outs


def _mla_prompt_kernel(q_ref, k_ref, v_ref, o_ref, *, nq):
    t = ATT_T
    for tile in range(nq):
        n = (tile + 1) * t
        (o,) = _causal_attend([q_ref[0, n - t:n, :]], k_ref[0, 0:n, :], v_ref, n, t)
        o_ref[n - t:n, :] = o.astype(BF16)


def _mla_prompt(qf, kf, v, nb, seq):
    return pl.pallas_call(
        functools.partial(_mla_prompt_kernel, nq=seq // ATT_T),
        grid=(nb, MLA_HEADS),
        in_specs=[pl.BlockSpec((1, seq, MLA_QK), lambda b, h: (h, b, 0)),
                  pl.BlockSpec((1, seq, MLA_QK), lambda b, h: (h, b, 0)),
                  pl.BlockSpec((seq, MLA_V), lambda b, h: (b, h))],
        out_specs=pl.BlockSpec((seq, MLA_V), lambda b, h: (b, h)),
        out_shape=jax.ShapeDtypeStruct((nb * seq, MLA_HEADS * MLA_V), BF16),
        compiler_params=_cparams(("arbitrary",) * 2),
        name="mla_prompt",
    )(qf, kf, v)


def _mla_sample_kernel(q_ref, cp_ref, rp_ref, cn_ref, rn_ref, wukv_ref, o_ref, q_scr, k_scr,
                       *, seq, past):
    nh = MLA_HEADS * 128
    lat = MLA_KV_RANK
    for h in range(MLA_HEADS):
        rows = slice(h * seq, (h + 1) * seq)
        w_k = wukv_ref[:, h * 128:(h + 1) * 128]
        q_scr[rows, 0:lat] = _dot_nt(q_ref[h, :, 0:MLA_NOPE], w_k).astype(BF16)
        q_scr[rows, lat:lat + 128] = q_ref[h, :, MLA_NOPE:MLA_QK]
    zeros = jnp.zeros((past, 128 - MLA_ROPE), BF16)
    for rows, c_ref, r_ref in ((slice(0, past), cp_ref, rp_ref),
                               (slice(past, past + seq), cn_ref, rn_ref)):
        n = rows.stop - rows.start
        k_scr[rows, 0:lat] = c_ref[...].astype(BF16)
        k_scr[rows, lat:lat + MLA_ROPE] = r_ref[...].astype(BF16)
        k_scr[rows, lat + MLA_ROPE:lat + 128] = zeros[0:n]
    q = q_scr[...]
    kp = k_scr[0:past, :]
    kn = k_scr[past:past + seq, :]
    o_lat = _softmax_pv([(_dot_nt(q, kp), kp[:, 0:lat]),
                         (_dot_nt(q, kn), kn[:, 0:lat])]).astype(BF16)
    for h in range(MLA_HEADS):
        w_v = wukv_ref[:, nh + h * 128:nh + (h + 1) * 128]
        o_ref[:, h * 128:(h + 1) * 128] = _dot(o_lat[h * seq:(h + 1) * seq], w_v).astype(BF16)


def _mla_sample(qf, c_past, r_past, c_new, r_new, w_ukv, l, nb, seq, past):
    lay = lambda rows, n: pl.BlockSpec((None, rows, n), lambda b: (l, b, 0))
    return pl.pallas_call(
        functools.partial(_mla_sample_kernel, seq=seq, past=past),
        grid=(nb,),
        in_specs=[pl.BlockSpec((MLA_HEADS, seq, MLA_QK), lambda b: (0, b, 0)),
                  lay(past, MLA_KV_RANK), lay(past, MLA_ROPE),
                  lay(seq, MLA_KV_RANK), lay(seq, MLA_ROPE), _layer_block(w_ukv, l)],
        out_specs=pl.BlockSpec((seq, MLA_HEADS * MLA_V), lambda b: (b, 0)),
        out_shape=jax.ShapeDtypeStruct((nb * seq, MLA_HEADS * MLA_V), BF16),
        scratch_shapes=[pltpu.VMEM((MLA_HEADS * seq, MLA_KV_RANK + 128), BF16),
                        pltpu.VMEM((past + seq, MLA_KV_RANK + 128), BF16)],
        compiler_params=_cparams(("arbitrary",)),
        name="mla_sample",
    )(qf, c_past, r_past, c_new, r_new, w_ukv)


def _diff_lambda(lam_ref, lam_init):
    v = lam_ref[...]
    a = jnp.sum(v[0:1] * v[1:2], axis=-1, keepdims=True)
    b = jnp.sum(v[2:3] * v[3:4], axis=-1, keepdims=True)
    return jnp.exp(a) - jnp.exp(b) + lam_init


def _split_q(q):
    lane = lax.broadcasted_iota(jnp.int32, q.shape, 1)
    qf = q.astype(F32)
    return (jnp.where(lane < DIFF_QK, qf, 0.0).astype(BF16),
            jnp.where(lane >= DIFF_QK, qf, 0.0).astype(BF16))


def _diff_finish(o0, o1, lam, g, lam_init):
    o = o0 - lam * o1
    return (_rms(o, g) * (1.0 - lam_init)).astype(BF16)


def _diff_prompt_kernel(q_ref, k_ref, v_ref, bias_ref, lam_ref, g_ref, o_ref, *, lam_init, nq):
    t = ATT_T
    seq = nq * t
    lam = _diff_lambda(lam_ref, lam_init)
    for tile in range(nq):
        n = (tile + 1) * t
        o0, o1 = _causal_attend(_split_q(q_ref[n - t:n, :]), k_ref[0:n, :], v_ref, n, t,
                                bias=bias_ref[0, :, seq - n:seq])
        o_ref[n - t:n, :] = _diff_finish(o0, o1, lam, g_ref[...], lam_init)


def _diff_prompt(dq, dk, dv, bias, wts, l, nb, seq, lam_init):
    t = ATT_T
    return pl.pallas_call(
        functools.partial(_diff_prompt_kernel, lam_init=lam_init, nq=seq // t),
        grid=(nb, DIFF_HEADS),
        in_specs=[pl.BlockSpec((seq, 128), lambda b, h: (b, h)),
                  pl.BlockSpec((seq, 128), lambda b, h: (b, h)),
                  pl.BlockSpec((seq, 128), lambda b, h: (b, h)),
                  pl.BlockSpec((1, t, seq), lambda b, h: (h, 0, 0)),
                  _layer_block(wts['lam'], l), _layer_block(wts['subln'], l)],
        out_specs=pl.BlockSpec((seq, DIFF_V), lambda b, h: (b, h)),
        out_shape=jax.ShapeDtypeStruct((nb * seq, DIFF_HEADS * DIFF_V), BF16),
        compiler_params=_cparams(("arbitrary",) * 2),
        name="diff_prompt",
    )(dq, dk, dv, bias, wts['lam'], wts['subln'])


def _diff_sample_kernel(q_ref, kp_ref, vp_ref, kn_ref, vn_ref, bias_ref, lam_ref, g_ref,
                        o_ref, *, lam_init, past):
    lam = _diff_lambda(lam_ref, lam_init)
    for h in range(DIFF_HEADS):
        cols = slice(h * 128, (h + 1) * 128)
        kp = _head_rows(kp_ref, h, past, DIFF_HEADS).astype(BF16)
        vp = _head_rows(vp_ref, h, past, DIFF_HEADS).astype(BF16)
        kn = kn_ref[:, cols]
        vn = vn_ref[:, cols]
        bias_p = bias_ref[h, :, 0:past]
        bias_n = bias_ref[h, :, past:]
        outs = [_softmax_pv([(_dot_nt(q, kp) + bias_p, vp), (_dot_nt(q, kn) + bias_n, vn)])
                for q in _split_q(q_ref[:, cols])]
        o_ref[:, cols] = _diff_finish(outs[0], outs[1], lam, g_ref[...], lam_init)


def _diff_sample(dq, dk_new, dv_new, k_past, v_past, bias, wts, l, nb, seq, past, lam_init):
    width = DIFF_HEADS * 128
    new = pl.BlockSpec((seq, width), lambda b: (b, 0))
    cache = pl.BlockSpec((None, None, past * DIFF_HEADS, 128), lambda b: (l, b, 0, 0))
    return pl.pallas_call(
        functools.partial(_diff_sample_kernel, lam_init=lam_init, past=past),
        grid=(nb,),
        in_specs=[new, cache, cache, new, new,
                  pl.BlockSpec(bias.shape, lambda b: (0, 0, 0)),
                  _layer_block(wts['lam'], l), _layer_block(wts['subln'], l)],
        out_specs=new,
        out_shape=jax.ShapeDtypeStruct((nb * seq, width), BF16),
        compiler_params=_cparams(("arbitrary",)),
        name="diff_sample",
    )(dq, k_past, v_past, dk_new, dv_new, bias, wts['lam'], wts['subln'])


def _band_prompt_kernel(q_ref, k_ref, v_ref, bias_ref, o_ref, *, nq):
    t = ATT_T
    full = BAND_ROWS + t
    for tile in range(nq):
        start = max(0, tile * t - BAND_ROWS)
        width = (tile + 1) * t - start
        s = (_dot_nt(q_ref[tile * t:(tile + 1) * t, :], k_ref[start:start + width, :])
             + bias_ref[0, :, full - width:full])
        o = _softmax_pv([(s, v_ref[start:start + width, :])])
        o_ref[tile * t:(tile + 1) * t, :] = o.astype(BF16)


def _band_prompt(bq, bk, bv, bias, nb, seq):
    t = ATT_T
    blk = pl.BlockSpec((seq, BAND_DIM), lambda b, h: (b, h))
    return pl.pallas_call(
        functools.partial(_band_prompt_kernel, nq=seq // t),
        grid=(nb, BAND_HEADS),
        in_specs=[blk, blk, blk,
                  pl.BlockSpec((1, t, BAND_ROWS + t), lambda b, h: (h, 0, 0))],
        out_specs=blk,
        out_shape=jax.ShapeDtypeStruct((nb * seq, BAND_HEADS * BAND_DIM), BF16),
        compiler_params=_cparams(("arbitrary",) * 2),
        name="band_prompt",
    )(bq, bk, bv, bias)


def _band_sample_kernel(q_ref, kp_ref, vp_ref, kn_ref, vn_ref, bias_ref, o_ref, *, past):
    for h in range(BAND_HEADS):
        cols = slice(h * 128, (h + 1) * 128)
        kp = _head_rows(kp_ref, h, past, BAND_HEADS).astype(BF16)
        vp = _head_rows(vp_ref, h, past, BAND_HEADS).astype(BF16)
        q = q_ref[:, cols]
        o = _softmax_pv([(_dot_nt(q, kp) + bias_ref[h, :, 0:past], vp),
                         (_dot_nt(q, kn_ref[:, cols]) + bias_ref[h, :, past:], vn_ref[:, cols])])
        o_ref[:, cols] = o.astype(BF16)


def _band_sample(bq, bk_new, bv_new, k_past, v_past, bias, l, nb, seq, past):
    width = BAND_HEADS * BAND_DIM
    new = pl.BlockSpec((seq, width), lambda b: (b, 0))
    cache = pl.BlockSpec((None, None, past * BAND_HEADS, 128), lambda b: (l, b, 0, 0))
    return pl.pallas_call(
        functools.partial(_band_sample_kernel, past=past),
        grid=(nb,),
        in_specs=[new, cache, cache, new, new, pl.BlockSpec(bias.shape, lambda b: (0, 0, 0))],
        out_specs=new,
        out_shape=jax.ShapeDtypeStruct((nb * seq, width), BF16),
        compiler_params=_cparams(("arbitrary",)),
        name="band_sample",
    )(bq, k_past, v_past, bk_new, bv_new, bias)


def _oproj_kernel(a_ref, b_ref, c_ref, x_ref, wo_ref, g_ref, beta_ref, o_ref):
    mix = (_dot(a_ref[...], wo_ref[0:_O_MIX_DIFF, :])
           + _dot(b_ref[...], wo_ref[_O_MIX_DIFF:_O_MIX_BAND, :])
           + _dot(c_ref[...], wo_ref[_O_MIX_BAND:, :]))
    o_ref[...] = _layer_norm(DEEPNORM_ALPHA * x_ref[...] + mix, g_ref[...], beta_ref[...])


def _oproj_ln(o_a, o_b, o_c, x2, wts, l, tm):
    m = x2.shape[0]
    row = lambda n: pl.BlockSpec((tm, n), lambda t: (t, 0))
    return pl.pallas_call(
        _oproj_kernel,
        grid=(m // tm,),
        in_specs=[row(o_a.shape[1]), row(o_b.shape[1]), row(o_c.shape[1]), row(D_MODEL),
                  _layer_block(wts['w_o'], l), _layer_block(wts['ln1_g'], l),
                  _layer_block(wts['ln1_b'], l)],
        out_specs=row(D_MODEL),
        out_shape=jax.ShapeDtypeStruct((m, D_MODEL), F32),
        compiler_params=_cparams(("arbitrary",)),
        name="oproj_ln",
    )(o_a, o_b, o_c, x2, wts['w_o'], wts['ln1_g'], wts['ln1_b'])


def _ffn_kernel(x_ref, prev_ref, wg_ref, wu_ref, wd_ref, cw_ref, cb_ref, g_ref, beta_ref,
                o_ref, st_ref, xb_ref, acc_ref, carry_ref, *, tf, seg, tiles_per_seq):
    i = pl.program_id(0)
    j = pl.program_id(1)
    nj = pl.num_programs(1)
    tm = x_ref.shape[0]
    col = pl.ds(pl.multiple_of(j * tf, tf), tf)

    @pl.when(j == 0)
    def _():
        xb_ref[...] = x_ref[...].astype(BF16)
        acc_ref[...] = jnp.zeros_like(acc_ref)

    xb = xb_ref[...]
    g = _dot(xb, wg_ref[...])
    u = _dot(xb, wu_ref[...])
    cw = cw_ref[...]
    cb = cb_ref[...]
    rows = lax.broadcasted_iota(jnp.int32, (seg, tf), 0)
    seq_start = (i % tiles_per_seq) == 0

    hs = []
    for s in range(tm // seg):
        gs = g[s * seg:(s + 1) * seg]
        state = prev_ref[s]
        if tiles_per_seq > 1:
            state = jnp.where(seq_start, state, carry_ref[:, col])
        pm2, pm1 = state[0:1], state[1:2]
        gm1 = jnp.where(rows == 0, pm1, pltpu.roll(gs, 1, 0))
        gm2 = jnp.where(rows == 0, pm2, jnp.where(rows == 1, pm1, pltpu.roll(gs, 2, 0)))
        gc = cb + cw[0:1] * gm2 + cw[1:2] * gm1 + cw[2:3] * gs
        hs.append(gc * (1.0 / (1.0 + jnp.exp(-gc))) * u[s * seg:(s + 1) * seg])
        last = gs[seg - 2:seg]
        st_ref[s, :, col] = last
        if tiles_per_seq > 1:
            carry_ref[:, col] = last
    h = hs[0] if len(hs) == 1 else jnp.concatenate(hs, axis=0)
    acc_ref[...] += _dot(h.astype(BF16), wd_ref[...])

    @pl.when(j == nj - 1)
    def _():
        o_ref[...] = _layer_norm(DEEPNORM_ALPHA * x_ref[...] + acc_ref[...],
                                 g_ref[...], beta_ref[...])


def _ffn(x2, prev, lp, wts, l, seq, tm, tf=512):
    m = x2.shape[0]
    nseq = m // seq
    seg = min(seq, tm)
    spt = tm // seg
    tps = seq // seg
    kern = functools.partial(_ffn_kernel, tf=tf, seg=seg, tiles_per_seq=tps)
    vec = lambda n: pl.BlockSpec((None, n, tf), lambda i, j: (l, 0, j))
    ln = lambda arr: pl.BlockSpec((None, 1, D_MODEL), lambda i, j: (l, 0, 0))
    return pl.pallas_call(
        kern,
        grid=(m // tm, D_FF // tf),
        in_specs=[pl.BlockSpec((tm, D_MODEL), lambda i, j: (i, 0)),
                  pl.BlockSpec((None, spt, CONV_W - 1, tf), lambda i, j: (lp, i // tps, 0, j)),
                  pl.BlockSpec((None, D_MODEL, tf), lambda i, j: (l, 0, j)),
                  pl.BlockSpec((None, D_MODEL, tf), lambda i, j: (l, 0, j)),
                  pl.BlockSpec((None, tf, D_MODEL), lambda i, j: (l, j, 0)),
                  vec(CONV_W), vec(1), ln(wts['ln2_g']), ln(wts['ln2_b'])],
        out_specs=[pl.BlockSpec((tm, D_MODEL), lambda i, j: (i, 0)),
                   pl.BlockSpec((spt, CONV_W - 1, D_FF), lambda i, j: (i // tps, 0, 0))],
        out_shape=[jax.ShapeDtypeStruct((m, D_MODEL), F32),
                   jax.ShapeDtypeStruct((nseq, CONV_W - 1, D_FF), F32)],
        scratch_shapes=[pltpu.VMEM((tm, D_MODEL), BF16),
                        pltpu.VMEM((tm, D_MODEL), F32),
                        pltpu.VMEM((CONV_W - 1, D_FF), F32)],
        compiler_params=_cparams(("arbitrary", "arbitrary")),
        name="ffn",
    )(x2, prev, wts['w_gate'], wts['w_up'], wts['w_down'], wts['conv_w'], wts['conv_b'],
      wts['ln2_g'], wts['ln2_b'])


def _prep_weights(w_in, mla_w_uq, mla_w_ukv):
    half = MLA_ROPE // 2
    swap = lambda w: jnp.concatenate([w[..., half:], w[..., :half]], axis=-1)
    pad = lambda w: jnp.concatenate([w, jnp.zeros_like(w)], axis=-1)
    offs = np.cumsum((512, 256, 64, 512, 512, 512, 768, 768, 768))[:-1].tolist()
    c_q, c_kv, k_r, d_q, d_k, d_v, b_q, b_k, b_v = jnp.split(w_in, offs, axis=-1)
    w_in_p = jnp.concatenate([c_q, c_kv, d_q, d_k, d_v, b_q, b_k, b_v, pad(k_r), pad(swap(k_r))],
                             axis=-1).astype(BF16)
    uq = mla_w_uq.reshape(DEPTH, MLA_Q_RANK, MLA_HEADS, MLA_NOPE + MLA_ROPE)
    nope, rope = uq[..., :MLA_NOPE], uq[..., MLA_NOPE:]
    flat = lambda w: w.reshape(DEPTH, w.shape[1], -1)
    w_uq_p = jnp.concatenate([flat(nope), flat(pad(rope)), flat(pad(swap(rope)))],
                             axis=-1).astype(BF16)
    ukv = mla_w_ukv.reshape(DEPTH, MLA_KV_RANK, MLA_HEADS, MLA_NOPE + MLA_V)
    w_ukv_p = jnp.concatenate([flat(ukv[..., :MLA_NOPE]), flat(ukv[..., MLA_NOPE:])],
                              axis=-1).astype(BF16)
    return w_in_p, w_uq_p, w_ukv_p


def _rope_tables(pos):
    half = MLA_ROPE // 2
    inv = ROPE_THETA ** (-jnp.arange(half, dtype=F32) / half)
    ang = pos.astype(F32)[:, None] * inv
    cos, sin = jnp.cos(ang), jnp.sin(ang)
    z = jnp.zeros((pos.shape[0], 128 - MLA_ROPE), F32)
    return jnp.concatenate([cos, cos, z], -1), jnp.concatenate([-sin, sin, z], -1)


def _t5_bucket(rel):
    half = T5_BUCKETS // 2
    exact = half // 2
    n = jnp.abs(rel)
    nf = jnp.maximum(n, 1).astype(F32)
    large = exact + (jnp.log(nf / exact) / math.log(T5_MAX_DIST / exact)
                     * (half - exact)).astype(jnp.int32)
    large = jnp.minimum(large, half - 1)
    return jnp.where(rel > 0, half, 0) + jnp.where(n < exact, n, large)


def _toeplitz(fn, rows, cols, shift):
    p = rows + cols
    k = jnp.arange(p, dtype=jnp.int32)
    u = fn(jnp.where(k < cols, k, k - p) + shift).astype(F32)
    flat = jnp.tile(u, (1, rows))[:, :rows * (p - 1)]
    return flat.reshape(u.shape[0], rows, p - 1)[:, :, :cols]


def _t5_bias(t5_table, rows, cols, shift):
    return _toeplitz(lambda rel: t5_table[_t5_bucket(rel)].T, rows, cols, shift)


def _band_bias(rel_table, t):
    def fn(d):
        rel = jnp.clip(BAND_ROWS - d, -BAND_REL_CLIP, BAND_REL_CLIP) + BAND_REL_CLIP
        return rel_table[:, rel]
    bias = _toeplitz(fn, t, BAND_ROWS + t, 0)
    qc = (jnp.arange(t, dtype=jnp.int32)[:, None] + BAND_ROWS) // CHUNK
    kc = jnp.arange(BAND_ROWS + t, dtype=jnp.int32)[None, :] // CHUNK
    ok = (kc <= qc) & (kc >= qc - BAND_PREV_CHUNKS)
    return jnp.where(ok[None], bias, NEG)


def _layer(x2, nb, seq, wts, l, tabs, caches, state_bufs, lam_init, tm_proj, tm_mid, tm_ffn):
    cos_t, sin_t = tabs['rope']
    state_bufs, (dq_b, dk_b, dv_b, bq_b, bk_b, bv_b, qf, kf, v_b) = _proj(
        x2, wts, l, cos_t, sin_t, tm_proj, seq, state_bufs)
    if caches is None:
        o_a = _mla_prompt(qf, kf, v_b, nb, seq)
        o_b = _diff_prompt(dq_b, dk_b, dv_b, tabs['t5'], wts, l, nb, seq, lam_init)
        o_c = _band_prompt(bq_b, bk_b, bv_b, tabs['band'], nb, seq)
        prev, lp = jnp.zeros((1, nb, CONV_W - 1, D_FF), F32), 0
    else:
        c_ckv, c_krope, c_dk, c_dv, c_bk, c_bv, prev = caches
        plen = c_ckv.shape[1] // nb
        o_a = _mla_sample(qf, c_ckv, c_krope, state_bufs[0], state_bufs[1], wts['w_ukv'], l,
                          nb, seq, plen)
        o_b = _diff_sample(dq_b, dk_b, dv_b, c_dk, c_dv, tabs['t5'], wts, l, nb, seq, plen,
                           lam_init)
        o_c = _band_sample(bq_b, bk_b, bv_b, c_bk, c_bv, tabs['band'], l, nb, seq,
                           c_bk.shape[2] // BAND_HEADS)
        lp = l
    x1 = _oproj_ln(o_a, o_b, o_c, x2, wts, l, tm_mid)
    x_out, conv_state = _ffn(x1, prev, lp, wts, l, seq, tm_ffn)
    return x_out, state_bufs, conv_state


def _final_states(bufs, conv, nb, seq):
    ckv, krope, dk, dv, bk, bv = bufs
    keep = min(BAND_ROWS, seq)
    return (ckv.reshape(DEPTH, nb, seq, MLA_KV_RANK), krope.reshape(DEPTH, nb, seq, MLA_ROPE),
            dk.reshape(DEPTH, nb, seq, DIFF_HEADS, 2 * DIFF_QK),
            dv.reshape(DEPTH, nb, seq, DIFF_HEADS, DIFF_V),
            bk.reshape(DEPTH, nb, keep, BAND_HEADS, BAND_DIM),
            bv.reshape(DEPTH, nb, keep, BAND_HEADS, BAND_DIM), jnp.stack(conv))


def kernel(x_prompt, x_sample, cache_mla_ckv, cache_mla_krope, cache_diff_k, cache_diff_v, cache_band_k, cache_band_v, state_ffn_conv, t5_table, w_in, mla_q_norm, mla_w_uq, mla_kv_norm, mla_w_ukv, diff_lq1, diff_lk1, diff_lq2, diff_lk2, diff_subln, band_rel_table, w_o, ln1_g, ln1_b, ffn_w_gate, ffn_w_up, ffn_conv_w, ffn_conv_b, ffn_w_down, ln2_g, ln2_b):
    nb_p, seq_p, _ = x_prompt.shape
    nb_s, seq_s, _ = x_sample.shape
    past_len = cache_mla_ckv.shape[2]
    band_len = cache_band_k.shape[2]
    assert seq_p % ATT_T == 0 and seq_s == CHUNK and past_len % CHUNK == 0
    assert band_len == BAND_ROWS

    w_in_p, w_uq_p, w_ukv_p = _prep_weights(w_in, mla_w_uq, mla_w_ukv)
    vec = lambda a: a.reshape(DEPTH, 1, -1)
    wts = {'w_in': w_in_p, 'w_uq': w_uq_p, 'w_ukv': w_ukv_p,
           'q_norm': vec(mla_q_norm), 'kv_norm': vec(mla_kv_norm),
           'lam': jnp.stack([diff_lq1, diff_lk1, diff_lq2, diff_lk2], axis=1),
           'subln': vec(diff_subln), 'w_o': w_o.astype(BF16),
           'ln1_g': vec(ln1_g), 'ln1_b': vec(ln1_b),
           'w_gate': ffn_w_gate.astype(BF16), 'w_up': ffn_w_up.astype(BF16),
           'w_down': ffn_w_down.astype(BF16), 'conv_w': ffn_conv_w, 'conv_b': vec(ffn_conv_b),
           'ln2_g': vec(ln2_g), 'ln2_b': vec(ln2_b)}
    caches = (cache_mla_ckv.reshape(DEPTH, nb_s * past_len, MLA_KV_RANK),
              cache_mla_krope.reshape(DEPTH, nb_s * past_len, MLA_ROPE),
              cache_diff_k.reshape(DEPTH, nb_s, past_len * DIFF_HEADS, 128),
              cache_diff_v.reshape(DEPTH, nb_s, past_len * DIFF_HEADS, 128),
              cache_band_k.reshape(DEPTH, nb_s, band_len * BAND_HEADS, 128),
              cache_band_v.reshape(DEPTH, nb_s, band_len * BAND_HEADS, 128),
              state_ffn_conv)

    t = ATT_T
    pos_p = jnp.arange(seq_p, dtype=jnp.int32)
    pos_s = past_len + jnp.arange(seq_s, dtype=jnp.int32)
    tm_s = nb_s * seq_s
    rope_s = tuple(jnp.tile(tb, (nb_s, 1)) for tb in _rope_tables(pos_s))
    tabs_p = {'rope': _rope_tables(pos_p), 't5': _t5_bias(t5_table, t, seq_p, t - seq_p)}
    tabs_s = {'rope': rope_s, 't5': _t5_bias(t5_table, seq_s, past_len + seq_s, -past_len)}

    y_p = x_prompt.reshape(nb_p * seq_p, D_MODEL)
    y_s = x_sample.reshape(tm_s, D_MODEL)
    bufs_p = bufs_s = None
    conv_p, conv_s = [], []
    for l in range(DEPTH):
        lam_init = 0.8 - 0.6 * math.exp(-0.3 * l)
        band_p = _band_bias(band_rel_table[l], t)
        band_s = _band_bias(band_rel_table[l], seq_s)
        y_p, bufs_p, cs = _layer(y_p, nb_p, seq_p, wts, l, dict(tabs_p, band=band_p), None,
                                 bufs_p, lam_init, tm_proj=256, tm_mid=512, tm_ffn=512)
        conv_p.append(cs)
        y_s, bufs_s, cs = _layer(y_s, nb_s, seq_s, wts, l, dict(tabs_s, band=band_s), caches,
                                 bufs_s, lam_init, tm_proj=256, tm_mid=512, tm_ffn=tm_s)
        conv_s.append(cs)
    return (y_p.reshape(nb_p, seq_p, D_MODEL), y_s.reshape(nb_s, seq_s, D_MODEL),
            *_final_states(bufs_p, conv_p, nb_p, seq_p),
            *_final_states(bufs_s, conv_s, nb_s, seq_s))
```

```python
import functools
import math

import jax
import jax.numpy as jnp
import numpy as np
from jax import lax
from jax.experimental import pallas as pl
from jax.experimental.pallas import tpu as pltpu

F32 = jnp.float32
BF16 = jnp.bfloat16

D_MODEL = 2048
DEPTH = 2
CHUNK = 64
MLA_HEADS = 6
MLA_Q_RANK = 512
MLA_KV_RANK = 256
MLA_NOPE = 128
MLA_ROPE = 64
MLA_V = 128
ROPE_THETA = 10000.0
DIFF_HEADS = 4
DIFF_QK = 64
DIFF_V = 128
BAND_HEADS = 6
BAND_DIM = 128
BAND_PREV_CHUNKS = 8
BAND_REL_CLIP = 256
T5_BUCKETS = 32
T5_MAX_DIST = 128
D_FF = 5632
CONV_W = 3
LN_EPS = 1e-5
RMS_EPS = 1e-6
DEEPNORM_ALPHA = (2 * DEPTH) ** 0.25

NEG = -1e30
MLA_QK = 256
MLA_SCALE = (MLA_NOPE + MLA_ROPE) ** -0.5
DIFF_SCALE = DIFF_QK ** -0.5
BAND_SCALE = BAND_DIM ** -0.5
BAND_ROWS = BAND_PREV_CHUNKS * CHUNK
ATT_T = 256
OPROJ_SUB = 128
VMEM_LIMIT = 56 * 1024 * 1024

_O_CQ, _O_CKV, _O_DQ, _O_DK, _O_DV = 0, 512, 768, 1280, 1792
_O_BQ, _O_BK, _O_BV, _O_KR, _O_KRS, _IN_P = 2304, 3072, 3840, 4608, 4736, 4864
_O_MIX_DIFF = MLA_HEADS * MLA_V
_O_MIX_BAND = _O_MIX_DIFF + DIFF_HEADS * DIFF_V
_N_STATE = 6


def _cparams(sem):
    return pltpu.CompilerParams(dimension_semantics=sem, vmem_limit_bytes=VMEM_LIMIT)


def _dot(a, b):
    return jnp.dot(a, b, preferred_element_type=F32)


def _dot_nt(a, b):
    return lax.dot_general(a, b, (((1,), (1,)), ((), ())), preferred_element_type=F32)


def _rms(x, g):
    return x * lax.rsqrt(jnp.mean(x * x, axis=-1, keepdims=True) + RMS_EPS) * g


def _layer_norm(x, g, b):
    mu = jnp.mean(x, axis=-1, keepdims=True)
    xc = x - mu
    var = jnp.mean(xc * xc, axis=-1, keepdims=True)
    return xc * lax.rsqrt(var + LN_EPS) * g + b


def _layer_block(arr, l):
    zeros = (0,) * (arr.ndim - 1)
    return pl.BlockSpec((None,) + arr.shape[1:], lambda *_: (l,) + zeros,
                        pipeline_mode=pl.Buffered(1))


def _heads_to_rows(o_ref, val, heads):
    rows = val.shape[0]
    for h in range(heads):
        o_ref[pl.ds(h, rows, stride=heads), :] = val[:, h * 128:(h + 1) * 128]


def _head_rows(ref, h, rows, heads):
    return ref[pl.ds(h, rows, stride=heads), :]


def _proj_kernel(*refs, layer, n_alias, tiles_per_seq, band_from):
    (x_ref, win_ref, wuq_ref, wukv_ref, qn_ref, kvn_ref, cos_ref, sin_ref) = refs[:8]
    (ckv_o, krope_o, dk_o, dv_o, bk_o, bv_o,
     dq_b, dk_b, dv_b, bq_b, bk_b, bv_b, qf_b, kf_b, v_b) = refs[8 + n_alias:]

    def put(ref, write):
        if n_alias:
            write(ref)
            return
        for d in range(DEPTH):
            if d == layer:
                write(ref.at[d])
            else:
                ref[d] = jnp.zeros(ref.shape[1:], F32)

    def store(val):
        def write(r):
            r[...] = val
        return write

    def store_heads(val, heads):
        return lambda r: _heads_to_rows(r, val, heads)

    xb = x_ref[...].astype(BF16)
    cos = cos_ref[...]
    sin = sin_ref[...]

    def seg(off, n):
        return _dot(xb, win_ref[:, off:off + n])

    cq = _rms(seg(_O_CQ, MLA_Q_RANK), qn_ref[...]).astype(BF16)
    nh = MLA_HEADS * 128
    q = _dot(cq, wuq_ref[...])
    for h in range(MLA_HEADS):
        lo = h * 128
        qr = q[:, nh + lo:nh + lo + 128] * cos + q[:, 2 * nh + lo:2 * nh + lo + 128] * sin
        qf_b[h, :, 0:128] = (q[:, lo:lo + 128] * MLA_SCALE).astype(BF16)
        qf_b[h, :, 128:256] = (qr * MLA_SCALE).astype(BF16)

    ckv = _rms(seg(_O_CKV, MLA_KV_RANK), kvn_ref[...])
    put(ckv_o, store(ckv))
    ckvb = ckv.astype(BF16)
    krs = seg(_O_KR, 256)
    kr = krs[:, 0:128] * cos + krs[:, 128:256] * sin
    put(krope_o, store(kr[:, 0:MLA_ROPE]))
    krb = kr.astype(BF16)
    kv = _dot(ckvb, wukv_ref[...])
    for h in range(MLA_HEADS):
        lo = h * 128
        kf_b[h, :, 0:128] = kv[:, lo:lo + 128].astype(BF16)
        kf_b[h, :, 128:256] = krb
    v_b[...] = kv[:, nh:2 * nh].astype(BF16)

    dq_b[...] = (seg(_O_DQ, 512) * DIFF_SCALE).astype(BF16)
    dk = seg(_O_DK, 512)
    put(dk_o, store_heads(dk, DIFF_HEADS))
    dk_b[...] = dk.astype(BF16)
    dv = seg(_O_DV, 512)
    put(dv_o, store_heads(dv, DIFF_HEADS))
    dv_b[...] = dv.astype(BF16)
    bq_b[...] = (seg(_O_BQ, 768) * BAND_SCALE).astype(BF16)
    bk = seg(_O_BK, 768)
    bk_b[...] = bk.astype(BF16)
    bv = seg(_O_BV, 768)
    bv_b[...] = bv.astype(BF16)

    def band_state():
        put(bk_o, store_heads(bk, BAND_HEADS))
        put(bv_o, store_heads(bv, BAND_HEADS))

    if band_from == 0:
        band_state()
    else:
        pl.when(pl.program_id(0) % tiles_per_seq >= band_from)(band_state)


def _proj(x2, wts, l, cos_t, sin_t, tm, seq, state_bufs):
    m = x2.shape[0]
    period = cos_t.shape[0] // tm
    keep = min(BAND_ROWS, seq)
    n_alias = 0 if state_bufs is None else _N_STATE
    lead, li = (DEPTH, 0) if state_bufs is None else (None, l)
    if seq > tm:
        tps, kt = seq // tm, keep // tm
        band_from = tps - kt
        band_idx = lambda t: (li, (t // tps) * kt + jnp.maximum(t % tps - band_from, 0), 0)
    else:
        assert keep == seq
        tps, band_from = 1, 0
        band_idx = lambda t: (li, t, 0)
    row = lambda n: pl.BlockSpec((tm, n), lambda t: (t, 0))
    srow = lambda n, k=1: pl.BlockSpec((lead, tm * k, n), lambda t: (li, t, 0))
    band = pl.BlockSpec((lead, tm * BAND_HEADS, 128), band_idx)
    tab = pl.BlockSpec((tm, 128), lambda t: (t % period, 0))
    head = pl.BlockSpec((MLA_HEADS, tm, MLA_QK), lambda t: (0, t, 0))
    st = lambda rows, n: jax.ShapeDtypeStruct((DEPTH, rows, n), F32)
    b16 = lambda n: jax.ShapeDtypeStruct((m, n), BF16)
    hb16 = jax.ShapeDtypeStruct((MLA_HEADS, m, MLA_QK), BF16)
    n_band = (m // seq) * keep * BAND_HEADS
    kern = functools.partial(_proj_kernel, layer=l, n_alias=n_alias, tiles_per_seq=tps,
                             band_from=band_from)
    ins = [x2, wts['w_in'], wts['w_uq'], wts['w_ukv'], wts['q_norm'], wts['kv_norm'],
           cos_t, sin_t]
    in_specs = [row(D_MODEL), _layer_block(wts['w_in'], l), _layer_block(wts['w_uq'], l),
                _layer_block(wts['w_ukv'], l), _layer_block(wts['q_norm'], l),
                _layer_block(wts['kv_norm'], l), tab, tab]
    if n_alias:
        ins += list(state_bufs)
        in_specs += [pl.BlockSpec(memory_space=pl.ANY)] * n_alias
    outs = pl.pallas_call(
        kern,
        grid=(m // tm,),
        in_specs=in_specs,
        out_specs=[srow(256), srow(64), srow(128, DIFF_HEADS), srow(128, DIFF_HEADS), band, band,
                   row(512), row(512), row(512), row(768), row(768), row(768),
                   head, head, row(768)],
        out_shape=[st(m, 256), st(m, 64), st(m * DIFF_HEADS, 128), st(m * DIFF_HEADS, 128),
                   st(n_band, 128), st(n_band, 128),
                   b16(512), b16(512), b16(512), b16(768), b16(768), b16(768),
                   hb16, hb16, b16(768)],
        input_output_aliases={8 + k: k for k in range(n_alias)},
        compiler_params=_cparams(("arbitrary",)),
        name="proj",
    )(*ins)
    return outs[:_N_STATE], outs[_N_STATE:]


def _chunk_mask(t):
    r = lax.broadcasted_iota(jnp.int32, (t, t), 0) // CHUNK
    c = lax.broadcasted_iota(jnp.int32, (t, t), 1) // CHUNK
    return r >= c


def _softmax_pv(blocks):
    m = None
    for s, _ in blocks:
        bm = jnp.max(s, axis=-1, keepdims=True)
        m = bm if m is None else jnp.maximum(m, bm)
    l = None
    acc = None
    for s, vb in blocks:
        p = jnp.exp(s - m)
        bl = jnp.sum(p, axis=-1, keepdims=True)
        ba = _dot(p.astype(BF16), vb)
        l = bl if l is None else l + bl
        acc = ba if acc is None else acc + ba
    return acc / l


def _causal_scores(qs, k, v_ref, n, t, mask, bias=None):
    out = []
    for q in qs:
        s = _dot_nt(q, k)
        if bias is not None:
            s = s + bias
        blocks = [(jnp.where(mask, s[:, n - t:], NEG), v_ref[n - t:n, :])]
        if n > t:
            blocks.append((s[:, :n - t], v_ref[0:n - t, :]))
        out.append(blocks)
    return out


def _TILE_ORDER(nq):
    return list(range(nq))[::-1]


def _pipelined(order, scores, finish):
    nxt = scores(order[0])
    for pos, i in enumerate(order):
        cur = nxt
        if pos + 1 < len(order):
            nxt = scores(order[pos + 1])
        finish(i, cur)


def _mla_prompt_kernel(q_ref, k_ref, v_ref, o_ref, *, nq):
    t = ATT_T
    mask = _chunk_mask(t)

    def scores(tile):
        n = (tile + 1) * t
        return _causal_scores([q_ref[0, n - t:n, :]], k_ref[0, 0:n, :], v_ref, n, t, mask)

    def finish(tile, blocks):
        o_ref[tile * t:(tile + 1) * t, :] = _softmax_pv(blocks[0]).astype(BF16)

    _pipelined(_TILE_ORDER(nq), scores, finish)


def _mla_prompt(qf, kf, v, nb, seq):
    return pl.pallas_call(
        functools.partial(_mla_prompt_kernel, nq=seq // ATT_T),
        grid=(nb, MLA_HEADS),
        in_specs=[pl.BlockSpec((1, seq, MLA_QK), lambda b, h: (h, b, 0)),
                  pl.BlockSpec((1, seq, MLA_QK), lambda b, h: (h, b, 0)),
                  pl.BlockSpec((seq, MLA_V), lambda b, h: (b, h))],
        out_specs=pl.BlockSpec((seq, MLA_V), lambda b, h: (b, h)),
        out_shape=jax.ShapeDtypeStruct((nb * seq, MLA_HEADS * MLA_V), BF16),
        compiler_params=_cparams(("arbitrary",) * 2),
        name="mla_prompt",
    )(qf, kf, v)


def _mla_sample_kernel(q_ref, cp_ref, rp_ref, cn_ref, rn_ref, wukv_ref, o_ref, q_scr, k_scr,
                       *, seq, past):
    nh = MLA_HEADS * 128
    lat = MLA_KV_RANK
    for h in range(MLA_HEADS):
        rows = slice(h * seq, (h + 1) * seq)
        w_k = wukv_ref[:, h * 128:(h + 1) * 128]
        q_scr[rows, 0:lat] = _dot_nt(q_ref[h, :, 0:MLA_NOPE], w_k).astype(BF16)
        q_scr[rows, lat:lat + 128] = q_ref[h, :, MLA_NOPE:MLA_QK]
    zeros = jnp.zeros((past, 128 - MLA_ROPE), BF16)
    for rows, c_ref, r_ref in ((slice(0, past), cp_ref, rp_ref),
                               (slice(past, past + seq), cn_ref, rn_ref)):
        n = rows.stop - rows.start
        k_scr[rows, 0:lat] = c_ref[...].astype(BF16)
        k_scr[rows, lat:lat + MLA_ROPE] = r_ref[...].astype(BF16)
        k_scr[rows, lat + MLA_ROPE:lat + 128] = zeros[0:n]
    q = q_scr[...]
    kp = k_scr[0:past, :]
    kn = k_scr[past:past + seq, :]
    o_lat = _softmax_pv([(_dot_nt(q, kp), kp[:, 0:lat]),
                         (_dot_nt(q, kn), kn[:, 0:lat])]).astype(BF16)
    for h in range(MLA_HEADS):
        w_v = wukv_ref[:, nh + h * 128:nh + (h + 1) * 128]
        o_ref[:, h * 128:(h + 1) * 128] = _dot(o_lat[h * seq:(h + 1) * seq], w_v).astype(BF16)


def _mla_sample(qf, c_past, r_past, c_new, r_new, w_ukv, l, nb, seq, past):
    lay = lambda rows, n: pl.BlockSpec((None, rows, n), lambda b: (l, b, 0))
    return pl.pallas_call(
        functools.partial(_mla_sample_kernel, seq=seq, past=past),
        grid=(nb,),
        in_specs=[pl.BlockSpec((MLA_HEADS, seq, MLA_QK), lambda b: (0, b, 0)),
                  lay(past, MLA_KV_RANK), lay(past, MLA_ROPE),
                  lay(seq, MLA_KV_RANK), lay(seq, MLA_ROPE), _layer_block(w_ukv, l)],
        out_specs=pl.BlockSpec((seq, MLA_HEADS * MLA_V), lambda b: (b, 0)),
        out_shape=jax.ShapeDtypeStruct((nb * seq, MLA_HEADS * MLA_V), BF16),
        scratch_shapes=[pltpu.VMEM((MLA_HEADS * seq, MLA_KV_RANK + 128), BF16),
                        pltpu.VMEM((past + seq, MLA_KV_RANK + 128), BF16)],
        compiler_params=_cparams(("arbitrary",)),
        name="mla_sample",
    )(qf, c_past, r_past, c_new, r_new, w_ukv)


def _diff_lambda(lam_ref, lam_init):
    v = lam_ref[...]
    a = jnp.sum(v[0:1] * v[1:2], axis=-1, keepdims=True)
    b = jnp.sum(v[2:3] * v[3:4], axis=-1, keepdims=True)
    return jnp.exp(a) - jnp.exp(b) + lam_init


def _split_q(q):
    lane = lax.broadcasted_iota(jnp.int32, q.shape, 1)
    qf = q.astype(F32)
    return (jnp.where(lane < DIFF_QK, qf, 0.0).astype(BF16),
            jnp.where(lane >= DIFF_QK, qf, 0.0).astype(BF16))


def _diff_finish(o0, o1, lam, g, lam_init):
    o = o0 - lam * o1
    return (_rms(o, g) * (1.0 - lam_init)).astype(BF16)


def _diff_prompt_kernel(q_ref, k_ref, v_ref, bias_ref, lam_ref, g_ref, o_ref, *, lam_init, nq):
    t = ATT_T
    seq = nq * t
    lam = _diff_lambda(lam_ref, lam_init)
    mask = _chunk_mask(t)

    def scores(item):
        tile, half = item
        n = (tile + 1) * t
        q = _split_q(q_ref[n - t:n, :])[half]
        return _causal_scores([q], k_ref[0:n, :], v_ref, n, t, mask,
                              bias=bias_ref[0, :, seq - n:seq])[0]

    first_half = {}

    def finish(item, blocks):
        tile, half = item
        o = _softmax_pv(blocks)
        if half == 0:
            first_half[tile] = o
        else:
            o_ref[tile * t:(tile + 1) * t, :] = _diff_finish(first_half.pop(tile), o, lam,
                                                             g_ref[...], lam_init)

    _pipelined([(tile, half) for tile in _TILE_ORDER(nq) for half in (0, 1)], scores, finish)


def _diff_prompt(dq, dk, dv, bias, wts, l, nb, seq, lam_init):
    t = ATT_T
    return pl.pallas_call(
        functools.partial(_diff_prompt_kernel, lam_init=lam_init, nq=seq // t),
        grid=(nb, DIFF_HEADS),
        in_specs=[pl.BlockSpec((seq, 128), lambda b, h: (b, h)),
                  pl.BlockSpec((seq, 128), lambda b, h: (b, h)),
                  pl.BlockSpec((seq, 128), lambda b, h: (b, h)),
                  pl.BlockSpec((1, t, seq), lambda b, h: (h, 0, 0)),
                  _layer_block(wts['lam'], l), _layer_block(wts['subln'], l)],
        out_specs=pl.BlockSpec((seq, DIFF_V), lambda b, h: (b, h)),
        out_shape=jax.ShapeDtypeStruct((nb * seq, DIFF_HEADS * DIFF_V), BF16),
        compiler_params=_cparams(("arbitrary",) * 2),
        name="diff_prompt",
    )(dq, dk, dv, bias, wts['lam'], wts['subln'])


def _diff_sample_kernel(q_ref, kp_ref, vp_ref, kn_ref, vn_ref, bias_ref, lam_ref, g_ref,
                        o_ref, *, lam_init, past):
    lam = _diff_lambda(lam_ref, lam_init)
    head_ops = {}

    def operands(h):
        if h not in head_ops:
            cols = slice(h * 128, (h + 1) * 128)
            head_ops[h] = (_split_q(q_ref[:, cols]),
                           _head_rows(kp_ref, h, past, DIFF_HEADS).astype(BF16),
                           _head_rows(vp_ref, h, past, DIFF_HEADS).astype(BF16),
                           kn_ref[:, cols], vn_ref[:, cols])
        return head_ops[h]

    def scores(item):
        h, half = item
        qs, kp, vp, kn, vn = operands(h)
        return [(_dot_nt(qs[half], kp) + bias_ref[h, :, 0:past], vp),
                (_dot_nt(qs[half], kn) + bias_ref[h, :, past:], vn)]

    first_half = {}

    def finish(item, blocks):
        h, half = item
        o = _softmax_pv(blocks)
        if half == 0:
            first_half[h] = o
        else:
            o_ref[:, h * 128:(h + 1) * 128] = _diff_finish(first_half.pop(h), o, lam,
                                                           g_ref[...], lam_init)

    _pipelined([(h, half) for h in range(DIFF_HEADS) for half in (0, 1)], scores, finish)


def _diff_sample(dq, dk_new, dv_new, k_past, v_past, bias, wts, l, nb, seq, past, lam_init):
    width = DIFF_HEADS * 128
    new = pl.BlockSpec((seq, width), lambda b: (b, 0))
    cache = pl.BlockSpec((None, None, past * DIFF_HEADS, 128), lambda b: (l, b, 0, 0))
    return pl.pallas_call(
        functools.partial(_diff_sample_kernel, lam_init=lam_init, past=past),
        grid=(nb,),
        in_specs=[new, cache, cache, new, new,
                  pl.BlockSpec(bias.shape, lambda b: (0, 0, 0)),
                  _layer_block(wts['lam'], l), _layer_block(wts['subln'], l)],
        out_specs=new,
        out_shape=jax.ShapeDtypeStruct((nb * seq, width), BF16),
        compiler_params=_cparams(("arbitrary",)),
        name="diff_sample",
    )(dq, k_past, v_past, dk_new, dv_new, bias, wts['lam'], wts['subln'])


def _band_prompt_kernel(q_ref, k_ref, v_ref, bias_ref, o_ref, *, nq):
    t = ATT_T
    full = BAND_ROWS + t

    def scores(tile):
        start = max(0, tile * t - BAND_ROWS)
        width = (tile + 1) * t - start
        s = (_dot_nt(q_ref[tile * t:(tile + 1) * t, :], k_ref[start:start + width, :])
             + bias_ref[0, :, full - width:full])
        return [(s, v_ref[start:start + width, :])]

    def finish(tile, blocks):
        o_ref[tile * t:(tile + 1) * t, :] = _softmax_pv(blocks).astype(BF16)

    _pipelined(_TILE_ORDER(nq), scores, finish)


def _band_prompt(bq, bk, bv, bias, nb, seq):
    t = ATT_T
    blk = pl.BlockSpec((seq, BAND_DIM), lambda b, h: (b, h))
    return pl.pallas_call(
        functools.partial(_band_prompt_kernel, nq=seq // t),
        grid=(nb, BAND_HEADS),
        in_specs=[blk, blk, blk,
                  pl.BlockSpec((1, t, BAND_ROWS + t), lambda b, h: (h, 0, 0))],
        out_specs=blk,
        out_shape=jax.ShapeDtypeStruct((nb * seq, BAND_HEADS * BAND_DIM), BF16),
        compiler_params=_cparams(("arbitrary",) * 2),
        name="band_prompt",
    )(bq, bk, bv, bias)


def _band_sample_kernel(q_ref, kp_ref, vp_ref, kn_ref, vn_ref, bias_ref, o_ref, *, past):
    def scores(h):
        cols = slice(h * 128, (h + 1) * 128)
        kp = _head_rows(kp_ref, h, past, BAND_HEADS).astype(BF16)
        vp = _head_rows(vp_ref, h, past, BAND_HEADS).astype(BF16)
        q = q_ref[:, cols]
        return [(_dot_nt(q, kp) + bias_ref[h, :, 0:past], vp),
                (_dot_nt(q, kn_ref[:, cols]) + bias_ref[h, :, past:], vn_ref[:, cols])]

    def finish(h, blocks):
        o_ref[:, h * 128:(h + 1) * 128] = _softmax_pv(blocks).astype(BF16)

    _pipelined(list(range(BAND_HEADS)), scores, finish)


def _band_sample(bq, bk_new, bv_new, k_past, v_past, bias, l, nb, seq, past):
    width = BAND_HEADS * BAND_DIM
    new = pl.BlockSpec((seq, width), lambda b: (b, 0))
    cache = pl.BlockSpec((None, None, past * BAND_HEADS, 128), lambda b: (l, b, 0, 0))
    return pl.pallas_call(
        functools.partial(_band_sample_kernel, past=past),
        grid=(nb,),
        in_specs=[new, cache, cache, new, new, pl.BlockSpec(bias.shape, lambda b: (0, 0, 0))],
        out_specs=new,
        out_shape=jax.ShapeDtypeStruct((nb * seq, width), BF16),
        compiler_params=_cparams(("arbitrary",)),
        name="band_sample",
    )(bq, k_past, v_past, bk_new, bv_new, bias)


def _oproj_kernel(a_ref, b_ref, c_ref, x_ref, wo_ref, g_ref, beta_ref, o_ref):
    def mix(r):
        rows = slice(r, r + OPROJ_SUB)
        return (_dot(a_ref[rows, :], wo_ref[0:_O_MIX_DIFF, :])
                + _dot(b_ref[rows, :], wo_ref[_O_MIX_DIFF:_O_MIX_BAND, :])
                + _dot(c_ref[rows, :], wo_ref[_O_MIX_BAND:, :]))

    def finish(r, m):
        rows = slice(r, r + OPROJ_SUB)
        o_ref[rows, :] = _layer_norm(DEEPNORM_ALPHA * x_ref[rows, :] + m,
                                     g_ref[...], beta_ref[...])

    _pipelined(list(range(0, x_ref.shape[0], OPROJ_SUB)), mix, finish)


def _oproj_ln(o_a, o_b, o_c, x2, wts, l, tm):
    m = x2.shape[0]
    row = lambda n: pl.BlockSpec((tm, n), lambda t: (t, 0))
    return pl.pallas_call(
        _oproj_kernel,
        grid=(m // tm,),
        in_specs=[row(o_a.shape[1]), row(o_b.shape[1]), row(o_c.shape[1]), row(D_MODEL),
                  _layer_block(wts['w_o'], l), _layer_block(wts['ln1_g'], l),
                  _layer_block(wts['ln1_b'], l)],
        out_specs=row(D_MODEL),
        out_shape=jax.ShapeDtypeStruct((m, D_MODEL), F32),
        compiler_params=_cparams(("arbitrary",)),
        name="oproj_ln",
    )(o_a, o_b, o_c, x2, wts['w_o'], wts['ln1_g'], wts['ln1_b'])


def _ffn_kernel(x_ref, prev_ref, wg_ref, wu_ref, wd_ref, cw_ref, cb_ref, g_ref, beta_ref,
                o_ref, st_ref, xb_ref, acc_ref, carry_ref, *, tf, seg, tiles_per_seq):
    i = pl.program_id(0)
    j = pl.program_id(1)
    nj = pl.num_programs(1)
    tm = x_ref.shape[0]
    col = pl.ds(pl.multiple_of(j * tf, tf), tf)

    @pl.when(j == 0)
    def _():
        xb_ref[...] = x_ref[...].astype(BF16)
        acc_ref[...] = jnp.zeros_like(acc_ref)

    xb = xb_ref[...]
    g = _dot(xb, wg_ref[...])
    u = _dot(xb, wu_ref[...])
    cw = cw_ref[...]
    cb = cb_ref[...]
    rows = lax.broadcasted_iota(jnp.int32, (seg, tf), 0)
    seq_start = (i % tiles_per_seq) == 0

    hs = []
    for s in range(tm // seg):
        gs = g[s * seg:(s + 1) * seg]
        state = prev_ref[s]
        if tiles_per_seq > 1:
            state = jnp.where(seq_start, state, carry_ref[:, col])
        pm2, pm1 = state[0:1], state[1:2]
        gm1 = jnp.where(rows == 0, pm1, pltpu.roll(gs, 1, 0))
        gm2 = jnp.where(rows == 0, pm2, jnp.where(rows == 1, pm1, pltpu.roll(gs, 2, 0)))
        gc = cb + cw[0:1] * gm2 + cw[1:2] * gm1 + cw[2:3] * gs
        hs.append(gc * (1.0 / (1.0 + jnp.exp(-gc))) * u[s * seg:(s + 1) * seg])
        last = gs[seg - 2:seg]
        st_ref[s, :, col] = last
        if tiles_per_seq > 1:
            carry_ref[:, col] = last
    h = hs[0] if len(hs) == 1 else jnp.concatenate(hs, axis=0)
    acc_ref[...] += _dot(h.astype(BF16), wd_ref[...])

    @pl.when(j == nj - 1)
    def _():
        o_ref[...] = _layer_norm(DEEPNORM_ALPHA * x_ref[...] + acc_ref[...],
                                 g_ref[...], beta_ref[...])


def _ffn(x2, prev, lp, wts, l, seq, tm, tf=512):
    m = x2.shape[0]
    nseq = m // seq
    seg = min(seq, tm)
    spt = tm // seg
    tps = seq // seg
    kern = functools.partial(_ffn_kernel, tf=tf, seg=seg, tiles_per_seq=tps)
    vec = lambda n: pl.BlockSpec((None, n, tf), lambda i, j: (l, 0, j))
    ln = lambda arr: pl.BlockSpec((None, 1, D_MODEL), lambda i, j: (l, 0, 0))
    return pl.pallas_call(
        kern,
        grid=(m // tm, D_FF // tf),
        in_specs=[pl.BlockSpec((tm, D_MODEL), lambda i, j: (i, 0)),
                  pl.BlockSpec((None, spt, CONV_W - 1, tf), lambda i, j: (lp, i // tps, 0, j)),
                  pl.BlockSpec((None, D_MODEL, tf), lambda i, j: (l, 0, j)),
                  pl.BlockSpec((None, D_MODEL, tf), lambda i, j: (l, 0, j)),
                  pl.BlockSpec((None, tf, D_MODEL), lambda i, j: (l, j, 0)),
                  vec(CONV_W), vec(1), ln(wts['ln2_g']), ln(wts['ln2_b'])],
        out_specs=[pl.BlockSpec((tm, D_MODEL), lambda i, j: (i, 0)),
                   pl.BlockSpec((spt, CONV_W - 1, D_FF), lambda i, j: (i // tps, 0, 0))],
        out_shape=[jax.ShapeDtypeStruct((m, D_MODEL), F32),
                   jax.ShapeDtypeStruct((nseq, CONV_W - 1, D_FF), F32)],
        scratch_shapes=[pltpu.VMEM((tm, D_MODEL), BF16),
                        pltpu.VMEM((tm, D_MODEL), F32),
                        pltpu.VMEM((CONV_W - 1, D_FF), F32)],
        compiler_params=_cparams(("arbitrary", "arbitrary")),
        name="ffn",
    )(x2, prev, wts['w_gate'], wts['w_up'], wts['w_down'], wts['conv_w'], wts['conv_b'],
      wts['ln2_g'], wts['ln2_b'])


def _prep_weights(w_in, mla_w_uq, mla_w_ukv):
    half = MLA_ROPE // 2
    swap = lambda w: jnp.concatenate([w[..., half:], w[..., :half]], axis=-1)
    pad = lambda w: jnp.concatenate([w, jnp.zeros_like(w)], axis=-1)
    offs = np.cumsum((512, 256, 64, 512, 512, 512, 768, 768, 768))[:-1].tolist()
    c_q, c_kv, k_r, d_q, d_k, d_v, b_q, b_k, b_v = jnp.split(w_in, offs, axis=-1)
    w_in_p = jnp.concatenate([c_q, c_kv, d_q, d_k, d_v, b_q, b_k, b_v, pad(k_r), pad(swap(k_r))],
                             axis=-1).astype(BF16)
    uq = mla_w_uq.reshape(DEPTH, MLA_Q_RANK, MLA_HEADS, MLA_NOPE + MLA_ROPE)
    nope, rope = uq[..., :MLA_NOPE], uq[..., MLA_NOPE:]
    flat = lambda w: w.reshape(DEPTH, w.shape[1], -1)
    w_uq_p = jnp.concatenate([flat(nope), flat(pad(rope)), flat(pad(swap(rope)))],
                             axis=-1).astype(BF16)
    ukv = mla_w_ukv.reshape(DEPTH, MLA_KV_RANK, MLA_HEADS, MLA_NOPE + MLA_V)
    w_ukv_p = jnp.concatenate([flat(ukv[..., :MLA_NOPE]), flat(ukv[..., MLA_NOPE:])],
                              axis=-1).astype(BF16)
    return w_in_p, w_uq_p, w_ukv_p


def _rope_tables(pos):
    half = MLA_ROPE // 2
    inv = ROPE_THETA ** (-jnp.arange(half, dtype=F32) / half)
    ang = pos.astype(F32)[:, None] * inv
    cos, sin = jnp.cos(ang), jnp.sin(ang)
    z = jnp.zeros((pos.shape[0], 128 - MLA_ROPE), F32)
    return jnp.concatenate([cos, cos, z], -1), jnp.concatenate([-sin, sin, z], -1)


def _t5_bucket(rel):
    half = T5_BUCKETS // 2
    exact = half // 2
    n = jnp.abs(rel)
    nf = jnp.maximum(n, 1).astype(F32)
    large = exact + (jnp.log(nf / exact) / math.log(T5_MAX_DIST / exact)
                     * (half - exact)).astype(jnp.int32)
    large = jnp.minimum(large, half - 1)
    return jnp.where(rel > 0, half, 0) + jnp.where(n < exact, n, large)


def _toeplitz(fn, rows, cols, shift):
    p = rows + cols
    k = jnp.arange(p, dtype=jnp.int32)
    u = fn(jnp.where(k < cols, k, k - p) + shift).astype(F32)
    flat = jnp.tile(u, (1, rows))[:, :rows * (p - 1)]
    return flat.reshape(u.shape[0], rows, p - 1)[:, :, :cols]


def _t5_bias(t5_table, rows, cols, shift):
    return _toeplitz(lambda rel: t5_table[_t5_bucket(rel)].T, rows, cols, shift)


def _band_bias(rel_table, t):
    def fn(d):
        rel = jnp.clip(BAND_ROWS - d, -BAND_REL_CLIP, BAND_REL_CLIP) + BAND_REL_CLIP
        return rel_table[:, rel]
    bias = _toeplitz(fn, t, BAND_ROWS + t, 0)
    qc = (jnp.arange(t, dtype=jnp.int32)[:, None] + BAND_ROWS) // CHUNK
    kc = jnp.arange(BAND_ROWS + t, dtype=jnp.int32)[None, :] // CHUNK
    ok = (kc <= qc) & (kc >= qc - BAND_PREV_CHUNKS)
    return jnp.where(ok[None], bias, NEG)


def _layer(x2, nb, seq, wts, l, tabs, caches, state_bufs, lam_init, tm_proj, tm_mid, tm_ffn):
    cos_t, sin_t = tabs['rope']
    state_bufs, (dq_b, dk_b, dv_b, bq_b, bk_b, bv_b, qf, kf, v_b) = _proj(
        x2, wts, l, cos_t, sin_t, tm_proj, seq, state_bufs)
    if caches is None:
        o_a = _mla_prompt(qf, kf, v_b, nb, seq)
        o_b = _diff_prompt(dq_b, dk_b, dv_b, tabs['t5'], wts, l, nb, seq, lam_init)
        o_c = _band_prompt(bq_b, bk_b, bv_b, tabs['band'], nb, seq)
        prev, lp = jnp.zeros((1, nb, CONV_W - 1, D_FF), F32), 0
    else:
        c_ckv, c_krope, c_dk, c_dv, c_bk, c_bv, prev = caches
        plen = c_ckv.shape[1] // nb
        o_a = _mla_sample(qf, c_ckv, c_krope, state_bufs[0], state_bufs[1], wts['w_ukv'], l,
                          nb, seq, plen)
        o_b = _diff_sample(dq_b, dk_b, dv_b, c_dk, c_dv, tabs['t5'], wts, l, nb, seq, plen,
                           lam_init)
        o_c = _band_sample(bq_b, bk_b, bv_b, c_bk, c_bv, tabs['band'], l, nb, seq,
                           c_bk.shape[2] // BAND_HEADS)
        lp = l
    x1 = _oproj_ln(o_a, o_b, o_c, x2, wts, l, tm_mid)
    x_out, conv_state = _ffn(x1, prev, lp, wts, l, seq, tm_ffn)
    return x_out, state_bufs, conv_state


def _final_states(bufs, conv, nb, seq):
    ckv, krope, dk, dv, bk, bv = bufs
    keep = min(BAND_ROWS, seq)
    return (ckv.reshape(DEPTH, nb, seq, MLA_KV_RANK), krope.reshape(DEPTH, nb, seq, MLA_ROPE),
            dk.reshape(DEPTH, nb, seq, DIFF_HEADS, 2 * DIFF_QK),
            dv.reshape(DEPTH, nb, seq, DIFF_HEADS, DIFF_V),
            bk.reshape(DEPTH, nb, keep, BAND_HEADS, BAND_DIM),
            bv.reshape(DEPTH, nb, keep, BAND_HEADS, BAND_DIM), jnp.stack(conv))


def kernel(x_prompt, x_sample, cache_mla_ckv, cache_mla_krope, cache_diff_k, cache_diff_v, cache_band_k, cache_band_v, state_ffn_conv, t5_table, w_in, mla_q_norm, mla_w_uq, mla_kv_norm, mla_w_ukv, diff_lq1, diff_lk1, diff_lq2, diff_lk2, diff_subln, band_rel_table, w_o, ln1_g, ln1_b, ffn_w_gate, ffn_w_up, ffn_conv_w, ffn_conv_b, ffn_w_down, ln2_g, ln2_b):
    nb_p, seq_p, _ = x_prompt.shape
    nb_s, seq_s, _ = x_sample.shape
    past_len = cache_mla_ckv.shape[2]
    band_len = cache_band_k.shape[2]
    assert seq_p % ATT_T == 0 and seq_s == CHUNK and past_len % CHUNK == 0
    assert band_len == BAND_ROWS

    w_in_p, w_uq_p, w_ukv_p = _prep_weights(w_in, mla_w_uq, mla_w_ukv)
    vec = lambda a: a.reshape(DEPTH, 1, -1)
    wts = {'w_in': w_in_p, 'w_uq': w_uq_p, 'w_ukv': w_ukv_p,
           'q_norm': vec(mla_q_norm), 'kv_norm': vec(mla_kv_norm),
           'lam': jnp.stack([diff_lq1, diff_lk1, diff_lq2, diff_lk2], axis=1),
           'subln': vec(diff_subln), 'w_o': w_o.astype(BF16),
           'ln1_g': vec(ln1_g), 'ln1_b': vec(ln1_b),
           'w_gate': ffn_w_gate.astype(BF16), 'w_up': ffn_w_up.astype(BF16),
           'w_down': ffn_w_down.astype(BF16), 'conv_w': ffn_conv_w, 'conv_b': vec(ffn_conv_b),
           'ln2_g': vec(ln2_g), 'ln2_b': vec(ln2_b)}
    caches = (cache_mla_ckv.reshape(DEPTH, nb_s * past_len, MLA_KV_RANK),
              cache_mla_krope.reshape(DEPTH, nb_s * past_len, MLA_ROPE),
              cache_diff_k.reshape(DEPTH, nb_s, past_len * DIFF_HEADS, 128),
              cache_diff_v.reshape(DEPTH, nb_s, past_len * DIFF_HEADS, 128),
              cache_band_k.reshape(DEPTH, nb_s, band_len * BAND_HEADS, 128),
              cache_band_v.reshape(DEPTH, nb_s, band_len * BAND_HEADS, 128),
              state_ffn_conv)

    t = ATT_T
    pos_p = jnp.arange(seq_p, dtype=jnp.int32)
    pos_s = past_len + jnp.arange(seq_s, dtype=jnp.int32)
    tm_s = nb_s * seq_s
    rope_s = tuple(jnp.tile(tb, (nb_s, 1)) for tb in _rope_tables(pos_s))
    tabs_p = {'rope': _rope_tables(pos_p),
              't5': _t5_bias(t5_table, ATT_T, seq_p, ATT_T - seq_p)}
    tabs_s = {'rope': rope_s, 't5': _t5_bias(t5_table, seq_s, past_len + seq_s, -past_len)}

    y_p = x_prompt.reshape(nb_p * seq_p, D_MODEL)
    y_s = x_sample.reshape(tm_s, D_MODEL)
    bufs_p = bufs_s = None
    conv_p, conv_s = [], []
    for l in range(DEPTH):
        lam_init = 0.8 - 0.6 * math.exp(-0.3 * l)
        band_p = _band_bias(band_rel_table[l], t)
        band_s = _band_bias(band_rel_table[l], seq_s)
        y_p, bufs_p, cs = _layer(y_p, nb_p, seq_p, wts, l, dict(tabs_p, band=band_p), None,
                                 bufs_p, lam_init, tm_proj=256, tm_mid=512, tm_ffn=512)
        conv_p.append(cs)
        y_s, bufs_s, cs = _layer(y_s, nb_s, seq_s, wts, l, dict(tabs_s, band=band_s), caches,
                                 bufs_s, lam_init, tm_proj=256, tm_mid=512, tm_ffn=tm_s)
        conv_s.append(cs)
    return (y_p.reshape(nb_p, seq_p, D_MODEL), y_s.reshape(nb_s, seq_s, D_MODEL),
            *_final_states(bufs_p, conv_p, nb_p, seq_p),
            *_final_states(bufs_s, conv_s, nb_s, seq_s))
```

```python
import functools
import math

import jax
import jax.numpy as jnp
import numpy as np
from jax import lax
from jax.experimental import pallas as pl
from jax.experimental.pallas import tpu as pltpu

F32 = jnp.float32
BF16 = jnp.bfloat16

D_MODEL = 2048
DEPTH = 2
CHUNK = 64
MLA_HEADS = 6
MLA_Q_RANK = 512
MLA_KV_RANK = 256
MLA_NOPE = 128
MLA_ROPE = 64
MLA_V = 128
ROPE_THETA = 10000.0
DIFF_HEADS = 4
DIFF_QK = 64
DIFF_V = 128
BAND_HEADS = 6
BAND_DIM = 128
BAND_PREV_CHUNKS = 8
BAND_REL_CLIP = 256
T5_BUCKETS = 32
T5_MAX_DIST = 128
D_FF = 5632
CONV_W = 3
LN_EPS = 1e-5
RMS_EPS = 1e-6
DEEPNORM_ALPHA = (2 * DEPTH) ** 0.25

NEG = -1e30
MLA_QK = 256
MLA_SCALE = (MLA_NOPE + MLA_ROPE) ** -0.5
DIFF_SCALE = DIFF_QK ** -0.5
BAND_SCALE = BAND_DIM ** -0.5
BAND_ROWS = BAND_PREV_CHUNKS * CHUNK
ATT_T = 256
OPROJ_SUB = 128
VMEM_LIMIT = 56 * 1024 * 1024

_O_CQ, _O_CKV, _O_DQ, _O_DK, _O_DV = 0, 512, 768, 1280, 1792
_O_BQ, _O_BK, _O_BV, _O_KR, _O_KRS, _IN_P = 2304, 3072, 3840, 4608, 4736, 4864
_O_MIX_DIFF = MLA_HEADS * MLA_V
_O_MIX_BAND = _O_MIX_DIFF + DIFF_HEADS * DIFF_V
_N_STATE = 6


def _cparams(sem):
    return pltpu.CompilerParams(dimension_semantics=sem, vmem_limit_bytes=VMEM_LIMIT)


def _dot(a, b):
    return jnp.dot(a, b, preferred_element_type=F32)


def _dot_nt(a, b):
    return lax.dot_general(a, b, (((1,), (1,)), ((), ())), preferred_element_type=F32)


def _rms(x, g):
    return x * lax.rsqrt(jnp.mean(x * x, axis=-1, keepdims=True) + RMS_EPS) * g


def _layer_norm(x, g, b):
    mu = jnp.mean(x, axis=-1, keepdims=True)
    xc = x - mu
    var = jnp.mean(xc * xc, axis=-1, keepdims=True)
    return xc * lax.rsqrt(var + LN_EPS) * g + b


def _layer_block(arr, l):
    zeros = (0,) * (arr.ndim - 1)
    return pl.BlockSpec((None,) + arr.shape[1:], lambda *_: (l,) + zeros,
                        pipeline_mode=pl.Buffered(1))


def _heads_to_rows(o_ref, val, heads):
    rows = val.shape[0]
    for h in range(heads):
        o_ref[pl.ds(h, rows, stride=heads), :] = val[:, h * 128:(h + 1) * 128]


def _head_rows(ref, h, rows, heads):
    return ref[pl.ds(h, rows, stride=heads), :]


def _proj_kernel(*refs, layer, n_alias, tiles_per_seq, band_from):
    (x_ref, win_ref, wuq_ref, wukv_ref, qn_ref, kvn_ref, cos_ref, sin_ref) = refs[:8]
    (ckv_o, krope_o, dk_o, dv_o, bk_o, bv_o,
     dq_b, dk_b, dv_b, bq_b, bk_b, bv_b, qf_b, kf_b, v_b) = refs[8 + n_alias:]

    def put(ref, write):
        if n_alias:
            write(ref)
            return
        for d in range(DEPTH):
            if d == layer:
                write(ref.at[d])
            else:
                ref[d] = jnp.zeros(ref.shape[1:], F32)

    def store(val):
        def write(r):
            r[...] = val
        return write

    def store_heads(val, heads):
        return lambda r: _heads_to_rows(r, val, heads)

    xb = x_ref[...].astype(BF16)
    cos = cos_ref[...]
    sin = sin_ref[...]

    def seg(off, n):
        return _dot(xb, win_ref[:, off:off + n])

    cq = _rms(seg(_O_CQ, MLA_Q_RANK), qn_ref[...]).astype(BF16)
    ckv = _rms(seg(_O_CKV, MLA_KV_RANK), kvn_ref[...])
    put(ckv_o, store(ckv))
    ckvb = ckv.astype(BF16)
    krs = seg(_O_KR, 256)
    kr = krs[:, 0:128] * cos + krs[:, 128:256] * sin
    put(krope_o, store(kr[:, 0:MLA_ROPE]))
    krb = kr.astype(BF16)

    dq_b[...] = (seg(_O_DQ, 512) * DIFF_SCALE).astype(BF16)
    dk = seg(_O_DK, 512)
    put(dk_o, store_heads(dk, DIFF_HEADS))
    dk_b[...] = dk.astype(BF16)
    dv = seg(_O_DV, 512)
    put(dv_o, store_heads(dv, DIFF_HEADS))
    dv_b[...] = dv.astype(BF16)

    nh = MLA_HEADS * 128
    q = _dot(cq, wuq_ref[...])
    for h in range(MLA_HEADS):
        lo = h * 128
        qr = q[:, nh + lo:nh + lo + 128] * cos + q[:, 2 * nh + lo:2 * nh + lo + 128] * sin
        qf_b[h, :, 0:128] = (q[:, lo:lo + 128] * MLA_SCALE).astype(BF16)
        qf_b[h, :, 128:256] = (qr * MLA_SCALE).astype(BF16)

    kv = _dot(ckvb, wukv_ref[...])
    for h in range(MLA_HEADS):
        lo = h * 128
        kf_b[h, :, 0:128] = kv[:, lo:lo + 128].astype(BF16)
        kf_b[h, :, 128:256] = krb
    v_b[...] = kv[:, nh:2 * nh].astype(BF16)

    bq_b[...] = (seg(_O_BQ, 768) * BAND_SCALE).astype(BF16)
    bk = seg(_O_BK, 768)
    bk_b[...] = bk.astype(BF16)
    bv = seg(_O_BV, 768)
    bv_b[...] = bv.astype(BF16)

    def band_state():
        put(bk_o, store_heads(bk, BAND_HEADS))
        put(bv_o, store_heads(bv, BAND_HEADS))

    if band_from == 0:
        band_state()
    else:
        pl.when(pl.program_id(0) % tiles_per_seq >= band_from)(band_state)


def _proj(x2, wts, l, cos_t, sin_t, tm, seq, state_bufs):
    m = x2.shape[0]
    period = cos_t.shape[0] // tm
    keep = min(BAND_ROWS, seq)
    n_alias = 0 if state_bufs is None else _N_STATE
    lead, li = (DEPTH, 0) if state_bufs is None else (None, l)
    if seq > tm:
        tps, kt = seq // tm, keep // tm
        band_from = tps - kt
        band_idx = lambda t: (li, (t // tps) * kt + jnp.maximum(t % tps - band_from, 0), 0)
    else:
        assert keep == seq
        tps, band_from = 1, 0
        band_idx = lambda t: (li, t, 0)
    row = lambda n: pl.BlockSpec((tm, n), lambda t: (t, 0))
    srow = lambda n, k=1: pl.BlockSpec((lead, tm * k, n), lambda t: (li, t, 0))
    band = pl.BlockSpec((lead, tm * BAND_HEADS, 128), band_idx)
    tab = pl.BlockSpec((tm, 128), lambda t: (t % period, 0))
    head = pl.BlockSpec((MLA_HEADS, tm, MLA_QK), lambda t: (0, t, 0))
    st = lambda rows, n: jax.ShapeDtypeStruct((DEPTH, rows, n), F32)
    b16 = lambda n: jax.ShapeDtypeStruct((m, n), BF16)
    hb16 = jax.ShapeDtypeStruct((MLA_HEADS, m, MLA_QK), BF16)
    n_band = (m // seq) * keep * BAND_HEADS
    kern = functools.partial(_proj_kernel, layer=l, n_alias=n_alias, tiles_per_seq=tps,
                             band_from=band_from)
    ins = [x2, wts['w_in'], wts['w_uq'], wts['w_ukv'], wts['q_norm'], wts['kv_norm'],
           cos_t, sin_t]
    in_specs = [row(D_MODEL), _layer_block(wts['w_in'], l), _layer_block(wts['w_uq'], l),
                _layer_block(wts['w_ukv'], l), _layer_block(wts['q_norm'], l),
                _layer_block(wts['kv_norm'], l), tab, tab]
    if n_alias:
        ins += list(state_bufs)
        in_specs += [pl.BlockSpec(memory_space=pl.ANY)] * n_alias
    outs = pl.pallas_call(
        kern,
        grid=(m // tm,),
        in_specs=in_specs,
        out_specs=[srow(256), srow(64), srow(128, DIFF_HEADS), srow(128, DIFF_HEADS), band, band,
                   row(512), row(512), row(512), row(768), row(768), row(768),
                   head, head, row(768)],
        out_shape=[st(m, 256), st(m, 64), st(m * DIFF_HEADS, 128), st(m * DIFF_HEADS, 128),
                   st(n_band, 128), st(n_band, 128),
                   b16(512), b16(512), b16(512), b16(768), b16(768), b16(768),
                   hb16, hb16, b16(768)],
        input_output_aliases={8 + k: k for k in range(n_alias)},
        compiler_params=_cparams(("arbitrary",)),
        name="proj",
    )(*ins)
    return outs[:_N_STATE], outs[_N_STATE:]


def _chunk_mask(t):
    r = lax.broadcasted_iota(jnp.int32, (t, t), 0) // CHUNK
    c = lax.broadcasted_iota(jnp.int32, (t, t), 1) // CHUNK
    return r >= c


def _softmax_pv(blocks):
    m = None
    for s, _ in blocks:
        bm = jnp.max(s, axis=-1, keepdims=True)
        m = bm if m is None else jnp.maximum(m, bm)
    l = None
    acc = None
    for s, vb in blocks:
        p = jnp.exp(s - m)
        bl = jnp.sum(p, axis=-1, keepdims=True)
        ba = _dot(p.astype(BF16), vb)
        l = bl if l is None else l + bl
        acc = ba if acc is None else acc + ba
    return acc / l


def _causal_scores(qs, k, v_ref, n, t, mask, bias=None):
    out = []
    for q in qs:
        s = _dot_nt(q, k)
        if bias is not None:
            s = s + bias
        blocks = [(jnp.where(mask, s[:, n - t:], NEG), v_ref[n - t:n, :])]
        if n > t:
            blocks.append((s[:, :n - t], v_ref[0:n - t, :]))
        out.append(blocks)
    return out


def _TILE_ORDER(nq):
    return list(range(nq))[::-1]


def _pipelined(order, scores, finish):
    nxt = scores(order[0])
    for pos, i in enumerate(order):
        cur = nxt
        if pos + 1 < len(order):
            nxt = scores(order[pos + 1])
        finish(i, cur)


def _mla_prompt_kernel(q_ref, k_ref, v_ref, o_ref, *, nq):
    t = ATT_T
    mask = _chunk_mask(t)

    def scores(tile):
        n = (tile + 1) * t
        return _causal_scores([q_ref[0, n - t:n, :]], k_ref[0, 0:n, :], v_ref, n, t, mask)

    def finish(tile, blocks):
        o_ref[tile * t:(tile + 1) * t, :] = _softmax_pv(blocks[0]).astype(BF16)

    _pipelined(_TILE_ORDER(nq), scores, finish)


def _mla_prompt(qf, kf, v, nb, seq):
    return pl.pallas_call(
        functools.partial(_mla_prompt_kernel, nq=seq // ATT_T),
        grid=(nb, MLA_HEADS),
        in_specs=[pl.BlockSpec((1, seq, MLA_QK), lambda b, h: (h, b, 0)),
                  pl.BlockSpec((1, seq, MLA_QK), lambda b, h: (h, b, 0)),
                  pl.BlockSpec((seq, MLA_V), lambda b, h: (b, h))],
        out_specs=pl.BlockSpec((seq, MLA_V), lambda b, h: (b, h)),
        out_shape=jax.ShapeDtypeStruct((nb * seq, MLA_HEADS * MLA_V), BF16),
        compiler_params=_cparams(("arbitrary",) * 2),
        name="mla_prompt",
    )(qf, kf, v)


def _mla_sample_kernel(q_ref, cp_ref, rp_ref, cn_ref, rn_ref, wukv_ref, o_ref, q_scr, k_scr,
                       *, seq, past):
    nh = MLA_HEADS * 128
    lat = MLA_KV_RANK
    for h in range(MLA_HEADS):
        rows = slice(h * seq, (h + 1) * seq)
        w_k = wukv_ref[:, h * 128:(h + 1) * 128]
        q_scr[rows, 0:lat] = _dot_nt(q_ref[h, :, 0:MLA_NOPE], w_k).astype(BF16)
        q_scr[rows, lat:lat + 128] = q_ref[h, :, MLA_NOPE:MLA_QK]
    zeros = jnp.zeros((past, 128 - MLA_ROPE), BF16)
    for rows, c_ref, r_ref in ((slice(0, past), cp_ref, rp_ref),
                               (slice(past, past + seq), cn_ref, rn_ref)):
        n = rows.stop - rows.start
        k_scr[rows, 0:lat] = c_ref[...].astype(BF16)
        k_scr[rows, lat:lat + MLA_ROPE] = r_ref[...].astype(BF16)
        k_scr[rows, lat + MLA_ROPE:lat + 128] = zeros[0:n]
    q = q_scr[...]
    kp = k_scr[0:past, :]
    kn = k_scr[past:past + seq, :]
    o_lat = _softmax_pv([(_dot_nt(q, kp), kp[:, 0:lat]),
                         (_dot_nt(q, kn), kn[:, 0:lat])]).astype(BF16)
    for h in range(MLA_HEADS):
        w_v = wukv_ref[:, nh + h * 128:nh + (h + 1) * 128]
        o_ref[:, h * 128:(h + 1) * 128] = _dot(o_lat[h * seq:(h + 1) * seq], w_v).astype(BF16)


def _mla_sample(qf, c_past, r_past, c_new, r_new, w_ukv, l, nb, seq, past):
    lay = lambda rows, n: pl.BlockSpec((None, rows, n), lambda b: (l, b, 0))
    return pl.pallas_call(
        functools.partial(_mla_sample_kernel, seq=seq, past=past),
        grid=(nb,),
        in_specs=[pl.BlockSpec((MLA_HEADS, seq, MLA_QK), lambda b: (0, b, 0)),
                  lay(past, MLA_KV_RANK), lay(past, MLA_ROPE),
                  lay(seq, MLA_KV_RANK), lay(seq, MLA_ROPE), _layer_block(w_ukv, l)],
        out_specs=pl.BlockSpec((seq, MLA_HEADS * MLA_V), lambda b: (b, 0)),
        out_shape=jax.ShapeDtypeStruct((nb * seq, MLA_HEADS * MLA_V), BF16),
        scratch_shapes=[pltpu.VMEM((MLA_HEADS * seq, MLA_KV_RANK + 128), BF16),
                        pltpu.VMEM((past + seq, MLA_KV_RANK + 128), BF16)],
        compiler_params=_cparams(("arbitrary",)),
        name="mla_sample",
    )(qf, c_past, r_past, c_new, r_new, w_ukv)


def _diff_lambda(lam_ref, lam_init):
    v = lam_ref[...]
    a = jnp.sum(v[0:1] * v[1:2], axis=-1, keepdims=True)
    b = jnp.sum(v[2:3] * v[3:4], axis=-1, keepdims=True)
    return jnp.exp(a) - jnp.exp(b) + lam_init


def _split_q(q):
    lane = lax.broadcasted_iota(jnp.int32, q.shape, 1)
    qf = q.astype(F32)
    return (jnp.where(lane < DIFF_QK, qf, 0.0).astype(BF16),
            jnp.where(lane >= DIFF_QK, qf, 0.0).astype(BF16))


def _diff_finish(o0, o1, lam, g, lam_init):
    o = o0 - lam * o1
    return (_rms(o, g) * (1.0 - lam_init)).astype(BF16)


def _diff_prompt_kernel(q_ref, k_ref, v_ref, bias_ref, lam_ref, g_ref, o_ref, *, lam_init, nq):
    t = ATT_T
    seq = nq * t
    lam = _diff_lambda(lam_ref, lam_init)
    mask = _chunk_mask(t)

    def scores(item):
        tile, half = item
        n = (tile + 1) * t
        q = _split_q(q_ref[n - t:n, :])[half]
        return _causal_scores([q], k_ref[0:n, :], v_ref, n, t, mask,
                              bias=bias_ref[0, :, seq - n:seq])[0]

    first_half = {}

    def finish(item, blocks):
        tile, half = item
        o = _softmax_pv(blocks)
        if half == 0:
            first_half[tile] = o
        else:
            o_ref[tile * t:(tile + 1) * t, :] = _diff_finish(first_half.pop(tile), o, lam,
                                                             g_ref[...], lam_init)

    _pipelined([(tile, half) for tile in _TILE_ORDER(nq) for half in (0, 1)], scores, finish)


def _diff_prompt(dq, dk, dv, bias, wts, l, nb, seq, lam_init):
    t = ATT_T
    return pl.pallas_call(
        functools.partial(_diff_prompt_kernel, lam_init=lam_init, nq=seq // t),
        grid=(nb, DIFF_HEADS),
        in_specs=[pl.BlockSpec((seq, 128), lambda b, h: (b, h)),
                  pl.BlockSpec((seq, 128), lambda b, h: (b, h)),
                  pl.BlockSpec((seq, 128), lambda b, h: (b, h)),
                  pl.BlockSpec((1, t, seq), lambda b, h: (h, 0, 0)),
                  _layer_block(wts['lam'], l), _layer_block(wts['subln'], l)],
        out_specs=pl.BlockSpec((seq, DIFF_V), lambda b, h: (b, h)),
        out_shape=jax.ShapeDtypeStruct((nb * seq, DIFF_HEADS * DIFF_V), BF16),
        compiler_params=_cparams(("arbitrary",) * 2),
        name="diff_prompt",
    )(dq, dk, dv, bias, wts['lam'], wts['subln'])


def _diff_sample_kernel(q_ref, kp_ref, vp_ref, kn_ref, vn_ref, bias_ref, lam_ref, g_ref,
                        o_ref, *, lam_init, past):
    lam = _diff_lambda(lam_ref, lam_init)
    head_ops = {}

    def operands(h):
        if h not in head_ops:
            cols = slice(h * 128, (h + 1) * 128)
            head_ops[h] = (_split_q(q_ref[:, cols]),
                           _head_rows(kp_ref, h, past, DIFF_HEADS).astype(BF16),
                           _head_rows(vp_ref, h, past, DIFF_HEADS).astype(BF16),
                           kn_ref[:, cols], vn_ref[:, cols])
        return head_ops[h]

    def scores(item):
        h, half = item
        qs, kp, vp, kn, vn = operands(h)
        return [(_dot_nt(qs[half], kp) + bias_ref[h, :, 0:past], vp),
                (_dot_nt(qs[half], kn) + bias_ref[h, :, past:], vn)]

    first_half = {}

    def finish(item, blocks):
        h, half = item
        o = _softmax_pv(blocks)
        if half == 0:
            first_half[h] = o
        else:
            o_ref[:, h * 128:(h + 1) * 128] = _diff_finish(first_half.pop(h), o, lam,
                                                           g_ref[...], lam_init)

    _pipelined([(h, half) for h in range(DIFF_HEADS) for half in (0, 1)], scores, finish)


def _diff_sample(dq, dk_new, dv_new, k_past, v_past, bias, wts, l, nb, seq, past, lam_init):
    width = DIFF_HEADS * 128
    new = pl.BlockSpec((seq, width), lambda b: (b, 0))
    cache = pl.BlockSpec((None, None, past * DIFF_HEADS, 128), lambda b: (l, b, 0, 0))
    return pl.pallas_call(
        functools.partial(_diff_sample_kernel, lam_init=lam_init, past=past),
        grid=(nb,),
        in_specs=[new, cache, cache, new, new,
                  pl.BlockSpec(bias.shape, lambda b: (0, 0, 0)),
                  _layer_block(wts['lam'], l), _layer_block(wts['subln'], l)],
        out_specs=new,
        out_shape=jax.ShapeDtypeStruct((nb * seq, width), BF16),
        compiler_params=_cparams(("arbitrary",)),
        name="diff_sample",
    )(dq, k_past, v_past, dk_new, dv_new, bias, wts['lam'], wts['subln'])


def _band_prompt_kernel(q_ref, k_ref, v_ref, bias_ref, o_ref, *, nq):
    t = ATT_T
    full = BAND_ROWS + t

    def scores(tile):
        start = max(0, tile * t - BAND_ROWS)
        width = (tile + 1) * t - start
        s = (_dot_nt(q_ref[tile * t:(tile + 1) * t, :], k_ref[start:start + width, :])
             + bias_ref[0, :, full - width:full])
        return [(s, v_ref[start:start + width, :])]

    def finish(tile, blocks):
        o_ref[tile * t:(tile + 1) * t, :] = _softmax_pv(blocks).astype(BF16)

    _pipelined(_TILE_ORDER(nq), scores, finish)


def _band_prompt(bq, bk, bv, bias, nb, seq):
    t = ATT_T
    blk = pl.BlockSpec((seq, BAND_DIM), lambda b, h: (b, h))
    return pl.pallas_call(
        functools.partial(_band_prompt_kernel, nq=seq // t),
        grid=(nb, BAND_HEADS),
        in_specs=[blk, blk, blk,
                  pl.BlockSpec((1, t, BAND_ROWS + t), lambda b, h: (h, 0, 0))],
        out_specs=blk,
        out_shape=jax.ShapeDtypeStruct((nb * seq, BAND_HEADS * BAND_DIM), BF16),
        compiler_params=_cparams(("arbitrary",) * 2),
        name="band_prompt",
    )(bq, bk, bv, bias)


def _band_sample_kernel(q_ref, kp_ref, vp_ref, kn_ref, vn_ref, bias_ref, o_ref, *, past):
    def scores(h):
        cols = slice(h * 128, (h + 1) * 128)
        kp = _head_rows(kp_ref, h, past, BAND_HEADS).astype(BF16)
        vp = _head_rows(vp_ref, h, past, BAND_HEADS).astype(BF16)
        q = q_ref[:, cols]
        return [(_dot_nt(q, kp) + bias_ref[h, :, 0:past], vp),
                (_dot_nt(q, kn_ref[:, cols]) + bias_ref[h, :, past:], vn_ref[:, cols])]

    def finish(h, blocks):
        o_ref[:, h * 128:(h + 1) * 128] = _softmax_pv(blocks).astype(BF16)

    _pipelined(list(range(BAND_HEADS)), scores, finish)


def _band_sample(bq, bk_new, bv_new, k_past, v_past, bias, l, nb, seq, past):
    width = BAND_HEADS * BAND_DIM
    new = pl.BlockSpec((seq, width), lambda b: (b, 0))
    cache = pl.BlockSpec((None, None, past * BAND_HEADS, 128), lambda b: (l, b, 0, 0))
    return pl.pallas_call(
        functools.partial(_band_sample_kernel, past=past),
        grid=(nb,),
        in_specs=[new, cache, cache, new, new, pl.BlockSpec(bias.shape, lambda b: (0, 0, 0))],
        out_specs=new,
        out_shape=jax.ShapeDtypeStruct((nb * seq, width), BF16),
        compiler_params=_cparams(("arbitrary",)),
        name="band_sample",
    )(bq, k_past, v_past, bk_new, bv_new, bias)


def _oproj_kernel(a_ref, b_ref, c_ref, x_ref, wo_ref, g_ref, beta_ref, o_ref):
    def mix(r):
        rows = slice(r, r + OPROJ_SUB)
        return (_dot(a_ref[rows, :], wo_ref[0:_O_MIX_DIFF, :])
                + _dot(b_ref[rows, :], wo_ref[_O_MIX_DIFF:_O_MIX_BAND, :])
                + _dot(c_ref[rows, :], wo_ref[_O_MIX_BAND:, :]))

    def finish(r, m):
        rows = slice(r, r + OPROJ_SUB)
        o_ref[rows, :] = _layer_norm(DEEPNORM_ALPHA * x_ref[rows, :] + m,
                                     g_ref[...], beta_ref[...])

    _pipelined(list(range(0, x_ref.shape[0], OPROJ_SUB)), mix, finish)


def _oproj_ln(o_a, o_b, o_c, x2, wts, l, tm):
    m = x2.shape[0]
    row = lambda n: pl.BlockSpec((tm, n), lambda t: (t, 0))
    return pl.pallas_call(
        _oproj_kernel,
        grid=(m // tm,),
        in_specs=[row(o_a.shape[1]), row(o_b.shape[1]), row(o_c.shape[1]), row(D_MODEL),
                  _layer_block(wts['w_o'], l), _layer_block(wts['ln1_g'], l),
                  _layer_block(wts['ln1_b'], l)],
        out_specs=row(D_MODEL),
        out_shape=jax.ShapeDtypeStruct((m, D_MODEL), F32),
        compiler_params=_cparams(("arbitrary",)),
        name="oproj_ln",
    )(o_a, o_b, o_c, x2, wts['w_o'], wts['ln1_g'], wts['ln1_b'])


def _ffn_kernel(x_ref, prev_ref, wg_ref, wu_ref, wd_ref, cw_ref, cb_ref, g_ref, beta_ref,
                o_ref, st_ref, acc_ref, carry_ref, *, tf, seg, tiles_per_seq):
    i = pl.program_id(0)
    j = pl.program_id(1)
    nj = pl.num_programs(1)
    tm = x_ref.shape[0]
    col = pl.ds(pl.multiple_of(j * tf, tf), tf)

    xb = x_ref[...].astype(BF16)
    g = _dot(xb, wg_ref[...])
    u = _dot(xb, wu_ref[...])
    cw = cw_ref[...]
    cb = cb_ref[...]
    rows = lax.broadcasted_iota(jnp.int32, (seg, tf), 0)
    seq_start = (i % tiles_per_seq) == 0

    hs = []
    for s in range(tm // seg):
        gs = g[s * seg:(s + 1) * seg]
        state = prev_ref[s]
        if tiles_per_seq > 1:
            state = jnp.where(seq_start, state, carry_ref[:, col])
        pm2, pm1 = state[0:1], state[1:2]
        gm1 = jnp.where(rows == 0, pm1, pltpu.roll(gs, 1, 0))
        gm2 = jnp.where(rows == 0, pm2, jnp.where(rows == 1, pm1, pltpu.roll(gs, 2, 0)))
        gc = cb + cw[0:1] * gm2 + cw[1:2] * gm1 + cw[2:3] * gs
        hs.append(gc * (1.0 / (1.0 + jnp.exp(-gc))) * u[s * seg:(s + 1) * seg])
        last = gs[seg - 2:seg]
        st_ref[s, :, col] = last
        if tiles_per_seq > 1:
            carry_ref[:, col] = last
    h = hs[0] if len(hs) == 1 else jnp.concatenate(hs, axis=0)
    acc_ref[...] = jnp.where(j == 0, 0.0, acc_ref[...]) + _dot(h.astype(BF16), wd_ref[...])

    @pl.when(j == nj - 1)
    def _():
        o_ref[...] = _layer_norm(DEEPNORM_ALPHA * x_ref[...] + acc_ref[...],
                                 g_ref[...], beta_ref[...])


def _ffn(x2, prev, lp, wts, l, seq, tm, tf=512):
    m = x2.shape[0]
    nseq = m // seq
    seg = min(seq, tm)
    spt = tm // seg
    tps = seq // seg
    kern = functools.partial(_ffn_kernel, tf=tf, seg=seg, tiles_per_seq=tps)
    vec = lambda n: pl.BlockSpec((None, n, tf), lambda i, j: (l, 0, j))
    ln = lambda arr: pl.BlockSpec((None, 1, D_MODEL), lambda i, j: (l, 0, 0))
    return pl.pallas_call(
        kern,
        grid=(m // tm, D_FF // tf),
        in_specs=[pl.BlockSpec((tm, D_MODEL), lambda i, j: (i, 0)),
                  pl.BlockSpec((None, spt, CONV_W - 1, tf), lambda i, j: (lp, i // tps, 0, j)),
                  pl.BlockSpec((None, D_MODEL, tf), lambda i, j: (l, 0, j)),
                  pl.BlockSpec((None, D_MODEL, tf), lambda i, j: (l, 0, j)),
                  pl.BlockSpec((None, tf, D_MODEL), lambda i, j: (l, j, 0)),
                  vec(CONV_W), vec(1), ln(wts['ln2_g']), ln(wts['ln2_b'])],
        out_specs=[pl.BlockSpec((tm, D_MODEL), lambda i, j: (i, 0)),
                   pl.BlockSpec((spt, CONV_W - 1, D_FF), lambda i, j: (i // tps, 0, 0))],
        out_shape=[jax.ShapeDtypeStruct((m, D_MODEL), F32),
                   jax.ShapeDtypeStruct((nseq, CONV_W - 1, D_FF), F32)],
        scratch_shapes=[pltpu.VMEM((tm, D_MODEL), F32),
                        pltpu.VMEM((CONV_W - 1, D_FF), F32)],
        compiler_params=_cparams(("arbitrary", "arbitrary")),
        name="ffn",
    )(x2, prev, wts['w_gate'], wts['w_up'], wts['w_down'], wts['conv_w'], wts['conv_b'],
      wts['ln2_g'], wts['ln2_b'])


def _win_kernel(w_ref, o_ref):
    w = w_ref[...]
    half = MLA_ROPE // 2
    lo, hi = _O_CKV + MLA_KV_RANK, _O_CKV + MLA_KV_RANK + MLA_ROPE
    o_ref[:, 0:lo] = w[:, 0:lo].astype(BF16)
    o_ref[:, lo:_O_KR] = w[:, hi:].astype(BF16)
    zeros = jnp.zeros((w.shape[0], 128 - MLA_ROPE), BF16)
    o_ref[:, _O_KR:_O_KR + MLA_ROPE] = w[:, lo:hi].astype(BF16)
    o_ref[:, _O_KR + MLA_ROPE:_O_KRS] = zeros
    o_ref[:, _O_KRS:_O_KRS + half] = w[:, lo + half:hi].astype(BF16)
    o_ref[:, _O_KRS + half:_O_KRS + MLA_ROPE] = w[:, lo:lo + half].astype(BF16)
    o_ref[:, _O_KRS + MLA_ROPE:_IN_P] = zeros


def _win_relayout(w_in, tk=256):
    depth, k, n = w_in.shape
    return pl.pallas_call(
        _win_kernel,
        grid=(depth, k // tk),
        in_specs=[pl.BlockSpec((None, tk, n), lambda d, t: (d, t, 0))],
        out_specs=pl.BlockSpec((None, tk, _IN_P), lambda d, t: (d, t, 0)),
        out_shape=jax.ShapeDtypeStruct((depth, k, _IN_P), BF16),
        compiler_params=_cparams(("arbitrary", "arbitrary")),
        name="win_relayout",
    )(w_in)


def _prep_weights(w_in, mla_w_uq, mla_w_ukv):
    half = MLA_ROPE // 2
    swap = lambda w: jnp.concatenate([w[..., half:], w[..., :half]], axis=-1)
    pad = lambda w: jnp.concatenate([w, jnp.zeros_like(w)], axis=-1)
    w_in_p = _win_relayout(w_in)
    uq = mla_w_uq.reshape(DEPTH, MLA_Q_RANK, MLA_HEADS, MLA_NOPE + MLA_ROPE)
    nope, rope = uq[..., :MLA_NOPE], uq[..., MLA_NOPE:]
    flat = lambda w: w.reshape(DEPTH, w.shape[1], -1)
    w_uq_p = jnp.concatenate([flat(nope), flat(pad(rope)), flat(pad(swap(rope)))],
                             axis=-1).astype(BF16)
    ukv = mla_w_ukv.reshape(DEPTH, MLA_KV_RANK, MLA_HEADS, MLA_NOPE + MLA_V)
    w_ukv_p = jnp.concatenate([flat(ukv[..., :MLA_NOPE]), flat(ukv[..., MLA_NOPE:])],
                              axis=-1).astype(BF16)
    return w_in_p, w_uq_p, w_ukv_p


def _rope_tables(pos):
    half = MLA_ROPE // 2
    inv = ROPE_THETA ** (-jnp.arange(half, dtype=F32) / half)
    ang = pos.astype(F32)[:, None] * inv
    cos, sin = jnp.cos(ang), jnp.sin(ang)
    z = jnp.zeros((pos.shape[0], 128 - MLA_ROPE), F32)
    return jnp.concatenate([cos, cos, z], -1), jnp.concatenate([-sin, sin, z], -1)


def _t5_bucket(rel):
    half = T5_BUCKETS // 2
    exact = half // 2
    n = jnp.abs(rel)
    nf = jnp.maximum(n, 1).astype(F32)
    large = exact + (jnp.log(nf / exact) / math.log(T5_MAX_DIST / exact)
                     * (half - exact)).astype(jnp.int32)
    large = jnp.minimum(large, half - 1)
    return jnp.where(rel > 0, half, 0) + jnp.where(n < exact, n, large)


def _toeplitz(fn, rows, cols, shift):
    p = rows + cols
    k = jnp.arange(p, dtype=jnp.int32)
    u = fn(jnp.where(k < cols, k, k - p) + shift).astype(F32)
    flat = jnp.tile(u, (1, rows))[:, :rows * (p - 1)]
    return flat.reshape(u.shape[0], rows, p - 1)[:, :, :cols]


def _t5_bias(t5_table, rows, cols, shift):
    return _toeplitz(lambda rel: t5_table[_t5_bucket(rel)].T, rows, cols, shift)


def _band_bias(rel_table, t):
    def fn(d):
        rel = jnp.clip(BAND_ROWS - d, -BAND_REL_CLIP, BAND_REL_CLIP) + BAND_REL_CLIP
        return rel_table[:, rel]
    bias = _toeplitz(fn, t, BAND_ROWS + t, 0)
    qc = (jnp.arange(t, dtype=jnp.int32)[:, None] + BAND_ROWS) // CHUNK
    kc = jnp.arange(BAND_ROWS + t, dtype=jnp.int32)[None, :] // CHUNK
    ok = (kc <= qc) & (kc >= qc - BAND_PREV_CHUNKS)
    return jnp.where(ok[None], bias, NEG)


def _layer(x2, nb, seq, wts, l, tabs, caches, state_bufs, lam_init, tm_proj, tm_mid, tm_ffn):
    cos_t, sin_t = tabs['rope']
    state_bufs, (dq_b, dk_b, dv_b, bq_b, bk_b, bv_b, qf, kf, v_b) = _proj(
        x2, wts, l, cos_t, sin_t, tm_proj, seq, state_bufs)
    if caches is None:
        o_a = _mla_prompt(qf, kf, v_b, nb, seq)
        o_b = _diff_prompt(dq_b, dk_b, dv_b, tabs['t5'], wts, l, nb, seq, lam_init)
        o_c = _band_prompt(bq_b, bk_b, bv_b, tabs['band'], nb, seq)
        prev, lp = jnp.zeros((1, nb, CONV_W - 1, D_FF), F32), 0
    else:
        c_ckv, c_krope, c_dk, c_dv, c_bk, c_bv, prev = caches
        plen = c_ckv.shape[1] // nb
        o_a = _mla_sample(qf, c_ckv, c_krope, state_bufs[0], state_bufs[1], wts['w_ukv'], l,
                          nb, seq, plen)
        o_b = _diff_sample(dq_b, dk_b, dv_b, c_dk, c_dv, tabs['t5'], wts, l, nb, seq, plen,
                           lam_init)
        o_c = _band_sample(bq_b, bk_b, bv_b, c_bk, c_bv, tabs['band'], l, nb, seq,
                           c_bk.shape[2] // BAND_HEADS)
        lp = l
    x1 = _oproj_ln(o_a, o_b, o_c, x2, wts, l, tm_mid)
    x_out, conv_state = _ffn(x1, prev, lp, wts, l, seq, tm_ffn)
    return x_out, state_bufs, conv_state


def _final_states(bufs, conv, nb, seq):
    ckv, krope, dk, dv, bk, bv = bufs
    keep = min(BAND_ROWS, seq)
    return (ckv.reshape(DEPTH, nb, seq, MLA_KV_RANK), krope.reshape(DEPTH, nb, seq, MLA_ROPE),
            dk.reshape(DEPTH, nb, seq, DIFF_HEADS, 2 * DIFF_QK),
            dv.reshape(DEPTH, nb, seq, DIFF_HEADS, DIFF_V),
            bk.reshape(DEPTH, nb, keep, BAND_HEADS, BAND_DIM),
            bv.reshape(DEPTH, nb, keep, BAND_HEADS, BAND_DIM), jnp.stack(conv))


def kernel(x_prompt, x_sample, cache_mla_ckv, cache_mla_krope, cache_diff_k, cache_diff_v, cache_band_k, cache_band_v, state_ffn_conv, t5_table, w_in, mla_q_norm, mla_w_uq, mla_kv_norm, mla_w_ukv, diff_lq1, diff_lk1, diff_lq2, diff_lk2, diff_subln, band_rel_table, w_o, ln1_g, ln1_b, ffn_w_gate, ffn_w_up, ffn_conv_w, ffn_conv_b, ffn_w_down, ln2_g, ln2_b):
    nb_p, seq_p, _ = x_prompt.shape
    nb_s, seq_s, _ = x_sample.shape
    past_len = cache_mla_ckv.shape[2]
    band_len = cache_band_k.shape[2]
    assert seq_p % ATT_T == 0 and seq_s == CHUNK and past_len % CHUNK == 0
    assert band_len == BAND_ROWS

    w_in_p, w_uq_p, w_ukv_p = _prep_weights(w_in, mla_w_uq, mla_w_ukv)
    vec = lambda a: a.reshape(DEPTH, 1, -1)
    wts = {'w_in': w_in_p, 'w_uq': w_uq_p, 'w_ukv': w_ukv_p,
           'q_norm': vec(mla_q_norm), 'kv_norm': vec(mla_kv_norm),
           'lam': jnp.stack([diff_lq1, diff_lk1, diff_lq2, diff_lk2], axis=1),
           'subln': vec(diff_subln), 'w_o': w_o.astype(BF16),
           'ln1_g': vec(ln1_g), 'ln1_b': vec(ln1_b),
           'w_gate': ffn_w_gate.astype(BF16), 'w_up': ffn_w_up.astype(BF16),
           'w_down': ffn_w_down.astype(BF16), 'conv_w': ffn_conv_w, 'conv_b': vec(ffn_conv_b),
           'ln2_g': vec(ln2_g), 'ln2_b': vec(ln2_b)}
    caches = (cache_mla_ckv.reshape(DEPTH, nb_s * past_len, MLA_KV_RANK),
              cache_mla_krope.reshape(DEPTH, nb_s * past_len, MLA_ROPE),
              cache_diff_k.reshape(DEPTH, nb_s, past_len * DIFF_HEADS, 128),
              cache_diff_v.reshape(DEPTH, nb_s, past_len * DIFF_HEADS, 128),
              cache_band_k.reshape(DEPTH, nb_s, band_len * BAND_HEADS, 128),
              cache_band_v.reshape(DEPTH, nb_s, band_len * BAND_HEADS, 128),
              state_ffn_conv)

    t = ATT_T
    pos_p = jnp.arange(seq_p, dtype=jnp.int32)
    pos_s = past_len + jnp.arange(seq_s, dtype=jnp.int32)
    tm_s = nb_s * seq_s
    rope_s = tuple(jnp.tile(tb, (nb_s, 1)) for tb in _rope_tables(pos_s))
    tabs_p = {'rope': _rope_tables(pos_p),
              't5': _t5_bias(t5_table, ATT_T, seq_p, ATT_T - seq_p)}
    tabs_s = {'rope': rope_s, 't5': _t5_bias(t5_table, seq_s, past_len + seq_s, -past_len)}

    y_p = x_prompt.reshape(nb_p * seq_p, D_MODEL)
    y_s = x_sample.reshape(tm_s, D_MODEL)
    bufs_p = bufs_s = None
    conv_p, conv_s = [], []
    for l in range(DEPTH):
        lam_init = 0.8 - 0.6 * math.exp(-0.3 * l)
        band_p = _band_bias(band_rel_table[l], t)
        band_s = _band_bias(band_rel_table[l], seq_s)
        y_p, bufs_p, cs = _layer(y_p, nb_p, seq_p, wts, l, dict(tabs_p, band=band_p), None,
                                 bufs_p, lam_init, tm_proj=256, tm_mid=512, tm_ffn=512)
        conv_p.append(cs)
        y_s, bufs_s, cs = _layer(y_s, nb_s, seq_s, wts, l, dict(tabs_s, band=band_s), caches,
                                 bufs_s, lam_init, tm_proj=256, tm_mid=512, tm_ffn=tm_s)
        conv_s.append(cs)
    return (y_p.reshape(nb_p, seq_p, D_MODEL), y_s.reshape(nb_s, seq_s, D_MODEL),
            *_final_states(bufs_p, conv_p, nb_p, seq_p),
            *_final_states(bufs_s, conv_s, nb_s, seq_s))
```

```python
import functools
import math

import jax
import jax.numpy as jnp
import numpy as np
from jax import lax
from jax.experimental import pallas as pl
from jax.experimental.pallas import tpu as pltpu

F32 = jnp.float32
BF16 = jnp.bfloat16

D_MODEL = 2048
DEPTH = 2
CHUNK = 64
MLA_HEADS = 6
MLA_Q_RANK = 512
MLA_KV_RANK = 256
MLA_NOPE = 128
MLA_ROPE = 64
MLA_V = 128
ROPE_THETA = 10000.0
DIFF_HEADS = 4
DIFF_QK = 64
DIFF_V = 128
BAND_HEADS = 6
BAND_DIM = 128
BAND_PREV_CHUNKS = 8
BAND_REL_CLIP = 256
T5_BUCKETS = 32
T5_MAX_DIST = 128
D_FF = 5632
CONV_W = 3
LN_EPS = 1e-5
RMS_EPS = 1e-6
DEEPNORM_ALPHA = (2 * DEPTH) ** 0.25

NEG = -1e30
MLA_QK = 256
MLA_SCALE = (MLA_NOPE + MLA_ROPE) ** -0.5
DIFF_SCALE = DIFF_QK ** -0.5
BAND_SCALE = BAND_DIM ** -0.5
BAND_ROWS = BAND_PREV_CHUNKS * CHUNK
ATT_T = 256
OPROJ_SUB = 128
VMEM_LIMIT = 56 * 1024 * 1024

_O_CQ, _O_CKV, _O_DQ, _O_DK, _O_DV = 0, 512, 768, 1280, 1792
_O_BQ, _O_BK, _O_BV, _O_KR, _O_KRS, _IN_P = 2304, 3072, 3840, 4608, 4736, 4864
_O_MIX_DIFF = MLA_HEADS * MLA_V
_O_MIX_BAND = _O_MIX_DIFF + DIFF_HEADS * DIFF_V
_N_STATE = 6


def _cparams(sem):
    return pltpu.CompilerParams(dimension_semantics=sem, vmem_limit_bytes=VMEM_LIMIT)


def _dot(a, b):
    return jnp.dot(a, b, preferred_element_type=F32)


def _dot_nt(a, b):
    return lax.dot_general(a, b, (((1,), (1,)), ((), ())), preferred_element_type=F32)


def _rms(x, g):
    return x * lax.rsqrt(jnp.mean(x * x, axis=-1, keepdims=True) + RMS_EPS) * g


def _layer_norm(x, g, b):
    mu = jnp.mean(x, axis=-1, keepdims=True)
    xc = x - mu
    var = jnp.mean(xc * xc, axis=-1, keepdims=True)
    return xc * lax.rsqrt(var + LN_EPS) * g + b


def _layer_block(arr, l):
    zeros = (0,) * (arr.ndim - 1)
    return pl.BlockSpec((None,) + arr.shape[1:], lambda *_: (l,) + zeros,
                        pipeline_mode=pl.Buffered(1))


def _heads_to_rows(o_ref, val, heads):
    rows = val.shape[0]
    for h in range(heads):
        o_ref[pl.ds(h, rows, stride=heads), :] = val[:, h * 128:(h + 1) * 128]


def _head_rows(ref, h, rows, heads):
    return ref[pl.ds(h, rows, stride=heads), :]


def _proj_kernel(*refs, layer, n_alias, streams, tiles_per_seq, band_from):
    (x_ref, win_ref, wuq_ref, wukv_ref, qn_ref, kvn_ref, cos_ref, sin_ref) = refs[:8]
    (ckv_o, krope_o, dk_o, dv_o, bk_o, bv_o,
     dq_b, dk_b, dv_b, bq_b, bk_b, bv_b, qf_b, kf_b, v_b) = refs[8 + n_alias:]
    rows = x_ref.shape[0] // streams

    def put(ref, write):
        if n_alias:
            write(ref)
            return
        for d in range(DEPTH):
            if d == layer:
                write(ref.at[d])
            else:
                ref[d] = jnp.zeros(ref.shape[1:], F32)

    def store(val):
        def write(r):
            r[...] = val
        return write

    def store_heads(val, heads):
        return lambda r: _heads_to_rows(r, val, heads)

    def store_rope_t(val):
        vt = val.T[0:MLA_ROPE]
        def write(r):
            for s in range(streams):
                r[s] = vt[:, s * rows:(s + 1) * rows]
        return write

    def store_head_major(val, heads):
        def write(r):
            for s in range(streams):
                for h in range(heads):
                    r[s, h] = val[s * rows:(s + 1) * rows, h * 128:(h + 1) * 128]
        return write

    xb = x_ref[...].astype(BF16)
    cos = cos_ref[...]
    sin = sin_ref[...]

    def seg(off, n):
        return _dot(xb, win_ref[:, off:off + n])

    cq = _rms(seg(_O_CQ, MLA_Q_RANK), qn_ref[...]).astype(BF16)
    ckv = _rms(seg(_O_CKV, MLA_KV_RANK), kvn_ref[...])
    put(ckv_o, store(ckv))
    ckvb = ckv.astype(BF16)
    krs = seg(_O_KR, 256)
    kr = krs[:, 0:128] * cos + krs[:, 128:256] * sin
    put(krope_o, store_rope_t(kr))
    krb = kr.astype(BF16)

    dq_b[...] = (seg(_O_DQ, 512) * DIFF_SCALE).astype(BF16)
    dk = seg(_O_DK, 512)
    put(dk_o, store_heads(dk, DIFF_HEADS))
    dk_b[...] = dk.astype(BF16)
    dv = seg(_O_DV, 512)
    put(dv_o, store_heads(dv, DIFF_HEADS))
    dv_b[...] = dv.astype(BF16)

    nh = MLA_HEADS * 128
    q = _dot(cq, wuq_ref[...])
    for h in range(MLA_HEADS):
        lo = h * 128
        qr = q[:, nh + lo:nh + lo + 128] * cos + q[:, 2 * nh + lo:2 * nh + lo + 128] * sin
        qf_b[h, :, 0:128] = (q[:, lo:lo + 128] * MLA_SCALE).astype(BF16)
        qf_b[h, :, 128:256] = (qr * MLA_SCALE).astype(BF16)

    kv = _dot(ckvb, wukv_ref[...])
    for h in range(MLA_HEADS):
        lo = h * 128
        kf_b[h, :, 0:128] = kv[:, lo:lo + 128].astype(BF16)
        kf_b[h, :, 128:256] = krb
    v_b[...] = kv[:, nh:2 * nh].astype(BF16)

    bq_b[...] = (seg(_O_BQ, 768) * BAND_SCALE).astype(BF16)
    bk = seg(_O_BK, 768)
    bk_b[...] = bk.astype(BF16)
    bv = seg(_O_BV, 768)
    bv_b[...] = bv.astype(BF16)

    def band_state():
        put(bk_o, store_head_major(bk, BAND_HEADS))
        put(bv_o, store_head_major(bv, BAND_HEADS))

    if band_from == 0:
        band_state()
    else:
        pl.when(pl.program_id(0) % tiles_per_seq >= band_from)(band_state)


def _proj(x2, wts, l, cos_t, sin_t, tm, seq, state_bufs):
    m = x2.shape[0]
    nb = m // seq
    period = cos_t.shape[0] // tm
    keep = min(BAND_ROWS, seq)
    n_alias = 0 if state_bufs is None else _N_STATE
    lead, li = (DEPTH, 0) if state_bufs is None else (None, l)
    if seq > tm:
        tps, kt, spt = seq // tm, keep // tm, 1
        band_from = tps - kt
        band_tile = lambda t: jnp.maximum(t % tps - band_from, 0)
    else:
        assert keep == seq
        tps, band_from, spt = 1, 0, tm // seq
        band_tile = lambda t: 0
    rows = tm // spt
    row = lambda n: pl.BlockSpec((tm, n), lambda t: (t, 0))
    srow = lambda n, k=1: pl.BlockSpec((lead, tm * k, n), lambda t: (li, t, 0))
    rope_t = pl.BlockSpec((lead, spt, MLA_ROPE, rows), lambda t: (li, t // tps, 0, t % tps))
    band = pl.BlockSpec((lead, spt, BAND_HEADS, rows, 128),
                        lambda t: (li, t // tps, 0, band_tile(t), 0))
    tab = pl.BlockSpec((tm, 128), lambda t: (t % period, 0))
    head = pl.BlockSpec((MLA_HEADS, tm, MLA_QK), lambda t: (0, t, 0))
    st = lambda *shape: jax.ShapeDtypeStruct((DEPTH,) + shape, F32)
    b16 = lambda n: jax.ShapeDtypeStruct((m, n), BF16)
    hb16 = jax.ShapeDtypeStruct((MLA_HEADS, m, MLA_QK), BF16)
    kern = functools.partial(_proj_kernel, layer=l, n_alias=n_alias, streams=spt,
                             tiles_per_seq=tps, band_from=band_from)
    ins = [x2, wts['w_in'], wts['w_uq'], wts['w_ukv'], wts['q_norm'], wts['kv_norm'],
           cos_t, sin_t]
    in_specs = [row(D_MODEL), _layer_block(wts['w_in'], l), _layer_block(wts['w_uq'], l),
                _layer_block(wts['w_ukv'], l), _layer_block(wts['q_norm'], l),
                _layer_block(wts['kv_norm'], l), tab, tab]
    if n_alias:
        ins += list(state_bufs)
        in_specs += [pl.BlockSpec(memory_space=pl.ANY)] * n_alias
    outs = pl.pallas_call(
        kern,
        grid=(m // tm,),
        in_specs=in_specs,
        out_specs=[srow(256), rope_t, srow(128, DIFF_HEADS), srow(128, DIFF_HEADS), band, band,
                   row(512), row(512), row(512), row(768), row(768), row(768),
                   head, head, row(768)],
        out_shape=[st(m, 256), st(nb, MLA_ROPE, seq), st(m * DIFF_HEADS, 128),
                   st(m * DIFF_HEADS, 128), st(nb, BAND_HEADS, keep, 128),
                   st(nb, BAND_HEADS, keep, 128),
                   b16(512), b16(512), b16(512), b16(768), b16(768), b16(768),
                   hb16, hb16, b16(768)],
        input_output_aliases={8 + k: k for k in range(n_alias)},
        compiler_params=_cparams(("arbitrary",)),
        name="proj",
    )(*ins)
    return outs[:_N_STATE], outs[_N_STATE:]


def _chunk_mask(t):
    r = lax.broadcasted_iota(jnp.int32, (t, t), 0) // CHUNK
    c = lax.broadcasted_iota(jnp.int32, (t, t), 1) // CHUNK
    return r >= c


def _softmax_pv(blocks):
    m = None
    for s, _ in blocks:
        bm = jnp.max(s, axis=-1, keepdims=True)
        m = bm if m is None else jnp.maximum(m, bm)
    l = None
    acc = None
    for s, vb in blocks:
        p = jnp.exp(s - m)
        bl = jnp.sum(p, axis=-1, keepdims=True)
        ba = _dot(p.astype(BF16), vb)
        l = bl if l is None else l + bl
        acc = ba if acc is None else acc + ba
    return acc / l


def _causal_scores(qs, k, v_ref, n, t, mask, bias=None):
    out = []
    for q in qs:
        s = _dot_nt(q, k)
        if bias is not None:
            s = s + bias
        blocks = [(jnp.where(mask, s[:, n - t:], NEG), v_ref[n - t:n, :])]
        if n > t:
            blocks.append((s[:, :n - t], v_ref[0:n - t, :]))
        out.append(blocks)
    return out


def _TILE_ORDER(nq):
    return list(range(nq))[::-1]


def _pipelined(order, scores, finish):
    nxt = scores(order[0])
    for pos, i in enumerate(order):
        cur = nxt
        if pos + 1 < len(order):
            nxt = scores(order[pos + 1])
        finish(i, cur)


def _mla_prompt_kernel(q_ref, k_ref, v_ref, o_ref, *, nq):
    t = ATT_T
    mask = _chunk_mask(t)

    def scores(tile):
        n = (tile + 1) * t
        return _causal_scores([q_ref[0, n - t:n, :]], k_ref[0, 0:n, :], v_ref, n, t, mask)

    def finish(tile, blocks):
        o_ref[tile * t:(tile + 1) * t, :] = _softmax_pv(blocks[0]).astype(BF16)

    _pipelined(_TILE_ORDER(nq), scores, finish)


def _mla_prompt(qf, kf, v, nb, seq):
    return pl.pallas_call(
        functools.partial(_mla_prompt_kernel, nq=seq // ATT_T),
        grid=(nb, MLA_HEADS),
        in_specs=[pl.BlockSpec((1, seq, MLA_QK), lambda b, h: (h, b, 0)),
                  pl.BlockSpec((1, seq, MLA_QK), lambda b, h: (h, b, 0)),
                  pl.BlockSpec((seq, MLA_V), lambda b, h: (b, h))],
        out_specs=pl.BlockSpec((seq, MLA_V), lambda b, h: (b, h)),
        out_shape=jax.ShapeDtypeStruct((nb * seq, MLA_HEADS * MLA_V), BF16),
        compiler_params=_cparams(("arbitrary",) * 2),
        name="mla_prompt",
    )(qf, kf, v)


def _mla_sample_kernel(q_ref, cp_ref, rp_ref, cn_ref, rn_ref, wukv_ref, o_ref, q_scr, *, seq):
    nh = MLA_HEADS * 128
    lat = MLA_KV_RANK
    for h in range(MLA_HEADS):
        rows = slice(h * seq, (h + 1) * seq)
        w_k = wukv_ref[:, h * 128:(h + 1) * 128]
        q_scr[rows, 0:lat] = _dot_nt(q_ref[h, :, 0:MLA_NOPE], w_k).astype(BF16)
        q_scr[rows, lat:lat + 128] = q_ref[h, :, MLA_NOPE:MLA_QK]
    q_lat = q_scr[:, 0:lat]
    q_rope = q_scr[:, lat:lat + MLA_ROPE]
    blocks = []
    for c_ref, r_ref in ((cp_ref, rp_ref), (cn_ref, rn_ref)):
        c = c_ref[...].astype(BF16)
        r_t = r_ref[...].astype(BF16)
        blocks.append((_dot_nt(q_lat, c) + _dot(q_rope, r_t), c))
    o_lat = _softmax_pv(blocks).astype(BF16)
    for h in range(MLA_HEADS):
        w_v = wukv_ref[:, nh + h * 128:nh + (h + 1) * 128]
        o_ref[:, h * 128:(h + 1) * 128] = _dot(o_lat[h * seq:(h + 1) * seq], w_v).astype(BF16)


def _mla_sample(qf, c_past, r_past, c_new, r_new, w_ukv, l, nb, seq, past):
    lat = lambda rows: pl.BlockSpec((None, rows, MLA_KV_RANK), lambda b: (l, b, 0))
    rope = lambda frames: pl.BlockSpec((None, None, MLA_ROPE, frames), lambda b: (l, b, 0, 0))
    return pl.pallas_call(
        functools.partial(_mla_sample_kernel, seq=seq),
        grid=(nb,),
        in_specs=[pl.BlockSpec((MLA_HEADS, seq, MLA_QK), lambda b: (0, b, 0)),
                  lat(past), rope(past), lat(seq), rope(seq), _layer_block(w_ukv, l)],
        out_specs=pl.BlockSpec((seq, MLA_HEADS * MLA_V), lambda b: (b, 0)),
        out_shape=jax.ShapeDtypeStruct((nb * seq, MLA_HEADS * MLA_V), BF16),
        scratch_shapes=[pltpu.VMEM((MLA_HEADS * seq, MLA_KV_RANK + 128), BF16)],
        compiler_params=_cparams(("arbitrary",)),
        name="mla_sample",
    )(qf, c_past, r_past, c_new, r_new, w_ukv)


def _diff_lambda(lam_ref, lam_init):
    v = lam_ref[...]
    a = jnp.sum(v[0:1] * v[1:2], axis=-1, keepdims=True)
    b = jnp.sum(v[2:3] * v[3:4], axis=-1, keepdims=True)
    return jnp.exp(a) - jnp.exp(b) + lam_init


def _split_q(q):
    lane = lax.broadcasted_iota(jnp.int32, q.shape, 1)
    qf = q.astype(F32)
    return (jnp.where(lane < DIFF_QK, qf, 0.0).astype(BF16),
            jnp.where(lane >= DIFF_QK, qf, 0.0).astype(BF16))


def _diff_finish(o0, o1, lam, g, lam_init):
    o = o0 - lam * o1
    return (_rms(o, g) * (1.0 - lam_init)).astype(BF16)


def _diff_prompt_kernel(q_ref, k_ref, v_ref, bias_ref, lam_ref, g_ref, o_ref, *, lam_init, nq):
    t = ATT_T
    seq = nq * t
    lam = _diff_lambda(lam_ref, lam_init)
    mask = _chunk_mask(t)

    def scores(item):
        tile, half = item
        n = (tile + 1) * t
        q = _split_q(q_ref[n - t:n, :])[half]
        return _causal_scores([q], k_ref[0:n, :], v_ref, n, t, mask,
                              bias=bias_ref[0, :, seq - n:seq])[0]

    first_half = {}

    def finish(item, blocks):
        tile, half = item
        o = _softmax_pv(blocks)
        if half == 0:
            first_half[tile] = o
        else:
            o_ref[tile * t:(tile + 1) * t, :] = _diff_finish(first_half.pop(tile), o, lam,
                                                             g_ref[...], lam_init)

    _pipelined([(tile, half) for tile in _TILE_ORDER(nq) for half in (0, 1)], scores, finish)


def _diff_prompt(dq, dk, dv, bias, wts, l, nb, seq, lam_init):
    t = ATT_T
    return pl.pallas_call(
        functools.partial(_diff_prompt_kernel, lam_init=lam_init, nq=seq // t),
        grid=(nb, DIFF_HEADS),
        in_specs=[pl.BlockSpec((seq, 128), lambda b, h: (b, h)),
                  pl.BlockSpec((seq, 128), lambda b, h: (b, h)),
                  pl.BlockSpec((seq, 128), lambda b, h: (b, h)),
                  pl.BlockSpec((1, t, seq), lambda b, h: (h, 0, 0)),
                  _layer_block(wts['lam'], l), _layer_block(wts['subln'], l)],
        out_specs=pl.BlockSpec((seq, DIFF_V), lambda b, h: (b, h)),
        out_shape=jax.ShapeDtypeStruct((nb * seq, DIFF_HEADS * DIFF_V), BF16),
        compiler_params=_cparams(("arbitrary",) * 2),
        name="diff_prompt",
    )(dq, dk, dv, bias, wts['lam'], wts['subln'])


def _diff_sample_kernel(q_ref, kp_ref, vp_ref, kn_ref, vn_ref, bias_ref, lam_ref, g_ref,
                        o_ref, *, lam_init, past):
    lam = _diff_lambda(lam_ref, lam_init)
    head_ops = {}

    def operands(h):
        if h not in head_ops:
            cols = slice(h * 128, (h + 1) * 128)
            head_ops[h] = (_split_q(q_ref[:, cols]),
                           _head_rows(kp_ref, h, past, DIFF_HEADS).astype(BF16),
                           _head_rows(vp_ref, h, past, DIFF_HEADS).astype(BF16),
                           kn_ref[:, cols], vn_ref[:, cols])
        return head_ops[h]

    def scores(item):
        h, half = item
        qs, kp, vp, kn, vn = operands(h)
        return [(_dot_nt(qs[half], kp) + bias_ref[h, :, 0:past], vp),
                (_dot_nt(qs[half], kn) + bias_ref[h, :, past:], vn)]

    first_half = {}

    def finish(item, blocks):
        h, half = item
        o = _softmax_pv(blocks)
        if half == 0:
            first_half[h] = o
        else:
            o_ref[:, h * 128:(h + 1) * 128] = _diff_finish(first_half.pop(h), o, lam,
                                                           g_ref[...], lam_init)

    _pipelined([(h, half) for h in range(DIFF_HEADS) for half in (0, 1)], scores, finish)


def _diff_sample(dq, dk_new, dv_new, k_past, v_past, bias, wts, l, nb, seq, past, lam_init):
    width = DIFF_HEADS * 128
    new = pl.BlockSpec((seq, width), lambda b: (b, 0))
    cache = pl.BlockSpec((None, None, past * DIFF_HEADS, 128), lambda b: (l, b, 0, 0))
    return pl.pallas_call(
        functools.partial(_diff_sample_kernel, lam_init=lam_init, past=past),
        grid=(nb,),
        in_specs=[new, cache, cache, new, new,
                  pl.BlockSpec(bias.shape, lambda b: (0, 0, 0)),
                  _layer_block(wts['lam'], l), _layer_block(wts['subln'], l)],
        out_specs=new,
        out_shape=jax.ShapeDtypeStruct((nb * seq, width), BF16),
        compiler_params=_cparams(("arbitrary",)),
        name="diff_sample",
    )(dq, k_past, v_past, dk_new, dv_new, bias, wts['lam'], wts['subln'])


def _band_prompt_kernel(q_ref, k_ref, v_ref, bias_ref, o_ref, *, nq):
    t = ATT_T
    full = BAND_ROWS + t

    def scores(tile):
        start = max(0, tile * t - BAND_ROWS)
        width = (tile + 1) * t - start
        s = (_dot_nt(q_ref[tile * t:(tile + 1) * t, :], k_ref[start:start + width, :])
             + bias_ref[0, :, full - width:full])
        return [(s, v_ref[start:start + width, :])]

    def finish(tile, blocks):
        o_ref[tile * t:(tile + 1) * t, :] = _softmax_pv(blocks).astype(BF16)

    _pipelined(_TILE_ORDER(nq), scores, finish)


def _band_prompt(bq, bk, bv, bias, nb, seq):
    t = ATT_T
    blk = pl.BlockSpec((seq, BAND_DIM), lambda b, h: (b, h))
    return pl.pallas_call(
        functools.partial(_band_prompt_kernel, nq=seq // t),
        grid=(nb, BAND_HEADS),
        in_specs=[blk, blk, blk,
                  pl.BlockSpec((1, t, BAND_ROWS + t), lambda b, h: (h, 0, 0))],
        out_specs=blk,
        out_shape=jax.ShapeDtypeStruct((nb * seq, BAND_HEADS * BAND_DIM), BF16),
        compiler_params=_cparams(("arbitrary",) * 2),
        name="band_prompt",
    )(bq, bk, bv, bias)


def _band_sample_kernel(q_ref, kp_ref, vp_ref, kn_ref, vn_ref, bias_ref, o_ref, *, past):
    def scores(h):
        cols = slice(h * 128, (h + 1) * 128)
        kp = kp_ref[h].astype(BF16)
        vp = vp_ref[h].astype(BF16)
        q = q_ref[:, cols]
        return [(_dot_nt(q, kp) + bias_ref[h, :, 0:past], vp),
                (_dot_nt(q, kn_ref[:, cols]) + bias_ref[h, :, past:], vn_ref[:, cols])]

    def finish(h, blocks):
        o_ref[:, h * 128:(h + 1) * 128] = _softmax_pv(blocks).astype(BF16)

    _pipelined(list(range(BAND_HEADS)), scores, finish)


def _band_sample(bq, bk_new, bv_new, k_past, v_past, bias, l, nb, seq, past):
    width = BAND_HEADS * BAND_DIM
    new = pl.BlockSpec((seq, width), lambda b: (b, 0))
    cache = pl.BlockSpec((None, None, BAND_HEADS, past, 128), lambda b: (l, b, 0, 0, 0))
    return pl.pallas_call(
        functools.partial(_band_sample_kernel, past=past),
        grid=(nb,),
        in_specs=[new, cache, cache, new, new, pl.BlockSpec(bias.shape, lambda b: (0, 0, 0))],
        out_specs=new,
        out_shape=jax.ShapeDtypeStruct((nb * seq, width), BF16),
        compiler_params=_cparams(("arbitrary",)),
        name="band_sample",
    )(bq, k_past, v_past, bk_new, bv_new, bias)


def _oproj_kernel(a_ref, b_ref, c_ref, x_ref, wo_ref, g_ref, beta_ref, o_ref):
    def mix(r):
        rows = slice(r, r + OPROJ_SUB)
        return (_dot(a_ref[rows, :], wo_ref[0:_O_MIX_DIFF, :])
                + _dot(b_ref[rows, :], wo_ref[_O_MIX_DIFF:_O_MIX_BAND, :])
                + _dot(c_ref[rows, :], wo_ref[_O_MIX_BAND:, :]))

    def finish(r, m):
        rows = slice(r, r + OPROJ_SUB)
        o_ref[rows, :] = _layer_norm(DEEPNORM_ALPHA * x_ref[rows, :] + m,
                                     g_ref[...], beta_ref[...])

    _pipelined(list(range(0, x_ref.shape[0], OPROJ_SUB)), mix, finish)


def _oproj_ln(o_a, o_b, o_c, x2, wts, l, tm):
    m = x2.shape[0]
    row = lambda n: pl.BlockSpec((tm, n), lambda t: (t, 0))
    return pl.pallas_call(
        _oproj_kernel,
        grid=(m // tm,),
        in_specs=[row(o_a.shape[1]), row(o_b.shape[1]), row(o_c.shape[1]), row(D_MODEL),
                  _layer_block(wts['w_o'], l), _layer_block(wts['ln1_g'], l),
                  _layer_block(wts['ln1_b'], l)],
        out_specs=row(D_MODEL),
        out_shape=jax.ShapeDtypeStruct((m, D_MODEL), F32),
        compiler_params=_cparams(("arbitrary",)),
        name="oproj_ln",
    )(o_a, o_b, o_c, x2, wts['w_o'], wts['ln1_g'], wts['ln1_b'])


def _ffn_kernel(x_ref, prev_ref, wg_ref, wu_ref, wd_ref, cw_ref, cb_ref, g_ref, beta_ref,
                o_ref, st_ref, acc_ref, carry_ref, *, tf, seg, tiles_per_seq):
    i = pl.program_id(0)
    j = pl.program_id(1)
    nj = pl.num_programs(1)
    tm = x_ref.shape[0]
    col = pl.ds(pl.multiple_of(j * tf, tf), tf)

    xb = x_ref[...].astype(BF16)
    g = _dot(xb, wg_ref[...])
    u = _dot(xb, wu_ref[...])
    cw = cw_ref[...]
    cb = cb_ref[...]
    rows = lax.broadcasted_iota(jnp.int32, (seg, tf), 0)
    seq_start = (i % tiles_per_seq) == 0

    hs = []
    for s in range(tm // seg):
        gs = g[s * seg:(s + 1) * seg]
        state = prev_ref[s]
        if tiles_per_seq > 1:
            state = jnp.where(seq_start, state, carry_ref[:, col])
        pm2, pm1 = state[0:1], state[1:2]
        gm1 = jnp.where(rows == 0, pm1, pltpu.roll(gs, 1, 0))
        gm2 = jnp.where(rows == 0, pm2, jnp.where(rows == 1, pm1, pltpu.roll(gs, 2, 0)))
        gc = cb + cw[0:1] * gm2 + cw[1:2] * gm1 + cw[2:3] * gs
        hs.append(gc * (1.0 / (1.0 + jnp.exp(-gc))) * u[s * seg:(s + 1) * seg])
        last = gs[seg - 2:seg]
        st_ref[s, :, col] = last
        if tiles_per_seq > 1:
            carry_ref[:, col] = last
    h = hs[0] if len(hs) == 1 else jnp.concatenate(hs, axis=0)
    acc_ref[...] = jnp.where(j == 0, 0.0, acc_ref[...]) + _dot(h.astype(BF16), wd_ref[...])

    @pl.when(j == nj - 1)
    def _():
        o_ref[...] = _layer_norm(DEEPNORM_ALPHA * x_ref[...] + acc_ref[...],
                                 g_ref[...], beta_ref[...])


def _ffn(x2, prev, lp, wts, l, seq, tm, tf=512):
    m = x2.shape[0]
    nseq = m // seq
    seg = min(seq, tm)
    spt = tm // seg
    tps = seq // seg
    kern = functools.partial(_ffn_kernel, tf=tf, seg=seg, tiles_per_seq=tps)
    vec = lambda n: pl.BlockSpec((None, n, tf), lambda i, j: (l, 0, j))
    ln = lambda arr: pl.BlockSpec((None, 1, D_MODEL), lambda i, j: (l, 0, 0))
    return pl.pallas_call(
        kern,
        grid=(m // tm, D_FF // tf),
        in_specs=[pl.BlockSpec((tm, D_MODEL), lambda i, j: (i, 0)),
                  pl.BlockSpec((None, spt, CONV_W - 1, tf), lambda i, j: (lp, i // tps, 0, j)),
                  pl.BlockSpec((None, D_MODEL, tf), lambda i, j: (l, 0, j)),
                  pl.BlockSpec((None, D_MODEL, tf), lambda i, j: (l, 0, j)),
                  pl.BlockSpec((None, tf, D_MODEL), lambda i, j: (l, j, 0)),
                  vec(CONV_W), vec(1), ln(wts['ln2_g']), ln(wts['ln2_b'])],
        out_specs=[pl.BlockSpec((tm, D_MODEL), lambda i, j: (i, 0)),
                   pl.BlockSpec((spt, CONV_W - 1, D_FF), lambda i, j: (i // tps, 0, 0))],
        out_shape=[jax.ShapeDtypeStruct((m, D_MODEL), F32),
                   jax.ShapeDtypeStruct((nseq, CONV_W - 1, D_FF), F32)],
        scratch_shapes=[pltpu.VMEM((tm, D_MODEL), F32),
                        pltpu.VMEM((CONV_W - 1, D_FF), F32)],
        compiler_params=_cparams(("arbitrary", "arbitrary")),
        name="ffn",
    )(x2, prev, wts['w_gate'], wts['w_up'], wts['w_down'], wts['conv_w'], wts['conv_b'],
      wts['ln2_g'], wts['ln2_b'])


def _win_kernel(wt_ref, o_ref):
    half = MLA_ROPE // 2
    lo, hi = _O_CKV + MLA_KV_RANK, _O_CKV + MLA_KV_RANK + MLA_ROPE

    def cols(a, b):
        return wt_ref[a:b, :].T.astype(BF16)

    o_ref[:, 0:lo] = cols(0, lo)
    o_ref[:, lo:_O_KR] = cols(hi, wt_ref.shape[0])
    kr = cols(lo, lo + 128)[:, 0:MLA_ROPE]
    zeros = jnp.zeros((kr.shape[0], 128 - MLA_ROPE), BF16)
    o_ref[:, _O_KR:_O_KR + MLA_ROPE] = kr
    o_ref[:, _O_KR + MLA_ROPE:_O_KRS] = zeros
    o_ref[:, _O_KRS:_O_KRS + half] = kr[:, half:]
    o_ref[:, _O_KRS + half:_O_KRS + MLA_ROPE] = kr[:, :half]
    o_ref[:, _O_KRS + MLA_ROPE:_IN_P] = zeros


def _win_relayout(w_in, tk=256):
    w_t = jnp.transpose(w_in, (0, 2, 1))
    depth, n, k = w_t.shape
    return pl.pallas_call(
        _win_kernel,
        grid=(depth, k // tk),
        in_specs=[pl.BlockSpec((None, n, tk), lambda d, t: (d, 0, t))],
        out_specs=pl.BlockSpec((None, tk, _IN_P), lambda d, t: (d, t, 0)),
        out_shape=jax.ShapeDtypeStruct((depth, k, _IN_P), BF16),
        compiler_params=_cparams(("arbitrary", "arbitrary")),
        name="win_relayout",
    )(w_t)


def _prep_weights(w_in, mla_w_uq, mla_w_ukv):
    half = MLA_ROPE // 2
    swap = lambda w: jnp.concatenate([w[..., half:], w[..., :half]], axis=-1)
    pad = lambda w: jnp.concatenate([w, jnp.zeros_like(w)], axis=-1)
    w_in_p = _win_relayout(w_in)
    uq = mla_w_uq.reshape(DEPTH, MLA_Q_RANK, MLA_HEADS, MLA_NOPE + MLA_ROPE)
    nope, rope = uq[..., :MLA_NOPE], uq[..., MLA_NOPE:]
    flat = lambda w: w.reshape(DEPTH, w.shape[1], -1)
    w_uq_p = jnp.concatenate([flat(nope), flat(pad(rope)), flat(pad(swap(rope)))],
                             axis=-1).astype(BF16)
    ukv = mla_w_ukv.reshape(DEPTH, MLA_KV_RANK, MLA_HEADS, MLA_NOPE + MLA_V)
    w_ukv_p = jnp.concatenate([flat(ukv[..., :MLA_NOPE]), flat(ukv[..., MLA_NOPE:])],
                              axis=-1).astype(BF16)
    return w_in_p, w_uq_p, w_ukv_p


def _rope_tables(pos):
    half = MLA_ROPE // 2
    inv = ROPE_THETA ** (-jnp.arange(half, dtype=F32) / half)
    ang = pos.astype(F32)[:, None] * inv
    cos, sin = jnp.cos(ang), jnp.sin(ang)
    z = jnp.zeros((pos.shape[0], 128 - MLA_ROPE), F32)
    return jnp.concatenate([cos, cos, z], -1), jnp.concatenate([-sin, sin, z], -1)


def _t5_bucket(rel):
    half = T5_BUCKETS // 2
    exact = half // 2
    n = jnp.abs(rel)
    nf = jnp.maximum(n, 1).astype(F32)
    large = exact + (jnp.log(nf / exact) / math.log(T5_MAX_DIST / exact)
                     * (half - exact)).astype(jnp.int32)
    large = jnp.minimum(large, half - 1)
    return jnp.where(rel > 0, half, 0) + jnp.where(n < exact, n, large)


def _toeplitz(fn, rows, cols, shift):
    p = rows + cols
    k = jnp.arange(p, dtype=jnp.int32)
    u = fn(jnp.where(k < cols, k, k - p) + shift).astype(F32)
    flat = jnp.tile(u, (1, rows))[:, :rows * (p - 1)]
    return flat.reshape(u.shape[0], rows, p - 1)[:, :, :cols]


def _t5_bias(t5_table, rows, cols, shift):
    return _toeplitz(lambda rel: t5_table[_t5_bucket(rel)].T, rows, cols, shift)


def _band_bias(rel_table, t):
    def fn(d):
        rel = jnp.clip(BAND_ROWS - d, -BAND_REL_CLIP, BAND_REL_CLIP) + BAND_REL_CLIP
        return rel_table[:, rel]
    bias = _toeplitz(fn, t, BAND_ROWS + t, 0)
    qc = (jnp.arange(t, dtype=jnp.int32)[:, None] + BAND_ROWS) // CHUNK
    kc = jnp.arange(BAND_ROWS + t, dtype=jnp.int32)[None, :] // CHUNK
    ok = (kc <= qc) & (kc >= qc - BAND_PREV_CHUNKS)
    return jnp.where(ok[None], bias, NEG)


def _layer(x2, nb, seq, wts, l, tabs, caches, state_bufs, lam_init, tm_proj, tm_mid, tm_ffn):
    cos_t, sin_t = tabs['rope']
    state_bufs, (dq_b, dk_b, dv_b, bq_b, bk_b, bv_b, qf, kf, v_b) = _proj(
        x2, wts, l, cos_t, sin_t, tm_proj, seq, state_bufs)
    if caches is None:
        o_a = _mla_prompt(qf, kf, v_b, nb, seq)
        o_b = _diff_prompt(dq_b, dk_b, dv_b, tabs['t5'], wts, l, nb, seq, lam_init)
        o_c = _band_prompt(bq_b, bk_b, bv_b, tabs['band'], nb, seq)
        prev, lp = jnp.zeros((1, nb, CONV_W - 1, D_FF), F32), 0
    else:
        c_ckv, c_krope, c_dk, c_dv, c_bk, c_bv, prev = caches
        plen = c_ckv.shape[1] // nb
        o_a = _mla_sample(qf, c_ckv, c_krope, state_bufs[0], state_bufs[1], wts['w_ukv'], l,
                          nb, seq, plen)
        o_b = _diff_sample(dq_b, dk_b, dv_b, c_dk, c_dv, tabs['t5'], wts, l, nb, seq, plen,
                           lam_init)
        o_c = _band_sample(bq_b, bk_b, bv_b, c_bk, c_bv, tabs['band'], l, nb, seq,
                           c_bk.shape[3])
        lp = l
    x1 = _oproj_ln(o_a, o_b, o_c, x2, wts, l, tm_mid)
    x_out, conv_state = _ffn(x1, prev, lp, wts, l, seq, tm_ffn)
    return x_out, state_bufs, conv_state


def _final_states(bufs, conv, nb, seq):
    ckv, krope_t, dk, dv, bk, bv = bufs
    return (ckv.reshape(DEPTH, nb, seq, MLA_KV_RANK), jnp.transpose(krope_t, (0, 1, 3, 2)),
            dk.reshape(DEPTH, nb, seq, DIFF_HEADS, 2 * DIFF_QK),
            dv.reshape(DEPTH, nb, seq, DIFF_HEADS, DIFF_V),
            jnp.transpose(bk, (0, 1, 3, 2, 4)), jnp.transpose(bv, (0, 1, 3, 2, 4)),
            jnp.stack(conv))


def kernel(x_prompt, x_sample, cache_mla_ckv, cache_mla_krope, cache_diff_k, cache_diff_v, cache_band_k, cache_band_v, state_ffn_conv, t5_table, w_in, mla_q_norm, mla_w_uq, mla_kv_norm, mla_w_ukv, diff_lq1, diff_lk1, diff_lq2, diff_lk2, diff_subln, band_rel_table, w_o, ln1_g, ln1_b, ffn_w_gate, ffn_w_up, ffn_conv_w, ffn_conv_b, ffn_w_down, ln2_g, ln2_b):
    nb_p, seq_p, _ = x_prompt.shape
    nb_s, seq_s, _ = x_sample.shape
    past_len = cache_mla_ckv.shape[2]
    band_len = cache_band_k.shape[2]
    assert seq_p % ATT_T == 0 and seq_s == CHUNK and past_len % CHUNK == 0
    assert band_len == BAND_ROWS

    w_in_p, w_uq_p, w_ukv_p = _prep_weights(w_in, mla_w_uq, mla_w_ukv)
    vec = lambda a: a.reshape(DEPTH, 1, -1)
    wts = {'w_in': w_in_p, 'w_uq': w_uq_p, 'w_ukv': w_ukv_p,
           'q_norm': vec(mla_q_norm), 'kv_norm': vec(mla_kv_norm),
           'lam': jnp.stack([diff_lq1, diff_lk1, diff_lq2, diff_lk2], axis=1),
           'subln': vec(diff_subln), 'w_o': w_o.astype(BF16),
           'ln1_g': vec(ln1_g), 'ln1_b': vec(ln1_b),
           'w_gate': ffn_w_gate.astype(BF16), 'w_up': ffn_w_up.astype(BF16),
           'w_down': ffn_w_down.astype(BF16), 'conv_w': ffn_conv_w, 'conv_b': vec(ffn_conv_b),
           'ln2_g': vec(ln2_g), 'ln2_b': vec(ln2_b)}
    caches = (cache_mla_ckv.reshape(DEPTH, nb_s * past_len, MLA_KV_RANK),
              jnp.transpose(cache_mla_krope, (0, 1, 3, 2)),
              cache_diff_k.reshape(DEPTH, nb_s, past_len * DIFF_HEADS, 128),
              cache_diff_v.reshape(DEPTH, nb_s, past_len * DIFF_HEADS, 128),
              jnp.transpose(cache_band_k, (0, 1, 3, 2, 4)),
              jnp.transpose(cache_band_v, (0, 1, 3, 2, 4)),
              state_ffn_conv)

    t = ATT_T
    pos_p = jnp.arange(seq_p, dtype=jnp.int32)
    pos_s = past_len + jnp.arange(seq_s, dtype=jnp.int32)
    tm_s = nb_s * seq_s
    rope_s = tuple(jnp.tile(tb, (nb_s, 1)) for tb in _rope_tables(pos_s))
    tabs_p = {'rope': _rope_tables(pos_p),
              't5': _t5_bias(t5_table, ATT_T, seq_p, ATT_T - seq_p)}
    tabs_s = {'rope': rope_s, 't5': _t5_bias(t5_table, seq_s, past_len + seq_s, -past_len)}

    y_p = x_prompt.reshape(nb_p * seq_p, D_MODEL)
    y_s = x_sample.reshape(tm_s, D_MODEL)
    bufs_p = bufs_s = None
    conv_p, conv_s = [], []
    for l in range(DEPTH):
        lam_init = 0.8 - 0.6 * math.exp(-0.3 * l)
        band_p = _band_bias(band_rel_table[l], t)
        band_s = _band_bias(band_rel_table[l], seq_s)
        y_p, bufs_p, cs = _layer(y_p, nb_p, seq_p, wts, l, dict(tabs_p, band=band_p), None,
                                 bufs_p, lam_init, tm_proj=256, tm_mid=512, tm_ffn=512)
        conv_p.append(cs)
        y_s, bufs_s, cs = _layer(y_s, nb_s, seq_s, wts, l, dict(tabs_s, band=band_s), caches,
                                 bufs_s, lam_init, tm_proj=256, tm_mid=512, tm_ffn=tm_s)
        conv_s.append(cs)
    return (y_p.reshape(nb_p, seq_p, D_MODEL), y_s.reshape(nb_s, seq_s, D_MODEL),
            *_final_states(bufs_p, conv_p, nb_p, seq_p),
            *_final_states(bufs_s, conv_s, nb_s, seq_s))
```

```python
import functools
import math

import jax
import jax.numpy as jnp
import numpy as np
from jax import lax
from jax.experimental import pallas as pl
from jax.experimental.pallas import tpu as pltpu

F32 = jnp.float32
BF16 = jnp.bfloat16

D_MODEL = 2048
DEPTH = 2
CHUNK = 64
MLA_HEADS = 6
MLA_Q_RANK = 512
MLA_KV_RANK = 256
MLA_NOPE = 128
MLA_ROPE = 64
MLA_V = 128
ROPE_THETA = 10000.0
DIFF_HEADS = 4
DIFF_QK = 64
DIFF_V = 128
BAND_HEADS = 6
BAND_DIM = 128
BAND_PREV_CHUNKS = 8
BAND_REL_CLIP = 256
T5_BUCKETS = 32
T5_MAX_DIST = 128
D_FF = 5632
CONV_W = 3
LN_EPS = 1e-5
RMS_EPS = 1e-6
DEEPNORM_ALPHA = (2 * DEPTH) ** 0.25

NEG = -1e30
MLA_QK = 256
MLA_SCALE = (MLA_NOPE + MLA_ROPE) ** -0.5
DIFF_SCALE = DIFF_QK ** -0.5
BAND_SCALE = BAND_DIM ** -0.5
BAND_ROWS = BAND_PREV_CHUNKS * CHUNK
ATT_T = 256
OPROJ_SUB = 128
VMEM_LIMIT = 60 * 1024 * 1024

_O_CQ, _O_CKV, _O_DQ, _O_DK, _O_DV = 0, 512, 768, 1280, 1792
_O_BQ, _O_BK, _O_BV, _O_KR, _O_KRS, _IN_P = 2304, 3072, 3840, 4608, 4736, 4864
_O_MIX_DIFF = MLA_HEADS * MLA_V
_O_MIX_BAND = _O_MIX_DIFF + DIFF_HEADS * DIFF_V
_N_STATE = 6


def _cparams(sem):
    return pltpu.CompilerParams(dimension_semantics=sem, vmem_limit_bytes=VMEM_LIMIT)


def _dot(a, b):
    return jnp.dot(a, b, preferred_element_type=F32)


def _dot_nt(a, b):
    return lax.dot_general(a, b, (((1,), (1,)), ((), ())), preferred_element_type=F32)


def _rms(x, g):
    return x * lax.rsqrt(jnp.mean(x * x, axis=-1, keepdims=True) + RMS_EPS) * g


def _layer_norm(x, g, b):
    mu = jnp.mean(x, axis=-1, keepdims=True)
    xc = x - mu
    var = jnp.mean(xc * xc, axis=-1, keepdims=True)
    return xc * lax.rsqrt(var + LN_EPS) * g + b


def _layer_block(arr, l):
    zeros = (0,) * (arr.ndim - 1)
    return pl.BlockSpec((None,) + arr.shape[1:], lambda *_: (l,) + zeros,
                        pipeline_mode=pl.Buffered(1))


def _heads_to_rows(o_ref, val, heads):
    rows = val.shape[0]
    for h in range(heads):
        o_ref[pl.ds(h, rows, stride=heads), :] = val[:, h * 128:(h + 1) * 128]


def _head_rows(ref, h, rows, heads):
    return ref[pl.ds(h, rows, stride=heads), :]


def _proj_kernel(*refs, layer, n_alias, streams, tiles_per_seq, band_from):
    (x_ref, win_ref, wuq_ref, wukv_ref, qn_ref, kvn_ref, cos_ref, sin_ref) = refs[:8]
    (ckv_o, krope_o, dk_o, dv_o, bk_o, bv_o,
     dq_b, dk_b, dv_b, bq_b, bk_b, bv_b, qf_b, kf_b, v_b) = refs[8 + n_alias:]
    rows = x_ref.shape[0] // streams

    def put(ref, write):
        if n_alias:
            write(ref)
            return
        for d in range(DEPTH):
            if d == layer:
                write(ref.at[d])
            else:
                ref[d] = jnp.zeros(ref.shape[1:], F32)

    def store(val):
        def write(r):
            r[...] = val
        return write

    def store_heads(val, heads):
        return lambda r: _heads_to_rows(r, val, heads)

    def store_rope_t(val):
        vt = val.T[0:MLA_ROPE]
        def write(r):
            for s in range(streams):
                r[s] = vt[:, s * rows:(s + 1) * rows]
        return write

    def store_head_major(val, heads):
        def write(r):
            for s in range(streams):
                for h in range(heads):
                    r[s, h] = val[s * rows:(s + 1) * rows, h * 128:(h + 1) * 128]
        return write

    xb = x_ref[...].astype(BF16)
    cos = cos_ref[...]
    sin = sin_ref[...]

    def seg(off, n):
        return _dot(xb, win_ref[:, off:off + n])

    cq = _rms(seg(_O_CQ, MLA_Q_RANK), qn_ref[...]).astype(BF16)
    ckv = _rms(seg(_O_CKV, MLA_KV_RANK), kvn_ref[...])
    put(ckv_o, store(ckv))
    ckvb = ckv.astype(BF16)
    krs = seg(_O_KR, 256)
    kr = krs[:, 0:128] * cos + krs[:, 128:256] * sin
    put(krope_o, store_rope_t(kr))
    krb = kr.astype(BF16)

    dq_b[...] = (seg(_O_DQ, 512) * DIFF_SCALE).astype(BF16)
    dk = seg(_O_DK, 512)
    put(dk_o, store_heads(dk, DIFF_HEADS))
    dk_b[...] = dk.astype(BF16)
    dv = seg(_O_DV, 512)
    put(dv_o, store_heads(dv, DIFF_HEADS))
    dv_b[...] = dv.astype(BF16)

    nh = MLA_HEADS * 128
    q = _dot(cq, wuq_ref[...])
    for h in range(MLA_HEADS):
        lo = h * 128
        qr = q[:, nh + lo:nh + lo + 128] * cos + q[:, 2 * nh + lo:2 * nh + lo + 128] * sin
        qf_b[h, :, 0:128] = (q[:, lo:lo + 128] * MLA_SCALE).astype(BF16)
        qf_b[h, :, 128:256] = (qr * MLA_SCALE).astype(BF16)

    kv = _dot(ckvb, wukv_ref[...])
    for h in range(MLA_HEADS):
        lo = h * 128
        kf_b[h, :, 0:128] = kv[:, lo:lo + 128].astype(BF16)
        kf_b[h, :, 128:256] = krb
    v_b[...] = kv[:, nh:2 * nh].astype(BF16)

    bq_b[...] = (seg(_O_BQ, 768) * BAND_SCALE).astype(BF16)
    bk = seg(_O_BK, 768)
    bk_b[...] = bk.astype(BF16)
    bv = seg(_O_BV, 768)
    bv_b[...] = bv.astype(BF16)

    def band_state():
        put(bk_o, store_head_major(bk, BAND_HEADS))
        put(bv_o, store_head_major(bv, BAND_HEADS))

    if band_from == 0:
        band_state()
    else:
        pl.when(pl.program_id(0) % tiles_per_seq >= band_from)(band_state)


def _proj(x2, wts, l, cos_t, sin_t, tm, seq, state_bufs):
    m = x2.shape[0]
    nb = m // seq
    period = cos_t.shape[0] // tm
    keep = min(BAND_ROWS, seq)
    n_alias = 0 if state_bufs is None else _N_STATE
    lead, li = (DEPTH, 0) if state_bufs is None else (None, l)
    if seq > tm:
        tps, kt, spt = seq // tm, keep // tm, 1
        band_from = tps - kt
        band_tile = lambda t: jnp.maximum(t % tps - band_from, 0)
    else:
        assert keep == seq
        tps, band_from, spt = 1, 0, tm // seq
        band_tile = lambda t: 0
    rows = tm // spt
    row = lambda n: pl.BlockSpec((tm, n), lambda t: (t, 0))
    srow = lambda n, k=1: pl.BlockSpec((lead, tm * k, n), lambda t: (li, t, 0))
    rope_t = pl.BlockSpec((lead, spt, MLA_ROPE, rows), lambda t: (li, t // tps, 0, t % tps))
    band = pl.BlockSpec((lead, spt, BAND_HEADS, rows, 128),
                        lambda t: (li, t // tps, 0, band_tile(t), 0))
    tab = pl.BlockSpec((tm, 128), lambda t: (t % period, 0))
    head = pl.BlockSpec((MLA_HEADS, tm, MLA_QK), lambda t: (0, t, 0))
    st = lambda *shape: jax.ShapeDtypeStruct((DEPTH,) + shape, F32)
    b16 = lambda n: jax.ShapeDtypeStruct((m, n), BF16)
    hb16 = jax.ShapeDtypeStruct((MLA_HEADS, m, MLA_QK), BF16)
    kern = functools.partial(_proj_kernel, layer=l, n_alias=n_alias, streams=spt,
                             tiles_per_seq=tps, band_from=band_from)
    ins = [x2, wts['w_in'], wts['w_uq'], wts['w_ukv'], wts['q_norm'], wts['kv_norm'],
           cos_t, sin_t]
    in_specs = [row(D_MODEL), _layer_block(wts['w_in'], l), _layer_block(wts['w_uq'], l),
                _layer_block(wts['w_ukv'], l), _layer_block(wts['q_norm'], l),
                _layer_block(wts['kv_norm'], l), tab, tab]
    if n_alias:
        ins += list(state_bufs)
        in_specs += [pl.BlockSpec(memory_space=pl.ANY)] * n_alias
    outs = pl.pallas_call(
        kern,
        grid=(m // tm,),
        in_specs=in_specs,
        out_specs=[srow(256), rope_t, srow(128, DIFF_HEADS), srow(128, DIFF_HEADS), band, band,
                   row(512), row(512), row(512), row(768), row(768), row(768),
                   head, head, row(768)],
        out_shape=[st(m, 256), st(nb, MLA_ROPE, seq), st(m * DIFF_HEADS, 128),
                   st(m * DIFF_HEADS, 128), st(nb, BAND_HEADS, keep, 128),
                   st(nb, BAND_HEADS, keep, 128),
                   b16(512), b16(512), b16(512), b16(768), b16(768), b16(768),
                   hb16, hb16, b16(768)],
        input_output_aliases={8 + k: k for k in range(n_alias)},
        compiler_params=_cparams(("arbitrary",)),
        name="proj",
    )(*ins)
    return outs[:_N_STATE], outs[_N_STATE:]


def _chunk_mask(t):
    r = lax.broadcasted_iota(jnp.int32, (t, t), 0) // CHUNK
    c = lax.broadcasted_iota(jnp.int32, (t, t), 1) // CHUNK
    return r >= c


def _softmax_pv(blocks):
    m = None
    for s, _ in blocks:
        bm = jnp.max(s, axis=-1, keepdims=True)
        m = bm if m is None else jnp.maximum(m, bm)
    l = None
    acc = None
    for s, vb in blocks:
        p = jnp.exp(s - m)
        bl = jnp.sum(p, axis=-1, keepdims=True)
        ba = _dot(p.astype(BF16), vb)
        l = bl if l is None else l + bl
        acc = ba if acc is None else acc + ba
    return acc / l


def _causal_scores(qs, k, v_ref, n, t, mask, bias=None):
    out = []
    for q in qs:
        s = _dot_nt(q, k)
        if bias is not None:
            s = s + bias
        blocks = [(jnp.where(mask, s[:, n - t:], NEG), v_ref[n - t:n, :])]
        if n > t:
            blocks.append((s[:, :n - t], v_ref[0:n - t, :]))
        out.append(blocks)
    return out


def _TILE_ORDER(nq):
    return list(range(nq))[::-1]


def _pipelined(order, scores, finish):
    nxt = scores(order[0])
    for pos, i in enumerate(order):
        cur = nxt
        if pos + 1 < len(order):
            nxt = scores(order[pos + 1])
        finish(i, cur)


def _mla_prompt_kernel(q_ref, k_ref, v_ref, o_ref, *, nq):
    t = ATT_T
    mask = _chunk_mask(t)

    def scores(tile):
        n = (tile + 1) * t
        return _causal_scores([q_ref[0, n - t:n, :]], k_ref[0, 0:n, :], v_ref, n, t, mask)

    def finish(tile, blocks):
        o_ref[tile * t:(tile + 1) * t, :] = _softmax_pv(blocks[0]).astype(BF16)

    _pipelined(_TILE_ORDER(nq), scores, finish)


def _mla_prompt(qf, kf, v, nb, seq):
    return pl.pallas_call(
        functools.partial(_mla_prompt_kernel, nq=seq // ATT_T),
        grid=(nb, MLA_HEADS),
        in_specs=[pl.BlockSpec((1, seq, MLA_QK), lambda b, h: (h, b, 0)),
                  pl.BlockSpec((1, seq, MLA_QK), lambda b, h: (h, b, 0)),
                  pl.BlockSpec((seq, MLA_V), lambda b, h: (b, h))],
        out_specs=pl.BlockSpec((seq, MLA_V), lambda b, h: (b, h)),
        out_shape=jax.ShapeDtypeStruct((nb * seq, MLA_HEADS * MLA_V), BF16),
        compiler_params=_cparams(("arbitrary",) * 2),
        name="mla_prompt",
    )(qf, kf, v)


def _mla_sample_kernel(q_ref, cp_ref, rp_ref, cn_ref, rn_ref, wukv_ref, o_ref, q_scr, *, seq):
    nh = MLA_HEADS * 128
    lat = MLA_KV_RANK
    for h in range(MLA_HEADS):
        rows = slice(h * seq, (h + 1) * seq)
        w_k = wukv_ref[:, h * 128:(h + 1) * 128]
        q_scr[rows, 0:lat] = _dot_nt(q_ref[h, :, 0:MLA_NOPE], w_k).astype(BF16)
        q_scr[rows, lat:lat + 128] = q_ref[h, :, MLA_NOPE:MLA_QK]
    q_lat = q_scr[:, 0:lat]
    q_rope = q_scr[:, lat:lat + MLA_ROPE]
    blocks = []
    for c_ref, r_ref in ((cp_ref, rp_ref), (cn_ref, rn_ref)):
        c = c_ref[...].astype(BF16)
        r_t = r_ref[...].astype(BF16)
        blocks.append((_dot_nt(q_lat, c) + _dot(q_rope, r_t), c))
    o_lat = _softmax_pv(blocks).astype(BF16)
    for h in range(MLA_HEADS):
        w_v = wukv_ref[:, nh + h * 128:nh + (h + 1) * 128]
        o_ref[:, h * 128:(h + 1) * 128] = _dot(o_lat[h * seq:(h + 1) * seq], w_v).astype(BF16)


def _mla_sample(qf, c_past, r_past, c_new, r_new, w_ukv, l, nb, seq, past):
    lat = lambda rows: pl.BlockSpec((None, rows, MLA_KV_RANK), lambda b: (l, b, 0))
    rope = lambda frames: pl.BlockSpec((None, None, MLA_ROPE, frames), lambda b: (l, b, 0, 0))
    return pl.pallas_call(
        functools.partial(_mla_sample_kernel, seq=seq),
        grid=(nb,),
        in_specs=[pl.BlockSpec((MLA_HEADS, seq, MLA_QK), lambda b: (0, b, 0)),
                  lat(past), rope(past), lat(seq), rope(seq), _layer_block(w_ukv, l)],
        out_specs=pl.BlockSpec((seq, MLA_HEADS * MLA_V), lambda b: (b, 0)),
        out_shape=jax.ShapeDtypeStruct((nb * seq, MLA_HEADS * MLA_V), BF16),
        scratch_shapes=[pltpu.VMEM((MLA_HEADS * seq, MLA_KV_RANK + 128), BF16)],
        compiler_params=_cparams(("arbitrary",)),
        name="mla_sample",
    )(qf, c_past, r_past, c_new, r_new, w_ukv)


def _diff_lambda(lam_ref, lam_init):
    v = lam_ref[...]
    a = jnp.sum(v[0:1] * v[1:2], axis=-1, keepdims=True)
    b = jnp.sum(v[2:3] * v[3:4], axis=-1, keepdims=True)
    return jnp.exp(a) - jnp.exp(b) + lam_init


def _split_q(q):
    lane = lax.broadcasted_iota(jnp.int32, q.shape, 1)
    qf = q.astype(F32)
    return (jnp.where(lane < DIFF_QK, qf, 0.0).astype(BF16),
            jnp.where(lane >= DIFF_QK, qf, 0.0).astype(BF16))


def _diff_finish(o0, o1, lam, g, lam_init):
    o = o0 - lam * o1
    return (_rms(o, g) * (1.0 - lam_init)).astype(BF16)


def _diff_prompt_kernel(q_ref, k_ref, v_ref, bias_ref, lam_ref, g_ref, o_ref, *, lam_init, nq):
    t = ATT_T
    seq = nq * t
    lam = _diff_lambda(lam_ref, lam_init)
    mask = _chunk_mask(t)

    def scores(item):
        tile, half = item
        n = (tile + 1) * t
        q = _split_q(q_ref[n - t:n, :])[half]
        return _causal_scores([q], k_ref[0:n, :], v_ref, n, t, mask,
                              bias=bias_ref[0, :, seq - n:seq])[0]

    first_half = {}

    def finish(item, blocks):
        tile, half = item
        o = _softmax_pv(blocks)
        if half == 0:
            first_half[tile] = o
        else:
            o_ref[tile * t:(tile + 1) * t, :] = _diff_finish(first_half.pop(tile), o, lam,
                                                             g_ref[...], lam_init)

    _pipelined([(tile, half) for tile in _TILE_ORDER(nq) for half in (0, 1)], scores, finish)


def _diff_prompt(dq, dk, dv, bias, wts, l, nb, seq, lam_init):
    t = ATT_T
    return pl.pallas_call(
        functools.partial(_diff_prompt_kernel, lam_init=lam_init, nq=seq // t),
        grid=(nb, DIFF_HEADS),
        in_specs=[pl.BlockSpec((seq, 128), lambda b, h: (b, h)),
                  pl.BlockSpec((seq, 128), lambda b, h: (b, h)),
                  pl.BlockSpec((seq, 128), lambda b, h: (b, h)),
                  pl.BlockSpec((1, t, seq), lambda b, h: (h, 0, 0)),
                  _layer_block(wts['lam'], l), _layer_block(wts['subln'], l)],
        out_specs=pl.BlockSpec((seq, DIFF_V), lambda b, h: (b, h)),
        out_shape=jax.ShapeDtypeStruct((nb * seq, DIFF_HEADS * DIFF_V), BF16),
        compiler_params=_cparams(("arbitrary",) * 2),
        name="diff_prompt",
    )(dq, dk, dv, bias, wts['lam'], wts['subln'])


def _diff_sample_kernel(q_ref, kp_ref, vp_ref, kn_ref, vn_ref, bias_ref, lam_ref, g_ref,
                        o_ref, *, lam_init, past):
    lam = _diff_lambda(lam_ref, lam_init)
    head_ops = {}

    def operands(h):
        if h not in head_ops:
            cols = slice(h * 128, (h + 1) * 128)
            head_ops[h] = (_split_q(q_ref[:, cols]),
                           _head_rows(kp_ref, h, past, DIFF_HEADS).astype(BF16),
                           _head_rows(vp_ref, h, past, DIFF_HEADS).astype(BF16),
                           kn_ref[:, cols], vn_ref[:, cols])
        return head_ops[h]

    def scores(item):
        h, half = item
        qs, kp, vp, kn, vn = operands(h)
        return [(_dot_nt(qs[half], kp) + bias_ref[h, :, 0:past], vp),
                (_dot_nt(qs[half], kn) + bias_ref[h, :, past:], vn)]

    first_half = {}

    def finish(item, blocks):
        h, half = item
        o = _softmax_pv(blocks)
        if half == 0:
            first_half[h] = o
        else:
            o_ref[:, h * 128:(h + 1) * 128] = _diff_finish(first_half.pop(h), o, lam,
                                                           g_ref[...], lam_init)

    _pipelined([(h, half) for h in range(DIFF_HEADS) for half in (0, 1)], scores, finish)


def _diff_sample(dq, dk_new, dv_new, k_past, v_past, bias, wts, l, nb, seq, past, lam_init):
    width = DIFF_HEADS * 128
    new = pl.BlockSpec((seq, width), lambda b: (b, 0))
    cache = pl.BlockSpec((None, None, past * DIFF_HEADS, 128), lambda b: (l, b, 0, 0))
    return pl.pallas_call(
        functools.partial(_diff_sample_kernel, lam_init=lam_init, past=past),
        grid=(nb,),
        in_specs=[new, cache, cache, new, new,
                  pl.BlockSpec(bias.shape, lambda b: (0, 0, 0)),
                  _layer_block(wts['lam'], l), _layer_block(wts['subln'], l)],
        out_specs=new,
        out_shape=jax.ShapeDtypeStruct((nb * seq, width), BF16),
        compiler_params=_cparams(("arbitrary",)),
        name="diff_sample",
    )(dq, k_past, v_past, dk_new, dv_new, bias, wts['lam'], wts['subln'])


def _band_prompt_kernel(q_ref, k_ref, v_ref, bias_ref, o_ref, *, nq):
    t = ATT_T
    full = BAND_ROWS + t

    def scores(tile):
        start = max(0, tile * t - BAND_ROWS)
        width = (tile + 1) * t - start
        s = (_dot_nt(q_ref[tile * t:(tile + 1) * t, :], k_ref[start:start + width, :])
             + bias_ref[0, :, full - width:full])
        return [(s, v_ref[start:start + width, :])]

    def finish(tile, blocks):
        o_ref[tile * t:(tile + 1) * t, :] = _softmax_pv(blocks).astype(BF16)

    _pipelined(_TILE_ORDER(nq), scores, finish)


def _band_prompt(bq, bk, bv, bias, nb, seq):
    t = ATT_T
    blk = pl.BlockSpec((seq, BAND_DIM), lambda b, h: (b, h))
    return pl.pallas_call(
        functools.partial(_band_prompt_kernel, nq=seq // t),
        grid=(nb, BAND_HEADS),
        in_specs=[blk, blk, blk,
                  pl.BlockSpec((1, t, BAND_ROWS + t), lambda b, h: (h, 0, 0))],
        out_specs=blk,
        out_shape=jax.ShapeDtypeStruct((nb * seq, BAND_HEADS * BAND_DIM), BF16),
        compiler_params=_cparams(("arbitrary",) * 2),
        name="band_prompt",
    )(bq, bk, bv, bias)


def _band_sample_kernel(q_ref, kp_ref, vp_ref, kn_ref, vn_ref, bias_ref, o_ref, *, past):
    def scores(h):
        cols = slice(h * 128, (h + 1) * 128)
        kp = kp_ref[h].astype(BF16)
        vp = vp_ref[h].astype(BF16)
        q = q_ref[:, cols]
        return [(_dot_nt(q, kp) + bias_ref[h, :, 0:past], vp),
                (_dot_nt(q, kn_ref[:, cols]) + bias_ref[h, :, past:], vn_ref[:, cols])]

    def finish(h, blocks):
        o_ref[:, h * 128:(h + 1) * 128] = _softmax_pv(blocks).astype(BF16)

    _pipelined(list(range(BAND_HEADS)), scores, finish)


def _band_sample(bq, bk_new, bv_new, k_past, v_past, bias, l, nb, seq, past):
    width = BAND_HEADS * BAND_DIM
    new = pl.BlockSpec((seq, width), lambda b: (b, 0))
    cache = pl.BlockSpec((None, None, BAND_HEADS, past, 128), lambda b: (l, b, 0, 0, 0))
    return pl.pallas_call(
        functools.partial(_band_sample_kernel, past=past),
        grid=(nb,),
        in_specs=[new, cache, cache, new, new, pl.BlockSpec(bias.shape, lambda b: (0, 0, 0))],
        out_specs=new,
        out_shape=jax.ShapeDtypeStruct((nb * seq, width), BF16),
        compiler_params=_cparams(("arbitrary",)),
        name="band_sample",
    )(bq, k_past, v_past, bk_new, bv_new, bias)


def _oproj_kernel(a_ref, b_ref, c_ref, x_ref, wo_ref, g_ref, beta_ref, o_ref):
    def mix(r):
        rows = slice(r, r + OPROJ_SUB)
        return (_dot(a_ref[rows, :], wo_ref[0:_O_MIX_DIFF, :])
                + _dot(b_ref[rows, :], wo_ref[_O_MIX_DIFF:_O_MIX_BAND, :])
                + _dot(c_ref[rows, :], wo_ref[_O_MIX_BAND:, :]))

    def finish(r, m):
        rows = slice(r, r + OPROJ_SUB)
        o_ref[rows, :] = _layer_norm(DEEPNORM_ALPHA * x_ref[rows, :] + m,
                                     g_ref[...], beta_ref[...])

    _pipelined(list(range(0, x_ref.shape[0], OPROJ_SUB)), mix, finish)


def _oproj_ln(o_a, o_b, o_c, x2, wts, l, tm):
    m = x2.shape[0]
    row = lambda n: pl.BlockSpec((tm, n), lambda t: (t, 0))
    return pl.pallas_call(
        _oproj_kernel,
        grid=(m // tm,),
        in_specs=[row(o_a.shape[1]), row(o_b.shape[1]), row(o_c.shape[1]), row(D_MODEL),
                  _layer_block(wts['w_o'], l), _layer_block(wts['ln1_g'], l),
                  _layer_block(wts['ln1_b'], l)],
        out_specs=row(D_MODEL),
        out_shape=jax.ShapeDtypeStruct((m, D_MODEL), F32),
        compiler_params=_cparams(("arbitrary",)),
        name="oproj_ln",
    )(o_a, o_b, o_c, x2, wts['w_o'], wts['ln1_g'], wts['ln1_b'])


def _ffn_kernel(x_ref, prev_ref, wg_ref, wu_ref, wd_ref, cw_ref, cb_ref, g_ref, beta_ref,
                o_ref, st_ref, carry_ref, *, tf, seg, tiles_per_seq):
    i = pl.program_id(0)
    j = pl.program_id(1)
    nj = pl.num_programs(1)
    tm = x_ref.shape[0]
    col = pl.ds(pl.multiple_of(j * tf, tf), tf)

    xb = x_ref[...].astype(BF16)
    g = _dot(xb, wg_ref[...])
    u = _dot(xb, wu_ref[...])
    cw = cw_ref[...]
    cb = cb_ref[...]
    rows = lax.broadcasted_iota(jnp.int32, (seg, tf), 0)
    seq_start = (i % tiles_per_seq) == 0

    hs = []
    for s in range(tm // seg):
        gs = g[s * seg:(s + 1) * seg]
        state = prev_ref[s]
        if tiles_per_seq > 1:
            state = jnp.where(seq_start, state, carry_ref[:, col])
        pm2, pm1 = state[0:1], state[1:2]
        gm1 = jnp.where(rows == 0, pm1, pltpu.roll(gs, 1, 0))
        gm2 = jnp.where(rows == 0, pm2, jnp.where(rows == 1, pm1, pltpu.roll(gs, 2, 0)))
        gc = cb + cw[0:1] * gm2 + cw[1:2] * gm1 + cw[2:3] * gs
        hs.append(gc * (1.0 / (1.0 + jnp.exp(-gc))) * u[s * seg:(s + 1) * seg])
        last = gs[seg - 2:seg]
        st_ref[s, :, col] = last
        if tiles_per_seq > 1:
            carry_ref[:, col] = last
    h = hs[0] if len(hs) == 1 else jnp.concatenate(hs, axis=0)
    o_ref[...] = jnp.where(j == 0, 0.0, o_ref[...]) + _dot(h.astype(BF16), wd_ref[...])

    @pl.when(j == nj - 1)
    def _():
        o_ref[...] = _layer_norm(DEEPNORM_ALPHA * x_ref[...] + o_ref[...],
                                 g_ref[...], beta_ref[...])


def _ffn(x2, prev, lp, wts, l, seq, tm, tf):
    m = x2.shape[0]
    nseq = m // seq
    seg = min(seq, tm)
    spt = tm // seg
    tps = seq // seg
    kern = functools.partial(_ffn_kernel, tf=tf, seg=seg, tiles_per_seq=tps)
    vec = lambda n: pl.BlockSpec((None, n, tf), lambda i, j: (l, 0, j))
    ln = lambda arr: pl.BlockSpec((None, 1, D_MODEL), lambda i, j: (l, 0, 0))
    return pl.pallas_call(
        kern,
        grid=(m // tm, D_FF // tf),
        in_specs=[pl.BlockSpec((tm, D_MODEL), lambda i, j: (i, 0)),
                  pl.BlockSpec((None, spt, CONV_W - 1, tf), lambda i, j: (lp, i // tps, 0, j)),
                  pl.BlockSpec((None, D_MODEL, tf), lambda i, j: (l, 0, j)),
                  pl.BlockSpec((None, D_MODEL, tf), lambda i, j: (l, 0, j)),
                  pl.BlockSpec((None, tf, D_MODEL), lambda i, j: (l, j, 0)),
                  vec(CONV_W), vec(1), ln(wts['ln2_g']), ln(wts['ln2_b'])],
        out_specs=[pl.BlockSpec((tm, D_MODEL), lambda i, j: (i, 0)),
                   pl.BlockSpec((spt, CONV_W - 1, D_FF), lambda i, j: (i // tps, 0, 0))],
        out_shape=[jax.ShapeDtypeStruct((m, D_MODEL), F32),
                   jax.ShapeDtypeStruct((nseq, CONV_W - 1, D_FF), F32)],
        scratch_shapes=[pltpu.VMEM((CONV_W - 1, D_FF), F32)],
        compiler_params=_cparams(("arbitrary", "arbitrary")),
        name="ffn",
    )(x2, prev, wts['w_gate'], wts['w_up'], wts['w_down'], wts['conv_w'], wts['conv_b'],
      wts['ln2_g'], wts['ln2_b'])


def _win_kernel(wt_ref, o_ref):
    half = MLA_ROPE // 2
    lo, hi = _O_CKV + MLA_KV_RANK, _O_CKV + MLA_KV_RANK + MLA_ROPE

    def cols(a, b):
        return wt_ref[a:b, :].T.astype(BF16)

    o_ref[:, 0:lo] = cols(0, lo)
    o_ref[:, lo:_O_KR] = cols(hi, wt_ref.shape[0])
    kr = cols(lo, lo + 128)[:, 0:MLA_ROPE]
    zeros = jnp.zeros((kr.shape[0], 128 - MLA_ROPE), BF16)
    o_ref[:, _O_KR:_O_KR + MLA_ROPE] = kr
    o_ref[:, _O_KR + MLA_ROPE:_O_KRS] = zeros
    o_ref[:, _O_KRS:_O_KRS + half] = kr[:, half:]
    o_ref[:, _O_KRS + half:_O_KRS + MLA_ROPE] = kr[:, :half]
    o_ref[:, _O_KRS + MLA_ROPE:_IN_P] = zeros


def _win_relayout(w_in, tk=256):
    w_t = jnp.transpose(w_in, (0, 2, 1))
    depth, n, k = w_t.shape
    return pl.pallas_call(
        _win_kernel,
        grid=(depth, k // tk),
        in_specs=[pl.BlockSpec((None, n, tk), lambda d, t: (d, 0, t))],
        out_specs=pl.BlockSpec((None, tk, _IN_P), lambda d, t: (d, t, 0)),
        out_shape=jax.ShapeDtypeStruct((depth, k, _IN_P), BF16),
        compiler_params=_cparams(("arbitrary", "arbitrary")),
        name="win_relayout",
    )(w_t)


def _prep_weights(w_in, mla_w_uq, mla_w_ukv):
    half = MLA_ROPE // 2
    swap = lambda w: jnp.concatenate([w[..., half:], w[..., :half]], axis=-1)
    pad = lambda w: jnp.concatenate([w, jnp.zeros_like(w)], axis=-1)
    w_in_p = _win_relayout(w_in)
    uq = mla_w_uq.reshape(DEPTH, MLA_Q_RANK, MLA_HEADS, MLA_NOPE + MLA_ROPE)
    nope, rope = uq[..., :MLA_NOPE], uq[..., MLA_NOPE:]
    flat = lambda w: w.reshape(DEPTH, w.shape[1], -1)
    w_uq_p = jnp.concatenate([flat(nope), flat(pad(rope)), flat(pad(swap(rope)))],
                             axis=-1).astype(BF16)
    ukv = mla_w_ukv.reshape(DEPTH, MLA_KV_RANK, MLA_HEADS, MLA_NOPE + MLA_V)
    w_ukv_p = jnp.concatenate([flat(ukv[..., :MLA_NOPE]), flat(ukv[..., MLA_NOPE:])],
                              axis=-1).astype(BF16)
    return w_in_p, w_uq_p, w_ukv_p


def _rope_tables(pos):
    half = MLA_ROPE // 2
    inv = ROPE_THETA ** (-jnp.arange(half, dtype=F32) / half)
    ang = pos.astype(F32)[:, None] * inv
    cos, sin = jnp.cos(ang), jnp.sin(ang)
    z = jnp.zeros((pos.shape[0], 128 - MLA_ROPE), F32)
    return jnp.concatenate([cos, cos, z], -1), jnp.concatenate([-sin, sin, z], -1)


def _t5_bucket(rel):
    half = T5_BUCKETS // 2
    exact = half // 2
    n = jnp.abs(rel)
    nf = jnp.maximum(n, 1).astype(F32)
    large = exact + (jnp.log(nf / exact) / math.log(T5_MAX_DIST / exact)
                     * (half - exact)).astype(jnp.int32)
    large = jnp.minimum(large, half - 1)
    return jnp.where(rel > 0, half, 0) + jnp.where(n < exact, n, large)


def _toeplitz(fn, rows, cols, shift):
    p = rows + cols
    k = jnp.arange(p, dtype=jnp.int32)
    u = fn(jnp.where(k < cols, k, k - p) + shift).astype(F32)
    flat = jnp.tile(u, (1, rows))[:, :rows * (p - 1)]
    return flat.reshape(u.shape[0], rows, p - 1)[:, :, :cols]


def _t5_bias(t5_table, rows, cols, shift):
    return _toeplitz(lambda rel: t5_table[_t5_bucket(rel)].T, rows, cols, shift)


def _band_bias(rel_table, t):
    def fn(d):
        rel = jnp.clip(BAND_ROWS - d, -BAND_REL_CLIP, BAND_REL_CLIP) + BAND_REL_CLIP
        return rel_table[:, rel]
    bias = _toeplitz(fn, t, BAND_ROWS + t, 0)
    qc = (jnp.arange(t, dtype=jnp.int32)[:, None] + BAND_ROWS) // CHUNK
    kc = jnp.arange(BAND_ROWS + t, dtype=jnp.int32)[None, :] // CHUNK
    ok = (kc <= qc) & (kc >= qc - BAND_PREV_CHUNKS)
    return jnp.where(ok[None], bias, NEG)


def _layer(x2, nb, seq, wts, l, tabs, caches, state_bufs, lam_init, tm_proj, tm_mid, tm_ffn, tf_ffn):
    cos_t, sin_t = tabs['rope']
    state_bufs, (dq_b, dk_b, dv_b, bq_b, bk_b, bv_b, qf, kf, v_b) = _proj(
        x2, wts, l, cos_t, sin_t, tm_proj, seq, state_bufs)
    if caches is None:
        o_a = _mla_prompt(qf, kf, v_b, nb, seq)
        o_b = _diff_prompt(dq_b, dk_b, dv_b, tabs['t5'], wts, l, nb, seq, lam_init)
        o_c = _band_prompt(bq_b, bk_b, bv_b, tabs['band'], nb, seq)
        prev, lp = jnp.zeros((1, nb, CONV_W - 1, D_FF), F32), 0
    else:
        c_ckv, c_krope, c_dk, c_dv, c_bk, c_bv, prev = caches
        plen = c_ckv.shape[1] // nb
        o_a = _mla_sample(qf, c_ckv, c_krope, state_bufs[0], state_bufs[1], wts['w_ukv'], l,
                          nb, seq, plen)
        o_b = _diff_sample(dq_b, dk_b, dv_b, c_dk, c_dv, tabs['t5'], wts, l, nb, seq, plen,
                           lam_init)
        o_c = _band_sample(bq_b, bk_b, bv_b, c_bk, c_bv, tabs['band'], l, nb, seq,
                           c_bk.shape[3])
        lp = l
    x1 = _oproj_ln(o_a, o_b, o_c, x2, wts, l, tm_mid)
    x_out, conv_state = _ffn(x1, prev, lp, wts, l, seq, tm_ffn, tf_ffn)
    return x_out, state_bufs, conv_state


def _final_states(bufs, conv, nb, seq):
    ckv, krope_t, dk, dv, bk, bv = bufs
    return (ckv.reshape(DEPTH, nb, seq, MLA_KV_RANK), jnp.transpose(krope_t, (0, 1, 3, 2)),
            dk.reshape(DEPTH, nb, seq, DIFF_HEADS, 2 * DIFF_QK),
            dv.reshape(DEPTH, nb, seq, DIFF_HEADS, DIFF_V),
            jnp.transpose(bk, (0, 1, 3, 2, 4)), jnp.transpose(bv, (0, 1, 3, 2, 4)),
            jnp.stack(conv))


def kernel(x_prompt, x_sample, cache_mla_ckv, cache_mla_krope, cache_diff_k, cache_diff_v, cache_band_k, cache_band_v, state_ffn_conv, t5_table, w_in, mla_q_norm, mla_w_uq, mla_kv_norm, mla_w_ukv, diff_lq1, diff_lk1, diff_lq2, diff_lk2, diff_subln, band_rel_table, w_o, ln1_g, ln1_b, ffn_w_gate, ffn_w_up, ffn_conv_w, ffn_conv_b, ffn_w_down, ln2_g, ln2_b):
    nb_p, seq_p, _ = x_prompt.shape
    nb_s, seq_s, _ = x_sample.shape
    past_len = cache_mla_ckv.shape[2]
    band_len = cache_band_k.shape[2]
    assert seq_p % ATT_T == 0 and seq_s == CHUNK and past_len % CHUNK == 0
    assert band_len == BAND_ROWS

    w_in_p, w_uq_p, w_ukv_p = _prep_weights(w_in, mla_w_uq, mla_w_ukv)
    vec = lambda a: a.reshape(DEPTH, 1, -1)
    wts = {'w_in': w_in_p, 'w_uq': w_uq_p, 'w_ukv': w_ukv_p,
           'q_norm': vec(mla_q_norm), 'kv_norm': vec(mla_kv_norm),
           'lam': jnp.stack([diff_lq1, diff_lk1, diff_lq2, diff_lk2], axis=1),
           'subln': vec(diff_subln), 'w_o': w_o.astype(BF16),
           'ln1_g': vec(ln1_g), 'ln1_b': vec(ln1_b),
           'w_gate': ffn_w_gate.astype(BF16), 'w_up': ffn_w_up.astype(BF16),
           'w_down': ffn_w_down.astype(BF16), 'conv_w': ffn_conv_w, 'conv_b': vec(ffn_conv_b),
           'ln2_g': vec(ln2_g), 'ln2_b': vec(ln2_b)}
    caches = (cache_mla_ckv.reshape(DEPTH, nb_s * past_len, MLA_KV_RANK),
              jnp.transpose(cache_mla_krope, (0, 1, 3, 2)),
              cache_diff_k.reshape(DEPTH, nb_s, past_len * DIFF_HEADS, 128),
              cache_diff_v.reshape(DEPTH, nb_s, past_len * DIFF_HEADS, 128),
              jnp.transpose(cache_band_k, (0, 1, 3, 2, 4)),
              jnp.transpose(cache_band_v, (0, 1, 3, 2, 4)),
              state_ffn_conv)

    t = ATT_T
    pos_p = jnp.arange(seq_p, dtype=jnp.int32)
    pos_s = past_len + jnp.arange(seq_s, dtype=jnp.int32)
    tm_s = nb_s * seq_s
    rope_s = tuple(jnp.tile(tb, (nb_s, 1)) for tb in _rope_tables(pos_s))
    tabs_p = {'rope': _rope_tables(pos_p),
              't5': _t5_bias(t5_table, ATT_T, seq_p, ATT_T - seq_p)}
    tabs_s = {'rope': rope_s, 't5': _t5_bias(t5_table, seq_s, past_len + seq_s, -past_len)}

    y_p = x_prompt.reshape(nb_p * seq_p, D_MODEL)
    y_s = x_sample.reshape(tm_s, D_MODEL)
    bufs_p = bufs_s = None
    conv_p, conv_s = [], []
    for l in range(DEPTH):
        lam_init = 0.8 - 0.6 * math.exp(-0.3 * l)
        band_p = _band_bias(band_rel_table[l], t)
        band_s = _band_bias(band_rel_table[l], seq_s)
        y_p, bufs_p, cs = _layer(y_p, nb_p, seq_p, wts, l, dict(tabs_p, band=band_p), None,
                                 bufs_p, lam_init, tm_proj=256, tm_mid=512, tm_ffn=1024, tf_ffn=512)
        conv_p.append(cs)
        y_s, bufs_s, cs = _layer(y_s, nb_s, seq_s, wts, l, dict(tabs_s, band=band_s), caches,
                                 bufs_s, lam_init, tm_proj=256, tm_mid=512, tm_ffn=tm_s, tf_ffn=512)
        conv_s.append(cs)
    return (y_p.reshape(nb_p, seq_p, D_MODEL), y_s.reshape(nb_s, seq_s, D_MODEL),
            *_final_states(bufs_p, conv_p, nb_p, seq_p),
            *_final_states(bufs_s, conv_s, nb_s, seq_s))
```

```python
import functools
import math

import jax
import jax.numpy as jnp
from jax import lax
from jax.experimental import pallas as pl
from jax.experimental.pallas import tpu as pltpu

F32 = jnp.float32
BF16 = jnp.bfloat16

D_MODEL = 2048
DEPTH = 2
CHUNK = 64
MLA_HEADS = 6
MLA_Q_RANK = 512
MLA_KV_RANK = 256
MLA_NOPE = 128
MLA_ROPE = 64
MLA_V = 128
ROPE_THETA = 10000.0
DIFF_HEADS = 4
DIFF_QK = 64
DIFF_V = 128
BAND_HEADS = 6
BAND_DIM = 128
BAND_PREV_CHUNKS = 8
BAND_REL_CLIP = 256
T5_BUCKETS = 32
T5_MAX_DIST = 128
D_FF = 5632
CONV_W = 3
LN_EPS = 1e-5
RMS_EPS = 1e-6
DEEPNORM_ALPHA = (2 * DEPTH) ** 0.25

NEG = -1e30
MLA_QK = 256
MLA_SCALE = (MLA_NOPE + MLA_ROPE) ** -0.5
DIFF_SCALE = DIFF_QK ** -0.5
BAND_SCALE = BAND_DIM ** -0.5
BAND_ROWS = BAND_PREV_CHUNKS * CHUNK
ATT_T = 256
OPROJ_SUB = 128
VMEM_LIMIT = 60 * 1024 * 1024

_O_CQ, _O_CKV, _O_DQ, _O_DK, _O_DV = 0, 512, 768, 1280, 1792
_O_BQ, _O_BK, _O_BV, _O_KR, _O_KRS, _IN_P = 2304, 3072, 3840, 4608, 4736, 4864
_O_MIX_DIFF = MLA_HEADS * MLA_V
_O_MIX_BAND = _O_MIX_DIFF + DIFF_HEADS * DIFF_V
_N_STATE = 6


def _cparams(sem):
    return pltpu.CompilerParams(dimension_semantics=sem, vmem_limit_bytes=VMEM_LIMIT)


def _dot(a, b):
    return jnp.dot(a, b, preferred_element_type=F32)


def _dot_nt(a, b):
    return lax.dot_general(a, b, (((1,), (1,)), ((), ())), preferred_element_type=F32)


def _rms(x, g):
    return x * lax.rsqrt(jnp.mean(x * x, axis=-1, keepdims=True) + RMS_EPS) * g


def _layer_norm(x, g, b):
    mu = jnp.mean(x, axis=-1, keepdims=True)
    xc = x - mu
    var = jnp.mean(xc * xc, axis=-1, keepdims=True)
    return xc * lax.rsqrt(var + LN_EPS) * g + b


def _layer_block(arr, l):
    zeros = (0,) * (arr.ndim - 1)
    return pl.BlockSpec((None,) + arr.shape[1:], lambda *_: (l,) + zeros,
                        pipeline_mode=pl.Buffered(1))


def _heads_to_rows(o_ref, val, heads):
    rows = val.shape[0]
    for h in range(heads):
        o_ref[pl.ds(h, rows, stride=heads), :] = val[:, h * 128:(h + 1) * 128]


def _head_rows(ref, h, rows, heads):
    return ref[pl.ds(h, rows, stride=heads), :]


def _proj_kernel(*refs, layer, n_alias, streams, tiles_per_seq, band_from):
    (x_ref, win_ref, wuq_ref, wukv_ref, qn_ref, kvn_ref, cos_ref, sin_ref) = refs[:8]
    (ckv_o, krope_o, dk_o, dv_o, bk_o, bv_o,
     dq_b, dk_b, dv_b, bq_b, bk_b, bv_b, qf_b, kf_b, v_b) = refs[8 + n_alias:]
    rows = x_ref.shape[0] // streams

    def put(ref, write):
        if n_alias:
            write(ref)
            return
        for d in range(DEPTH):
            if d == layer:
                write(ref.at[d])
            else:
                ref[d] = jnp.zeros(ref.shape[1:], F32)

    def store(val):
        def write(r):
            r[...] = val
        return write

    def store_heads(val, heads):
        return lambda r: _heads_to_rows(r, val, heads)

    def store_rope_t(val):
        vt = val.T[0:MLA_ROPE]
        def write(r):
            for s in range(streams):
                r[s] = vt[:, s * rows:(s + 1) * rows]
        return write

    def store_head_major(val, heads):
        def write(r):
            for s in range(streams):
                for h in range(heads):
                    r[s, h] = val[s * rows:(s + 1) * rows, h * 128:(h + 1) * 128]
        return write

    xb = x_ref[...].astype(BF16)
    cos = cos_ref[...]
    sin = sin_ref[...]

    def seg(off, n):
        return _dot(xb, win_ref[:, off:off + n])

    cq = _rms(seg(_O_CQ, MLA_Q_RANK), qn_ref[...]).astype(BF16)
    ckv = _rms(seg(_O_CKV, MLA_KV_RANK), kvn_ref[...])
    put(ckv_o, store(ckv))
    ckvb = ckv.astype(BF16)
    krs = seg(_O_KR, 256)
    kr = krs[:, 0:128] * cos + krs[:, 128:256] * sin
    put(krope_o, store_rope_t(kr))
    krb = kr.astype(BF16)

    dq_b[...] = (seg(_O_DQ, 512) * DIFF_SCALE).astype(BF16)
    dk = seg(_O_DK, 512)
    put(dk_o, store_heads(dk, DIFF_HEADS))
    dk_b[...] = dk.astype(BF16)
    dv = seg(_O_DV, 512)
    put(dv_o, store_heads(dv, DIFF_HEADS))
    dv_b[...] = dv.astype(BF16)

    nh = MLA_HEADS * 128
    q = _dot(cq, wuq_ref[...])
    for h in range(MLA_HEADS):
        lo = h * 128
        qr = q[:, nh + lo:nh + lo + 128] * cos + q[:, 2 * nh + lo:2 * nh + lo + 128] * sin
        qf_b[h, :, 0:128] = (q[:, lo:lo + 128] * MLA_SCALE).astype(BF16)
        qf_b[h, :, 128:256] = (qr * MLA_SCALE).astype(BF16)

    kv = _dot(ckvb, wukv_ref[...])
    for h in range(MLA_HEADS):
        lo = h * 128
        kf_b[h, :, 0:128] = kv[:, lo:lo + 128].astype(BF16)
        kf_b[h, :, 128:256] = krb
    v_b[...] = kv[:, nh:2 * nh].astype(BF16)

    bq_b[...] = (seg(_O_BQ, 768) * BAND_SCALE).astype(BF16)
    bk = seg(_O_BK, 768)
    bk_b[...] = bk.astype(BF16)
    bv = seg(_O_BV, 768)
    bv_b[...] = bv.astype(BF16)

    def band_state():
        put(bk_o, store_head_major(bk, BAND_HEADS))
        put(bv_o, store_head_major(bv, BAND_HEADS))

    if band_from == 0:
        band_state()
    else:
        pl.when(pl.program_id(0) % tiles_per_seq >= band_from)(band_state)


def _proj(x2, wts, l, cos_t, sin_t, tm, seq, state_bufs):
    m = x2.shape[0]
    nb = m // seq
    period = cos_t.shape[0] // tm
    keep = min(BAND_ROWS, seq)
    n_alias = 0 if state_bufs is None else _N_STATE
    lead, li = (DEPTH, 0) if state_bufs is None else (None, l)
    if seq > tm:
        tps, kt, spt = seq // tm, keep // tm, 1
        band_from = tps - kt
        band_tile = lambda t: jnp.maximum(t % tps - band_from, 0)
    else:
        assert keep == seq
        tps, band_from, spt = 1, 0, tm // seq
        band_tile = lambda t: 0
    rows = tm // spt
    row = lambda n: pl.BlockSpec((tm, n), lambda t: (t, 0))
    srow = lambda n, k=1: pl.BlockSpec((lead, tm * k, n), lambda t: (li, t, 0))
    rope_t = pl.BlockSpec((lead, spt, MLA_ROPE, rows), lambda t: (li, t // tps, 0, t % tps))
    band = pl.BlockSpec((lead, spt, BAND_HEADS, rows, 128),
                        lambda t: (li, t // tps, 0, band_tile(t), 0))
    tab = pl.BlockSpec((tm, 128), lambda t: (t % period, 0))
    head = pl.BlockSpec((MLA_HEADS, tm, MLA_QK), lambda t: (0, t, 0))
    st = lambda *shape: jax.ShapeDtypeStruct((DEPTH,) + shape, F32)
    b16 = lambda n: jax.ShapeDtypeStruct((m, n), BF16)
    hb16 = jax.ShapeDtypeStruct((MLA_HEADS, m, MLA_QK), BF16)
    kern = functools.partial(_proj_kernel, layer=l, n_alias=n_alias, streams=spt,
                             tiles_per_seq=tps, band_from=band_from)
    ins = [x2, wts['w_in'], wts['w_uq'], wts['w_ukv'], wts['q_norm'], wts['kv_norm'],
           cos_t, sin_t]
    in_specs = [row(D_MODEL), _layer_block(wts['w_in'], l), _layer_block(wts['w_uq'], l),
                _layer_block(wts['w_ukv'], l), _layer_block(wts['q_norm'], l),
                _layer_block(wts['kv_norm'], l), tab, tab]
    if n_alias:
        ins += list(state_bufs)
        in_specs += [pl.BlockSpec(memory_space=pl.ANY)] * n_alias
    outs = pl.pallas_call(
        kern,
        grid=(m // tm,),
        in_specs=in_specs,
        out_specs=[srow(256), rope_t, srow(128, DIFF_HEADS), srow(128, DIFF_HEADS), band, band,
                   row(512), row(512), row(512), row(768), row(768), row(768),
                   head, head, row(768)],
        out_shape=[st(m, 256), st(nb, MLA_ROPE, seq), st(m * DIFF_HEADS, 128),
                   st(m * DIFF_HEADS, 128), st(nb, BAND_HEADS, keep, 128),
                   st(nb, BAND_HEADS, keep, 128),
                   b16(512), b16(512), b16(512), b16(768), b16(768), b16(768),
                   hb16, hb16, b16(768)],
        input_output_aliases={8 + k: k for k in range(n_alias)},
        compiler_params=_cparams(("arbitrary",)),
        name="proj",
    )(*ins)
    return outs[:_N_STATE], outs[_N_STATE:]


def _chunk_mask(t):
    r = lax.broadcasted_iota(jnp.int32, (t, t), 0) // CHUNK
    c = lax.broadcasted_iota(jnp.int32, (t, t), 1) // CHUNK
    return r >= c


def _softmax_pv(blocks):
    m = None
    for s, _ in blocks:
        bm = jnp.max(s, axis=-1, keepdims=True)
        m = bm if m is None else jnp.maximum(m, bm)
    l = None
    acc = None
    for s, vb in blocks:
        p = jnp.exp(s - m)
        bl = jnp.sum(p, axis=-1, keepdims=True)
        ba = _dot(p.astype(BF16), vb)
        l = bl if l is None else l + bl
        acc = ba if acc is None else acc + ba
    return acc / l


def _causal_scores(qs, k, v_ref, n, t, mask, bias=None):
    out = []
    for q in qs:
        s = _dot_nt(q, k)
        if bias is not None:
            s = s + bias
        blocks = [(jnp.where(mask, s[:, n - t:], NEG), v_ref[n - t:n, :])]
        if n > t:
            blocks.append((s[:, :n - t], v_ref[0:n - t, :]))
        out.append(blocks)
    return out


def _tile_order(nq):
    return list(range(nq))[::-1]


def _pipelined(order, produce, consume):
    nxt = produce(order[0])
    for pos, i in enumerate(order):
        cur = nxt
        if pos + 1 < len(order):
            nxt = produce(order[pos + 1])
        consume(i, cur)


def _mla_prompt_kernel(q_ref, k_ref, v_ref, o_ref, *, nq):
    t = ATT_T
    mask = _chunk_mask(t)

    def scores(tile):
        n = (tile + 1) * t
        return _causal_scores([q_ref[0, n - t:n, :]], k_ref[0, 0:n, :], v_ref, n, t, mask)

    def finish(tile, blocks):
        o_ref[tile * t:(tile + 1) * t, :] = _softmax_pv(blocks[0]).astype(BF16)

    _pipelined(_tile_order(nq), scores, finish)


def _mla_prompt(qf, kf, v, nb, seq):
    return pl.pallas_call(
        functools.partial(_mla_prompt_kernel, nq=seq // ATT_T),
        grid=(nb, MLA_HEADS),
        in_specs=[pl.BlockSpec((1, seq, MLA_QK), lambda b, h: (h, b, 0)),
                  pl.BlockSpec((1, seq, MLA_QK), lambda b, h: (h, b, 0)),
                  pl.BlockSpec((seq, MLA_V), lambda b, h: (b, h))],
        out_specs=pl.BlockSpec((seq, MLA_V), lambda b, h: (b, h)),
        out_shape=jax.ShapeDtypeStruct((nb * seq, MLA_HEADS * MLA_V), BF16),
        compiler_params=_cparams(("arbitrary",) * 2),
        name="mla_prompt",
    )(qf, kf, v)


def _mla_sample_kernel(q_ref, cp_ref, rp_ref, cn_ref, rn_ref, wukv_ref, o_ref, q_scr, *, seq):
    nh = MLA_HEADS * 128
    lat = MLA_KV_RANK
    for h in range(MLA_HEADS):
        rows = slice(h * seq, (h + 1) * seq)
        w_k = wukv_ref[:, h * 128:(h + 1) * 128]
        q_scr[rows, 0:lat] = _dot_nt(q_ref[h, :, 0:MLA_NOPE], w_k).astype(BF16)
        q_scr[rows, lat:lat + 128] = q_ref[h, :, MLA_NOPE:MLA_QK]
    q_lat = q_scr[:, 0:lat]
    q_rope = q_scr[:, lat:lat + MLA_ROPE]
    blocks = []
    for c_ref, r_ref in ((cp_ref, rp_ref), (cn_ref, rn_ref)):
        c = c_ref[...].astype(BF16)
        r_t = r_ref[...].astype(BF16)
        blocks.append((_dot_nt(q_lat, c) + _dot(q_rope, r_t), c))
    o_lat = _softmax_pv(blocks).astype(BF16)
    for h in range(MLA_HEADS):
        w_v = wukv_ref[:, nh + h * 128:nh + (h + 1) * 128]
        o_ref[:, h * 128:(h + 1) * 128] = _dot(o_lat[h * seq:(h + 1) * seq], w_v).astype(BF16)


def _mla_sample(qf, c_past, r_past, c_new, r_new, w_ukv, l, nb, seq, past):
    lat = lambda rows: pl.BlockSpec((None, rows, MLA_KV_RANK), lambda b: (l, b, 0))
    rope = lambda frames: pl.BlockSpec((None, None, MLA_ROPE, frames), lambda b: (l, b, 0, 0))
    return pl.pallas_call(
        functools.partial(_mla_sample_kernel, seq=seq),
        grid=(nb,),
        in_specs=[pl.BlockSpec((MLA_HEADS, seq, MLA_QK), lambda b: (0, b, 0)),
                  lat(past), rope(past), lat(seq), rope(seq), _layer_block(w_ukv, l)],
        out_specs=pl.BlockSpec((seq, MLA_HEADS * MLA_V), lambda b: (b, 0)),
        out_shape=jax.ShapeDtypeStruct((nb * seq, MLA_HEADS * MLA_V), BF16),
        scratch_shapes=[pltpu.VMEM((MLA_HEADS * seq, MLA_KV_RANK + 128), BF16)],
        compiler_params=_cparams(("arbitrary",)),
        name="mla_sample",
    )(qf, c_past, r_past, c_new, r_new, w_ukv)


def _diff_lambda(lam_ref, lam_init):
    v = lam_ref[...]
    a = jnp.sum(v[0:1] * v[1:2], axis=-1, keepdims=True)
    b = jnp.sum(v[2:3] * v[3:4], axis=-1, keepdims=True)
    return jnp.exp(a) - jnp.exp(b) + lam_init


def _split_q(q):
    lane = lax.broadcasted_iota(jnp.int32, q.shape, 1)
    qf = q.astype(F32)
    return (jnp.where(lane < DIFF_QK, qf, 0.0).astype(BF16),
            jnp.where(lane >= DIFF_QK, qf, 0.0).astype(BF16))


def _diff_finish(o0, o1, lam, g, lam_init):
    o = o0 - lam * o1
    return (_rms(o, g) * (1.0 - lam_init)).astype(BF16)


def _diff_prompt_kernel(q_ref, k_ref, v_ref, bias_ref, lam_ref, g_ref, o_ref, *, lam_init, nq):
    t = ATT_T
    seq = nq * t
    lam = _diff_lambda(lam_ref, lam_init)
    mask = _chunk_mask(t)

    def scores(item):
        tile, half = item
        n = (tile + 1) * t
        q = _split_q(q_ref[n - t:n, :])[half]
        return _causal_scores([q], k_ref[0:n, :], v_ref, n, t, mask,
                              bias=bias_ref[0, :, seq - n:seq])[0]

    first_half = {}

    def finish(item, blocks):
        tile, half = item
        o = _softmax_pv(blocks)
        if half == 0:
            first_half[tile] = o
        else:
            o_ref[tile * t:(tile + 1) * t, :] = _diff_finish(first_half.pop(tile), o, lam,
                                                             g_ref[...], lam_init)

    _pipelined([(tile, half) for tile in _tile_order(nq) for half in (0, 1)], scores, finish)


def _diff_prompt(dq, dk, dv, bias, wts, l, nb, seq, lam_init):
    t = ATT_T
    return pl.pallas_call(
        functools.partial(_diff_prompt_kernel, lam_init=lam_init, nq=seq // t),
        grid=(nb, DIFF_HEADS),
        in_specs=[pl.BlockSpec((seq, 128), lambda b, h: (b, h)),
                  pl.BlockSpec((seq, 128), lambda b, h: (b, h)),
                  pl.BlockSpec((seq, 128), lambda b, h: (b, h)),
                  pl.BlockSpec((1, t, seq), lambda b, h: (h, 0, 0)),
                  _layer_block(wts['lam'], l), _layer_block(wts['subln'], l)],
        out_specs=pl.BlockSpec((seq, DIFF_V), lambda b, h: (b, h)),
        out_shape=jax.ShapeDtypeStruct((nb * seq, DIFF_HEADS * DIFF_V), BF16),
        compiler_params=_cparams(("arbitrary",) * 2),
        name="diff_prompt",
    )(dq, dk, dv, bias, wts['lam'], wts['subln'])


def _diff_sample_kernel(q_ref, kp_ref, vp_ref, kn_ref, vn_ref, bias_ref, lam_ref, g_ref,
                        o_ref, *, lam_init, past):
    lam = _diff_lambda(lam_ref, lam_init)
    head_ops = {}

    def operands(h):
        if h not in head_ops:
            cols = slice(h * 128, (h + 1) * 128)
            head_ops[h] = (_split_q(q_ref[:, cols]),
                           _head_rows(kp_ref, h, past, DIFF_HEADS).astype(BF16),
                           _head_rows(vp_ref, h, past, DIFF_HEADS).astype(BF16),
                           kn_ref[:, cols], vn_ref[:, cols])
        return head_ops[h]

    def scores(item):
        h, half = item
        qs, kp, vp, kn, vn = operands(h)
        return [(_dot_nt(qs[half], kp) + bias_ref[h, :, 0:past], vp),
                (_dot_nt(qs[half], kn) + bias_ref[h, :, past:], vn)]

    first_half = {}

    def finish(item, blocks):
        h, half = item
        o = _softmax_pv(blocks)
        if half == 0:
            first_half[h] = o
        else:
            o_ref[:, h * 128:(h + 1) * 128] = _diff_finish(first_half.pop(h), o, lam,
                                                           g_ref[...], lam_init)

    _pipelined([(h, half) for h in range(DIFF_HEADS) for half in (0, 1)], scores, finish)


def _diff_sample(dq, dk_new, dv_new, k_past, v_past, bias, wts, l, nb, seq, past, lam_init):
    width = DIFF_HEADS * 128
    new = pl.BlockSpec((seq, width), lambda b: (b, 0))
    cache = pl.BlockSpec((None, None, past * DIFF_HEADS, 128), lambda b: (l, b, 0, 0))
    return pl.pallas_call(
        functools.partial(_diff_sample_kernel, lam_init=lam_init, past=past),
        grid=(nb,),
        in_specs=[new, cache, cache, new, new,
                  pl.BlockSpec(bias.shape, lambda b: (0, 0, 0)),
                  _layer_block(wts['lam'], l), _layer_block(wts['subln'], l)],
        out_specs=new,
        out_shape=jax.ShapeDtypeStruct((nb * seq, width), BF16),
        compiler_params=_cparams(("arbitrary",)),
        name="diff_sample",
    )(dq, k_past, v_past, dk_new, dv_new, bias, wts['lam'], wts['subln'])


def _band_prompt_kernel(q_ref, k_ref, v_ref, bias_ref, o_ref, *, nq):
    t = ATT_T
    full = BAND_ROWS + t

    def scores(tile):
        start = max(0, tile * t - BAND_ROWS)
        width = (tile + 1) * t - start
        s = (_dot_nt(q_ref[tile * t:(tile + 1) * t, :], k_ref[start:start + width, :])
             + bias_ref[0, :, full - width:full])
        return [(s, v_ref[start:start + width, :])]

    def finish(tile, blocks):
        o_ref[tile * t:(tile + 1) * t, :] = _softmax_pv(blocks).astype(BF16)

    _pipelined(_tile_order(nq), scores, finish)


def _band_prompt(bq, bk, bv, bias, nb, seq):
    t = ATT_T
    blk = pl.BlockSpec((seq, BAND_DIM), lambda b, h: (b, h))
    return pl.pallas_call(
        functools.partial(_band_prompt_kernel, nq=seq // t),
        grid=(nb, BAND_HEADS),
        in_specs=[blk, blk, blk,
                  pl.BlockSpec((1, t, BAND_ROWS + t), lambda b, h: (h, 0, 0))],
        out_specs=blk,
        out_shape=jax.ShapeDtypeStruct((nb * seq, BAND_HEADS * BAND_DIM), BF16),
        compiler_params=_cparams(("arbitrary",) * 2),
        name="band_prompt",
    )(bq, bk, bv, bias)


def _band_sample_kernel(q_ref, kp_ref, vp_ref, kn_ref, vn_ref, bias_ref, o_ref, *, past):
    def scores(h):
        cols = slice(h * 128, (h + 1) * 128)
        kp = kp_ref[h].astype(BF16)
        vp = vp_ref[h].astype(BF16)
        q = q_ref[:, cols]
        return [(_dot_nt(q, kp) + bias_ref[h, :, 0:past], vp),
                (_dot_nt(q, kn_ref[:, cols]) + bias_ref[h, :, past:], vn_ref[:, cols])]

    def finish(h, blocks):
        o_ref[:, h * 128:(h + 1) * 128] = _softmax_pv(blocks).astype(BF16)

    _pipelined(list(range(BAND_HEADS)), scores, finish)


def _band_sample(bq, bk_new, bv_new, k_past, v_past, bias, l, nb, seq, past):
    width = BAND_HEADS * BAND_DIM
    new = pl.BlockSpec((seq, width), lambda b: (b, 0))
    cache = pl.BlockSpec((None, None, BAND_HEADS, past, 128), lambda b: (l, b, 0, 0, 0))
    return pl.pallas_call(
        functools.partial(_band_sample_kernel, past=past),
        grid=(nb,),
        in_specs=[new, cache, cache, new, new, pl.BlockSpec(bias.shape, lambda b: (0, 0, 0))],
        out_specs=new,
        out_shape=jax.ShapeDtypeStruct((nb * seq, width), BF16),
        compiler_params=_cparams(("arbitrary",)),
        name="band_sample",
    )(bq, k_past, v_past, bk_new, bv_new, bias)


def _oproj_kernel(a_ref, b_ref, c_ref, x_ref, wo_ref, g_ref, beta_ref, o_ref):
    def mix(r):
        rows = slice(r, r + OPROJ_SUB)
        return (_dot(a_ref[rows, :], wo_ref[0:_O_MIX_DIFF, :])
                + _dot(b_ref[rows, :], wo_ref[_O_MIX_DIFF:_O_MIX_BAND, :])
                + _dot(c_ref[rows, :], wo_ref[_O_MIX_BAND:, :]))

    def finish(r, m):
        rows = slice(r, r + OPROJ_SUB)
        o_ref[rows, :] = _layer_norm(DEEPNORM_ALPHA * x_ref[rows, :] + m,
                                     g_ref[...], beta_ref[...])

    _pipelined(list(range(0, x_ref.shape[0], OPROJ_SUB)), mix, finish)


def _oproj_ln(o_a, o_b, o_c, x2, wts, l, tm):
    m = x2.shape[0]
    row = lambda n: pl.BlockSpec((tm, n), lambda t: (t, 0))
    return pl.pallas_call(
        _oproj_kernel,
        grid=(m // tm,),
        in_specs=[row(o_a.shape[1]), row(o_b.shape[1]), row(o_c.shape[1]), row(D_MODEL),
                  _layer_block(wts['w_o'], l), _layer_block(wts['ln1_g'], l),
                  _layer_block(wts['ln1_b'], l)],
        out_specs=row(D_MODEL),
        out_shape=jax.ShapeDtypeStruct((m, D_MODEL), F32),
        compiler_params=_cparams(("arbitrary",)),
        name="oproj_ln",
    )(o_a, o_b, o_c, x2, wts['w_o'], wts['ln1_g'], wts['ln1_b'])


def _ffn_kernel(x_ref, prev_ref, wg_ref, wu_ref, wd_ref, cw_ref, cb_ref, g_ref, beta_ref,
                o_ref, st_ref, carry_ref, *, tf, seg, tiles_per_seq):
    i = pl.program_id(0)
    j = pl.program_id(1)
    nj = pl.num_programs(1)
    tm = x_ref.shape[0]
    col = pl.ds(pl.multiple_of(j * tf, tf), tf)

    xb = x_ref[...].astype(BF16)
    g = _dot(xb, wg_ref[...])
    u = _dot(xb, wu_ref[...])
    cw = cw_ref[...]
    cb = cb_ref[...]
    rows = lax.broadcasted_iota(jnp.int32, (seg, tf), 0)
    seq_start = (i % tiles_per_seq) == 0

    hs = []
    for s in range(tm // seg):
        gs = g[s * seg:(s + 1) * seg]
        state = prev_ref[s]
        if tiles_per_seq > 1:
            state = jnp.where(seq_start, state, carry_ref[:, col])
        pm2, pm1 = state[0:1], state[1:2]
        gm1 = jnp.where(rows == 0, pm1, pltpu.roll(gs, 1, 0))
        gm2 = jnp.where(rows == 0, pm2, jnp.where(rows == 1, pm1, pltpu.roll(gs, 2, 0)))
        gc = cb + cw[0:1] * gm2 + cw[1:2] * gm1 + cw[2:3] * gs
        hs.append(gc * (1.0 / (1.0 + jnp.exp(-gc))) * u[s * seg:(s + 1) * seg])
        last = gs[seg - 2:seg]
        st_ref[s, :, col] = last
        if tiles_per_seq > 1:
            carry_ref[:, col] = last
    h = hs[0] if len(hs) == 1 else jnp.concatenate(hs, axis=0)
    o_ref[...] = jnp.where(j == 0, 0.0, o_ref[...]) + _dot(h.astype(BF16), wd_ref[...])

    @pl.when(j == nj - 1)
    def _():
        o_ref[...] = _layer_norm(DEEPNORM_ALPHA * x_ref[...] + o_ref[...],
                                 g_ref[...], beta_ref[...])


def _ffn(x2, prev, lp, wts, l, seq, tm, tf):
    m = x2.shape[0]
    nseq = m // seq
    seg = min(seq, tm)
    spt = tm // seg
    tps = seq // seg
    kern = functools.partial(_ffn_kernel, tf=tf, seg=seg, tiles_per_seq=tps)
    vec = lambda n: pl.BlockSpec((None, n, tf), lambda i, j: (l, 0, j))
    ln = lambda arr: pl.BlockSpec((None, 1, D_MODEL), lambda i, j: (l, 0, 0))
    return pl.pallas_call(
        kern,
        grid=(m // tm, D_FF // tf),
        in_specs=[pl.BlockSpec((tm, D_MODEL), lambda i, j: (i, 0)),
                  pl.BlockSpec((None, spt, CONV_W - 1, tf), lambda i, j: (lp, i // tps, 0, j)),
                  pl.BlockSpec((None, D_MODEL, tf), lambda i, j: (l, 0, j)),
                  pl.BlockSpec((None, D_MODEL, tf), lambda i, j: (l, 0, j)),
                  pl.BlockSpec((None, tf, D_MODEL), lambda i, j: (l, j, 0)),
                  vec(CONV_W), vec(1), ln(wts['ln2_g']), ln(wts['ln2_b'])],
        out_specs=[pl.BlockSpec((tm, D_MODEL), lambda i, j: (i, 0)),
                   pl.BlockSpec((spt, CONV_W - 1, D_FF), lambda i, j: (i // tps, 0, 0))],
        out_shape=[jax.ShapeDtypeStruct((m, D_MODEL), F32),
                   jax.ShapeDtypeStruct((nseq, CONV_W - 1, D_FF), F32)],
        scratch_shapes=[pltpu.VMEM((CONV_W - 1, D_FF), F32)],
        compiler_params=_cparams(("arbitrary", "arbitrary")),
        name="ffn",
    )(x2, prev, wts['w_gate'], wts['w_up'], wts['w_down'], wts['conv_w'], wts['conv_b'],
      wts['ln2_g'], wts['ln2_b'])


def _win_kernel(wt_ref, o_ref):
    half = MLA_ROPE // 2
    lo, hi = _O_CKV + MLA_KV_RANK, _O_CKV + MLA_KV_RANK + MLA_ROPE

    def cols(a, b):
        return wt_ref[a:b, :].T.astype(BF16)

    o_ref[:, 0:lo] = cols(0, lo)
    o_ref[:, lo:_O_KR] = cols(hi, wt_ref.shape[0])
    kr = cols(lo, lo + 128)[:, 0:MLA_ROPE]
    zeros = jnp.zeros((kr.shape[0], 128 - MLA_ROPE), BF16)
    o_ref[:, _O_KR:_O_KR + MLA_ROPE] = kr
    o_ref[:, _O_KR + MLA_ROPE:_O_KRS] = zeros
    o_ref[:, _O_KRS:_O_KRS + half] = kr[:, half:]
    o_ref[:, _O_KRS + half:_O_KRS + MLA_ROPE] = kr[:, :half]
    o_ref[:, _O_KRS + MLA_ROPE:_IN_P] = zeros


def _win_relayout(w_in, tk=256):
    w_t = jnp.transpose(w_in, (0, 2, 1))
    depth, n, k = w_t.shape
    return pl.pallas_call(
        _win_kernel,
        grid=(depth, k // tk),
        in_specs=[pl.BlockSpec((None, n, tk), lambda d, t: (d, 0, t))],
        out_specs=pl.BlockSpec((None, tk, _IN_P), lambda d, t: (d, t, 0)),
        out_shape=jax.ShapeDtypeStruct((depth, k, _IN_P), BF16),
        compiler_params=_cparams(("arbitrary", "arbitrary")),
        name="win_relayout",
    )(w_t)


def _prep_weights(w_in, mla_w_uq, mla_w_ukv):
    half = MLA_ROPE // 2
    swap = lambda w: jnp.concatenate([w[..., half:], w[..., :half]], axis=-1)
    pad = lambda w: jnp.concatenate([w, jnp.zeros_like(w)], axis=-1)
    w_in_p = _win_relayout(w_in)
    uq = mla_w_uq.reshape(DEPTH, MLA_Q_RANK, MLA_HEADS, MLA_NOPE + MLA_ROPE)
    nope, rope = uq[..., :MLA_NOPE], uq[..., MLA_NOPE:]
    flat = lambda w: w.reshape(DEPTH, w.shape[1], -1)
    w_uq_p = jnp.concatenate([flat(nope), flat(pad(rope)), flat(pad(swap(rope)))],
                             axis=-1).astype(BF16)
    ukv = mla_w_ukv.reshape(DEPTH, MLA_KV_RANK, MLA_HEADS, MLA_NOPE + MLA_V)
    w_ukv_p = jnp.concatenate([flat(ukv[..., :MLA_NOPE]), flat(ukv[..., MLA_NOPE:])],
                              axis=-1).astype(BF16)
    return w_in_p, w_uq_p, w_ukv_p


def _rope_tables(pos):
    half = MLA_ROPE // 2
    inv = ROPE_THETA ** (-jnp.arange(half, dtype=F32) / half)
    ang = pos.astype(F32)[:, None] * inv
    cos, sin = jnp.cos(ang), jnp.sin(ang)
    z = jnp.zeros((pos.shape[0], 128 - MLA_ROPE), F32)
    return jnp.concatenate([cos, cos, z], -1), jnp.concatenate([-sin, sin, z], -1)


def _t5_bucket(rel):
    half = T5_BUCKETS // 2
    exact = half // 2
    n = jnp.abs(rel)
    nf = jnp.maximum(n, 1).astype(F32)
    large = exact + (jnp.log(nf / exact) / math.log(T5_MAX_DIST / exact)
                     * (half - exact)).astype(jnp.int32)
    large = jnp.minimum(large, half - 1)
    return jnp.where(rel > 0, half, 0) + jnp.where(n < exact, n, large)


def _toeplitz_kernel(u_ref, o_ref):
    rows, cols = o_ref.shape
    period = jnp.broadcast_to(u_ref[...], (rows, u_ref.shape[-1]))
    o_ref[...] = pltpu.roll(period, 0, 1, stride=1, stride_axis=0)[:, :cols]


def _toeplitz(fn, rows, cols, shift):
    p = rows + cols
    assert p % 128 == 0
    k = jnp.arange(p, dtype=jnp.int32)
    u = fn(jnp.where(k < cols, k, k - p) + shift).astype(F32)
    heads = u.shape[0]
    return pl.pallas_call(
        _toeplitz_kernel,
        grid=(heads,),
        in_specs=[pl.BlockSpec((None, 1, p), lambda h: (h, 0, 0))],
        out_specs=pl.BlockSpec((None, rows, cols), lambda h: (h, 0, 0)),
        out_shape=jax.ShapeDtypeStruct((heads, rows, cols), F32),
        compiler_params=_cparams(("arbitrary",)),
        name="toeplitz",
    )(u[:, None, :])


def _t5_bias(t5_table, rows, cols, shift):
    return _toeplitz(lambda rel: t5_table[_t5_bucket(rel)].T, rows, cols, shift)


def _band_bias(rel_table, t):
    def fn(d):
        rel = jnp.clip(BAND_ROWS - d, -BAND_REL_CLIP, BAND_REL_CLIP) + BAND_REL_CLIP
        return rel_table[:, rel]
    bias = _toeplitz(fn, t, BAND_ROWS + t, 0)
    qc = (jnp.arange(t, dtype=jnp.int32)[:, None] + BAND_ROWS) // CHUNK
    kc = jnp.arange(BAND_ROWS + t, dtype=jnp.int32)[None, :] // CHUNK
    ok = (kc <= qc) & (kc >= qc - BAND_PREV_CHUNKS)
    return jnp.where(ok[None], bias, NEG)


def _layer(x2, nb, seq, wts, l, tabs, caches, state_bufs, lam_init, tm_proj, tm_mid, tm_ffn, tf_ffn):
    cos_t, sin_t = tabs['rope']
    state_bufs, (dq_b, dk_b, dv_b, bq_b, bk_b, bv_b, qf, kf, v_b) = _proj(
        x2, wts, l, cos_t, sin_t, tm_proj, seq, state_bufs)
    if caches is None:
        o_a = _mla_prompt(qf, kf, v_b, nb, seq)
        o_b = _diff_prompt(dq_b, dk_b, dv_b, tabs['t5'], wts, l, nb, seq, lam_init)
        o_c = _band_prompt(bq_b, bk_b, bv_b, tabs['band'], nb, seq)
        prev, lp = jnp.zeros((1, nb, CONV_W - 1, D_FF), F32), 0
    else:
        c_ckv, c_krope, c_dk, c_dv, c_bk, c_bv, prev = caches
        plen = c_ckv.shape[1] // nb
        o_a = _mla_sample(qf, c_ckv, c_krope, state_bufs[0], state_bufs[1], wts['w_ukv'], l,
                          nb, seq, plen)
        o_b = _diff_sample(dq_b, dk_b, dv_b, c_dk, c_dv, tabs['t5'], wts, l, nb, seq, plen,
                           lam_init)
        o_c = _band_sample(bq_b, bk_b, bv_b, c_bk, c_bv, tabs['band'], l, nb, seq,
                           c_bk.shape[3])
        lp = l
    x1 = _oproj_ln(o_a, o_b, o_c, x2, wts, l, tm_mid)
    x_out, conv_state = _ffn(x1, prev, lp, wts, l, seq, tm_ffn, tf_ffn)
    return x_out, state_bufs, conv_state


def _final_states(bufs, conv, nb, seq):
    ckv, krope_t, dk, dv, bk, bv = bufs
    return (ckv.reshape(DEPTH, nb, seq, MLA_KV_RANK), jnp.transpose(krope_t, (0, 1, 3, 2)),
            dk.reshape(DEPTH, nb, seq, DIFF_HEADS, 2 * DIFF_QK),
            dv.reshape(DEPTH, nb, seq, DIFF_HEADS, DIFF_V),
            jnp.transpose(bk, (0, 1, 3, 2, 4)), jnp.transpose(bv, (0, 1, 3, 2, 4)),
            jnp.stack(conv))


def kernel(x_prompt, x_sample, cache_mla_ckv, cache_mla_krope, cache_diff_k, cache_diff_v, cache_band_k, cache_band_v, state_ffn_conv, t5_table, w_in, mla_q_norm, mla_w_uq, mla_kv_norm, mla_w_ukv, diff_lq1, diff_lk1, diff_lq2, diff_lk2, diff_subln, band_rel_table, w_o, ln1_g, ln1_b, ffn_w_gate, ffn_w_up, ffn_conv_w, ffn_conv_b, ffn_w_down, ln2_g, ln2_b):
    nb_p, seq_p, _ = x_prompt.shape
    nb_s, seq_s, _ = x_sample.shape
    past_len = cache_mla_ckv.shape[2]
    band_len = cache_band_k.shape[2]
    assert seq_p % ATT_T == 0 and seq_s == CHUNK and past_len % CHUNK == 0
    assert band_len == BAND_ROWS

    w_in_p, w_uq_p, w_ukv_p = _prep_weights(w_in, mla_w_uq, mla_w_ukv)
    vec = lambda a: a.reshape(DEPTH, 1, -1)
    wts = {'w_in': w_in_p, 'w_uq': w_uq_p, 'w_ukv': w_ukv_p,
           'q_norm': vec(mla_q_norm), 'kv_norm': vec(mla_kv_norm),
           'lam': jnp.stack([diff_lq1, diff_lk1, diff_lq2, diff_lk2], axis=1),
           'subln': vec(diff_subln), 'w_o': w_o.astype(BF16),
           'ln1_g': vec(ln1_g), 'ln1_b': vec(ln1_b),
           'w_gate': ffn_w_gate.astype(BF16), 'w_up': ffn_w_up.astype(BF16),
           'w_down': ffn_w_down.astype(BF16), 'conv_w': ffn_conv_w, 'conv_b': vec(ffn_conv_b),
           'ln2_g': vec(ln2_g), 'ln2_b': vec(ln2_b)}
    caches = (cache_mla_ckv.reshape(DEPTH, nb_s * past_len, MLA_KV_RANK),
              jnp.transpose(cache_mla_krope, (0, 1, 3, 2)),
              cache_diff_k.reshape(DEPTH, nb_s, past_len * DIFF_HEADS, 128),
              cache_diff_v.reshape(DEPTH, nb_s, past_len * DIFF_HEADS, 128),
              jnp.transpose(cache_band_k, (0, 1, 3, 2, 4)),
              jnp.transpose(cache_band_v, (0, 1, 3, 2, 4)),
              state_ffn_conv)

    t = ATT_T
    pos_p = jnp.arange(seq_p, dtype=jnp.int32)
    pos_s = past_len + jnp.arange(seq_s, dtype=jnp.int32)
    tm_s = nb_s * seq_s
    rope_s = tuple(jnp.tile(tb, (nb_s, 1)) for tb in _rope_tables(pos_s))
    tabs_p = {'rope': _rope_tables(pos_p),
              't5': _t5_bias(t5_table, ATT_T, seq_p, ATT_T - seq_p)}
    tabs_s = {'rope': rope_s, 't5': _t5_bias(t5_table, seq_s, past_len + seq_s, -past_len)}

    y_p = x_prompt.reshape(nb_p * seq_p, D_MODEL)
    y_s = x_sample.reshape(tm_s, D_MODEL)
    bufs_p = bufs_s = None
    conv_p, conv_s = [], []
    for l in range(DEPTH):
        lam_init = 0.8 - 0.6 * math.exp(-0.3 * l)
        band_p = _band_bias(band_rel_table[l], t)
        band_s = _band_bias(band_rel_table[l], seq_s)
        y_p, bufs_p, cs = _layer(y_p, nb_p, seq_p, wts, l, dict(tabs_p, band=band_p), None,
                                 bufs_p, lam_init, tm_proj=256, tm_mid=512, tm_ffn=1024, tf_ffn=512)
        conv_p.append(cs)
        y_s, bufs_s, cs = _layer(y_s, nb_s, seq_s, wts, l, dict(tabs_s, band=band_s), caches,
                                 bufs_s, lam_init, tm_proj=256, tm_mid=512, tm_ffn=tm_s, tf_ffn=512)
        conv_s.append(cs)
    return (y_p.reshape(nb_p, seq_p, D_MODEL), y_s.reshape(nb_s, seq_s, D_MODEL),
            *_final_states(bufs_p, conv_p, nb_p, seq_p),
            *_final_states(bufs_s, conv_s, nb_s, seq_s))
```

```python
import functools
import math

import jax
import jax.numpy as jnp
from jax import lax
from jax.experimental import pallas as pl
from jax.experimental.pallas import tpu as pltpu

F32 = jnp.float32
BF16 = jnp.bfloat16

D_MODEL = 2048
DEPTH = 2
CHUNK = 64
MLA_HEADS = 6
MLA_Q_RANK = 512
MLA_KV_RANK = 256
MLA_NOPE = 128
MLA_ROPE = 64
MLA_V = 128
ROPE_THETA = 10000.0
DIFF_HEADS = 4
DIFF_QK = 64
DIFF_V = 128
BAND_HEADS = 6
BAND_DIM = 128
BAND_PREV_CHUNKS = 8
BAND_REL_CLIP = 256
T5_BUCKETS = 32
T5_MAX_DIST = 128
D_FF = 5632
CONV_W = 3
LN_EPS = 1e-5
RMS_EPS = 1e-6
DEEPNORM_ALPHA = (2 * DEPTH) ** 0.25

NEG = -1e30
MLA_QK = 256
MLA_SCALE = (MLA_NOPE + MLA_ROPE) ** -0.5
DIFF_SCALE = DIFF_QK ** -0.5
BAND_SCALE = BAND_DIM ** -0.5
BAND_ROWS = BAND_PREV_CHUNKS * CHUNK
ATT_T = 256
ATT_HEADS_PER_STEP = 2
OPROJ_SUB = 128
VMEM_LIMIT = 60 * 1024 * 1024

_O_CQ, _O_CKV, _O_DQ, _O_DK, _O_DV = 0, 512, 768, 1280, 1792
_O_BQ, _O_BK, _O_BV, _O_KR, _O_KRS, _IN_P = 2304, 3072, 3840, 4608, 4736, 4864
_O_MIX_DIFF = MLA_HEADS * MLA_V
_O_MIX_BAND = _O_MIX_DIFF + DIFF_HEADS * DIFF_V
_N_STATE = 6


def _cparams(sem):
    return pltpu.CompilerParams(dimension_semantics=sem, vmem_limit_bytes=VMEM_LIMIT)


def _dot(a, b):
    return jnp.dot(a, b, preferred_element_type=F32)


def _dot_nt(a, b):
    return lax.dot_general(a, b, (((1,), (1,)), ((), ())), preferred_element_type=F32)


def _rms(x, g):
    return x * lax.rsqrt(jnp.mean(x * x, axis=-1, keepdims=True) + RMS_EPS) * g


def _layer_norm(x, g, b):
    mu = jnp.mean(x, axis=-1, keepdims=True)
    xc = x - mu
    var = jnp.mean(xc * xc, axis=-1, keepdims=True)
    return xc * lax.rsqrt(var + LN_EPS) * g + b


def _layer_block(arr, l):
    zeros = (0,) * (arr.ndim - 1)
    return pl.BlockSpec((None,) + arr.shape[1:], lambda *_: (l,) + zeros,
                        pipeline_mode=pl.Buffered(1))


def _heads_to_rows(o_ref, val, heads):
    rows = val.shape[0]
    for h in range(heads):
        o_ref[pl.ds(h, rows, stride=heads), :] = val[:, h * 128:(h + 1) * 128]


def _head_rows(ref, h, rows, heads):
    return ref[pl.ds(h, rows, stride=heads), :]


def _proj_kernel(*refs, layer, n_alias, streams, tiles_per_seq, band_from):
    (x_ref, win_ref, wuq_ref, wukv_ref, qn_ref, kvn_ref, cos_ref, sin_ref) = refs[:8]
    (ckv_o, krope_o, dk_o, dv_o, bk_o, bv_o,
     dq_b, dk_b, dv_b, bq_b, bk_b, bv_b, qf_b, kf_b, v_b) = refs[8 + n_alias:]
    rows = x_ref.shape[0] // streams

    def put(ref, write):
        if n_alias:
            write(ref)
            return
        for d in range(DEPTH):
            if d == layer:
                write(ref.at[d])
            else:
                ref[d] = jnp.zeros(ref.shape[1:], F32)

    def store(val):
        def write(r):
            r[...] = val
        return write

    def store_heads(val, heads):
        return lambda r: _heads_to_rows(r, val, heads)

    def store_rope_t(val):
        vt = val.T[0:MLA_ROPE]
        def write(r):
            for s in range(streams):
                r[s] = vt[:, s * rows:(s + 1) * rows]
        return write

    def store_head_major(val, heads):
        def write(r):
            for s in range(streams):
                for h in range(heads):
                    r[s, h] = val[s * rows:(s + 1) * rows, h * 128:(h + 1) * 128]
        return write

    xb = x_ref[...].astype(BF16)
    cos = cos_ref[...]
    sin = sin_ref[...]

    def seg(off, n):
        return _dot(xb, win_ref[:, off:off + n])

    cq = _rms(seg(_O_CQ, MLA_Q_RANK), qn_ref[...]).astype(BF16)
    ckv = _rms(seg(_O_CKV, MLA_KV_RANK), kvn_ref[...])
    put(ckv_o, store(ckv))
    ckvb = ckv.astype(BF16)
    krs = seg(_O_KR, 256)
    kr = krs[:, 0:128] * cos + krs[:, 128:256] * sin
    put(krope_o, store_rope_t(kr))
    krb = kr.astype(BF16)

    dq_b[...] = (seg(_O_DQ, 512) * DIFF_SCALE).astype(BF16)
    dk = seg(_O_DK, 512)
    put(dk_o, store_heads(dk, DIFF_HEADS))
    dk_b[...] = dk.astype(BF16)
    dv = seg(_O_DV, 512)
    put(dv_o, store_heads(dv, DIFF_HEADS))
    dv_b[...] = dv.astype(BF16)

    nh = MLA_HEADS * 128
    q = _dot(cq, wuq_ref[...])
    for h in range(MLA_HEADS):
        lo = h * 128
        qr = q[:, nh + lo:nh + lo + 128] * cos + q[:, 2 * nh + lo:2 * nh + lo + 128] * sin
        qf_b[h, :, 0:128] = (q[:, lo:lo + 128] * MLA_SCALE).astype(BF16)
        qf_b[h, :, 128:256] = (qr * MLA_SCALE).astype(BF16)

    kv = _dot(ckvb, wukv_ref[...])
    for h in range(MLA_HEADS):
        lo = h * 128
        kf_b[h, :, 0:128] = kv[:, lo:lo + 128].astype(BF16)
        kf_b[h, :, 128:256] = krb
    v_b[...] = kv[:, nh:2 * nh].astype(BF16)

    bq_b[...] = (seg(_O_BQ, 768) * BAND_SCALE).astype(BF16)
    bk = seg(_O_BK, 768)
    bk_b[...] = bk.astype(BF16)
    bv = seg(_O_BV, 768)
    bv_b[...] = bv.astype(BF16)

    def band_state():
        put(bk_o, store_head_major(bk, BAND_HEADS))
        put(bv_o, store_head_major(bv, BAND_HEADS))

    if band_from == 0:
        band_state()
    else:
        pl.when(pl.program_id(0) % tiles_per_seq >= band_from)(band_state)


def _proj(x2, wts, l, cos_t, sin_t, tm, seq, state_bufs):
    m = x2.shape[0]
    nb = m // seq
    period = cos_t.shape[0] // tm
    keep = min(BAND_ROWS, seq)
    n_alias = 0 if state_bufs is None else _N_STATE
    lead, li = (DEPTH, 0) if state_bufs is None else (None, l)
    if seq > tm:
        tps, kt, spt = seq // tm, keep // tm, 1
        band_from = tps - kt
        band_tile = lambda t: jnp.maximum(t % tps - band_from, 0)
    else:
        assert keep == seq
        tps, band_from, spt = 1, 0, tm // seq
        band_tile = lambda t: 0
    rows = tm // spt
    row = lambda n: pl.BlockSpec((tm, n), lambda t: (t, 0))
    srow = lambda n, k=1: pl.BlockSpec((lead, tm * k, n), lambda t: (li, t, 0))
    rope_t = pl.BlockSpec((lead, spt, MLA_ROPE, rows), lambda t: (li, t // tps, 0, t % tps))
    band = pl.BlockSpec((lead, spt, BAND_HEADS, rows, 128),
                        lambda t: (li, t // tps, 0, band_tile(t), 0))
    tab = pl.BlockSpec((tm, 128), lambda t: (t % period, 0))
    head = pl.BlockSpec((MLA_HEADS, tm, MLA_QK), lambda t: (0, t, 0))
    st = lambda *shape: jax.ShapeDtypeStruct((DEPTH,) + shape, F32)
    b16 = lambda n: jax.ShapeDtypeStruct((m, n), BF16)
    hb16 = jax.ShapeDtypeStruct((MLA_HEADS, m, MLA_QK), BF16)
    kern = functools.partial(_proj_kernel, layer=l, n_alias=n_alias, streams=spt,
                             tiles_per_seq=tps, band_from=band_from)
    ins = [x2, wts['w_in'], wts['w_uq'], wts['w_ukv'], wts['q_norm'], wts['kv_norm'],
           cos_t, sin_t]
    in_specs = [row(D_MODEL), _layer_block(wts['w_in'], l), _layer_block(wts['w_uq'], l),
                _layer_block(wts['w_ukv'], l), _layer_block(wts['q_norm'], l),
                _layer_block(wts['kv_norm'], l), tab, tab]
    if n_alias:
        ins += list(state_bufs)
        in_specs += [pl.BlockSpec(memory_space=pl.ANY)] * n_alias
    outs = pl.pallas_call(
        kern,
        grid=(m // tm,),
        in_specs=in_specs,
        out_specs=[srow(256), rope_t, srow(128, DIFF_HEADS), srow(128, DIFF_HEADS), band, band,
                   row(512), row(512), row(512), row(768), row(768), row(768),
                   head, head, row(768)],
        out_shape=[st(m, 256), st(nb, MLA_ROPE, seq), st(m * DIFF_HEADS, 128),
                   st(m * DIFF_HEADS, 128), st(nb, BAND_HEADS, keep, 128),
                   st(nb, BAND_HEADS, keep, 128),
                   b16(512), b16(512), b16(512), b16(768), b16(768), b16(768),
                   hb16, hb16, b16(768)],
        input_output_aliases={8 + k: k for k in range(n_alias)},
        compiler_params=_cparams(("arbitrary",)),
        name="proj",
    )(*ins)
    return outs[:_N_STATE], outs[_N_STATE:]


def _chunk_mask(t):
    r = lax.broadcasted_iota(jnp.int32, (t, t), 0) // CHUNK
    c = lax.broadcasted_iota(jnp.int32, (t, t), 1) // CHUNK
    return r >= c


def _softmax_pv(blocks):
    m = None
    for s, _ in blocks:
        bm = jnp.max(s, axis=-1, keepdims=True)
        m = bm if m is None else jnp.maximum(m, bm)
    l = None
    acc = None
    for s, vb in blocks:
        p = jnp.exp(s - m)
        bl = jnp.sum(p, axis=-1, keepdims=True)
        ba = _dot(p.astype(BF16), vb)
        l = bl if l is None else l + bl
        acc = ba if acc is None else acc + ba
    return acc / l


def _head_cols(ref, hh):
    return lambda a, b: ref[a:b, hh * 128:(hh + 1) * 128]


def _causal_scores(qs, k, values, n, t, mask, bias=None):
    out = []
    for q in qs:
        s = _dot_nt(q, k)
        if bias is not None:
            s = s + bias
        blocks = [(jnp.where(mask, s[:, n - t:], NEG), values(n - t, n))]
        if n > t:
            blocks.append((s[:, :n - t], values(0, n - t)))
        out.append(blocks)
    return out


def _tile_order(nq):
    return [(tile, hh) for tile in reversed(range(nq)) for hh in range(ATT_HEADS_PER_STEP)]


def _pipelined(order, produce, consume):
    nxt = produce(order[0])
    for pos, i in enumerate(order):
        cur = nxt
        if pos + 1 < len(order):
            nxt = produce(order[pos + 1])
        consume(i, cur)


def _mla_prompt_kernel(q_ref, k_ref, v_ref, o_ref, *, nq):
    t = ATT_T
    mask = _chunk_mask(t)

    def scores(item):
        tile, hh = item
        n = (tile + 1) * t
        return _causal_scores([q_ref[hh, n - t:n, :]], k_ref[hh, 0:n, :],
                              _head_cols(v_ref, hh), n, t, mask)

    def finish(item, blocks):
        tile, hh = item
        o_ref[tile * t:(tile + 1) * t, hh * 128:(hh + 1) * 128] = (
            _softmax_pv(blocks[0]).astype(BF16))

    _pipelined(_tile_order(nq), scores, finish)


def _mla_prompt(qf, kf, v, nb, seq):
    hps = ATT_HEADS_PER_STEP
    return pl.pallas_call(
        functools.partial(_mla_prompt_kernel, nq=seq // ATT_T),
        grid=(nb, MLA_HEADS // hps),
        in_specs=[pl.BlockSpec((hps, seq, MLA_QK), lambda b, h: (h, b, 0)),
                  pl.BlockSpec((hps, seq, MLA_QK), lambda b, h: (h, b, 0)),
                  pl.BlockSpec((seq, hps * MLA_V), lambda b, h: (b, h))],
        out_specs=pl.BlockSpec((seq, hps * MLA_V), lambda b, h: (b, h)),
        out_shape=jax.ShapeDtypeStruct((nb * seq, MLA_HEADS * MLA_V), BF16),
        compiler_params=_cparams(("arbitrary",) * 2),
        name="mla_prompt",
    )(qf, kf, v)


def _mla_sample_kernel(q_ref, cp_ref, rp_ref, cn_ref, rn_ref, wukv_ref, o_ref, q_scr, *, seq):
    nh = MLA_HEADS * 128
    lat = MLA_KV_RANK
    for h in range(MLA_HEADS):
        rows = slice(h * seq, (h + 1) * seq)
        w_k = wukv_ref[:, h * 128:(h + 1) * 128]
        q_scr[rows, 0:lat] = _dot_nt(q_ref[h, :, 0:MLA_NOPE], w_k).astype(BF16)
        q_scr[rows, lat:lat + 128] = q_ref[h, :, MLA_NOPE:MLA_QK]
    q_lat = q_scr[:, 0:lat]
    q_rope = q_scr[:, lat:lat + MLA_ROPE]
    blocks = []
    for c_ref, r_ref in ((cp_ref, rp_ref), (cn_ref, rn_ref)):
        c = c_ref[...].astype(BF16)
        r_t = r_ref[...].astype(BF16)
        blocks.append((_dot_nt(q_lat, c) + _dot(q_rope, r_t), c))
    o_lat = _softmax_pv(blocks).astype(BF16)
    for h in range(MLA_HEADS):
        w_v = wukv_ref[:, nh + h * 128:nh + (h + 1) * 128]
        o_ref[:, h * 128:(h + 1) * 128] = _dot(o_lat[h * seq:(h + 1) * seq], w_v).astype(BF16)


def _mla_sample(qf, c_past, r_past, c_new, r_new, w_ukv, l, nb, seq, past):
    lat = lambda rows: pl.BlockSpec((None, rows, MLA_KV_RANK), lambda b: (l, b, 0))
    rope = lambda frames: pl.BlockSpec((None, None, MLA_ROPE, frames), lambda b: (l, b, 0, 0))
    return pl.pallas_call(
        functools.partial(_mla_sample_kernel, seq=seq),
        grid=(nb,),
        in_specs=[pl.BlockSpec((MLA_HEADS, seq, MLA_QK), lambda b: (0, b, 0)),
                  lat(past), rope(past), lat(seq), rope(seq), _layer_block(w_ukv, l)],
        out_specs=pl.BlockSpec((seq, MLA_HEADS * MLA_V), lambda b: (b, 0)),
        out_shape=jax.ShapeDtypeStruct((nb * seq, MLA_HEADS * MLA_V), BF16),
        scratch_shapes=[pltpu.VMEM((MLA_HEADS * seq, MLA_KV_RANK + 128), BF16)],
        compiler_params=_cparams(("arbitrary",)),
        name="mla_sample",
    )(qf, c_past, r_past, c_new, r_new, w_ukv)


def _diff_lambda(lam_ref, lam_init):
    v = lam_ref[...]
    a = jnp.sum(v[0:1] * v[1:2], axis=-1, keepdims=True)
    b = jnp.sum(v[2:3] * v[3:4], axis=-1, keepdims=True)
    return jnp.exp(a) - jnp.exp(b) + lam_init


def _split_q(q):
    lane = lax.broadcasted_iota(jnp.int32, q.shape, 1)
    qf = q.astype(F32)
    return (jnp.where(lane < DIFF_QK, qf, 0.0).astype(BF16),
            jnp.where(lane >= DIFF_QK, qf, 0.0).astype(BF16))


def _diff_finish(o0, o1, lam, g, lam_init):
    o = o0 - lam * o1
    return (_rms(o, g) * (1.0 - lam_init)).astype(BF16)


def _diff_prompt_kernel(q_ref, k_ref, v_ref, bias_ref, lam_ref, g_ref, o_ref, *, lam_init, nq):
    t = ATT_T
    seq = nq * t
    lam = _diff_lambda(lam_ref, lam_init)
    mask = _chunk_mask(t)

    def scores(item):
        tile, hh, half = item
        n = (tile + 1) * t
        cols = slice(hh * 128, (hh + 1) * 128)
        q = _split_q(q_ref[n - t:n, cols])[half]
        return _causal_scores([q], k_ref[0:n, cols], _head_cols(v_ref, hh), n, t, mask,
                              bias=bias_ref[hh, :, seq - n:seq])[0]

    first_half = {}

    def finish(item, blocks):
        tile, hh, half = item
        o = _softmax_pv(blocks)
        if half == 0:
            first_half[tile, hh] = o
        else:
            o_ref[tile * t:(tile + 1) * t, hh * 128:(hh + 1) * 128] = _diff_finish(
                first_half.pop((tile, hh)), o, lam, g_ref[...], lam_init)

    _pipelined([(tile, hh, half) for tile, hh in _tile_order(nq) for half in (0, 1)],
               scores, finish)


def _diff_prompt(dq, dk, dv, bias, wts, l, nb, seq, lam_init):
    t = ATT_T
    hps = ATT_HEADS_PER_STEP
    blk = pl.BlockSpec((seq, hps * 128), lambda b, h: (b, h))
    return pl.pallas_call(
        functools.partial(_diff_prompt_kernel, lam_init=lam_init, nq=seq // t),
        grid=(nb, DIFF_HEADS // hps),
        in_specs=[blk, blk, blk,
                  pl.BlockSpec((hps, t, seq), lambda b, h: (h, 0, 0)),
                  _layer_block(wts['lam'], l), _layer_block(wts['subln'], l)],
        out_specs=blk,
        out_shape=jax.ShapeDtypeStruct((nb * seq, DIFF_HEADS * DIFF_V), BF16),
        compiler_params=_cparams(("arbitrary",) * 2),
        name="diff_prompt",
    )(dq, dk, dv, bias, wts['lam'], wts['subln'])


def _diff_sample_kernel(q_ref, kp_ref, vp_ref, kn_ref, vn_ref, bias_ref, lam_ref, g_ref,
                        o_ref, *, lam_init, past):
    lam = _diff_lambda(lam_ref, lam_init)
    head_ops = {}

    def operands(h):
        if h not in head_ops:
            cols = slice(h * 128, (h + 1) * 128)
            head_ops[h] = (_split_q(q_ref[:, cols]),
                           _head_rows(kp_ref, h, past, DIFF_HEADS).astype(BF16),
                           _head_rows(vp_ref, h, past, DIFF_HEADS).astype(BF16),
                           kn_ref[:, cols], vn_ref[:, cols])
        return head_ops[h]

    def scores(item):
        h, half = item
        qs, kp, vp, kn, vn = operands(h)
        return [(_dot_nt(qs[half], kp) + bias_ref[h, :, 0:past], vp),
                (_dot_nt(qs[half], kn) + bias_ref[h, :, past:], vn)]

    first_half = {}

    def finish(item, blocks):
        h, half = item
        o = _softmax_pv(blocks)
        if half == 0:
            first_half[h] = o
        else:
            o_ref[:, h * 128:(h + 1) * 128] = _diff_finish(first_half.pop(h), o, lam,
                                                           g_ref[...], lam_init)

    _pipelined([(h, half) for h in range(DIFF_HEADS) for half in (0, 1)], scores, finish)


def _diff_sample(dq, dk_new, dv_new, k_past, v_past, bias, wts, l, nb, seq, past, lam_init):
    width = DIFF_HEADS * 128
    new = pl.BlockSpec((seq, width), lambda b: (b, 0))
    cache = pl.BlockSpec((None, None, past * DIFF_HEADS, 128), lambda b: (l, b, 0, 0))
    return pl.pallas_call(
        functools.partial(_diff_sample_kernel, lam_init=lam_init, past=past),
        grid=(nb,),
        in_specs=[new, cache, cache, new, new,
                  pl.BlockSpec(bias.shape, lambda b: (0, 0, 0)),
                  _layer_block(wts['lam'], l), _layer_block(wts['subln'], l)],
        out_specs=new,
        out_shape=jax.ShapeDtypeStruct((nb * seq, width), BF16),
        compiler_params=_cparams(("arbitrary",)),
        name="diff_sample",
    )(dq, k_past, v_past, dk_new, dv_new, bias, wts['lam'], wts['subln'])


def _band_prompt_kernel(q_ref, k_ref, v_ref, bias_ref, o_ref, *, nq):
    t = ATT_T
    full = BAND_ROWS + t

    def scores(item):
        tile, hh = item
        cols = slice(hh * 128, (hh + 1) * 128)
        start = max(0, tile * t - BAND_ROWS)
        width = (tile + 1) * t - start
        s = (_dot_nt(q_ref[tile * t:(tile + 1) * t, cols], k_ref[start:start + width, cols])
             + bias_ref[hh, :, full - width:full])
        return [(s, v_ref[start:start + width, cols])]

    def finish(item, blocks):
        tile, hh = item
        o_ref[tile * t:(tile + 1) * t, hh * 128:(hh + 1) * 128] = (
            _softmax_pv(blocks).astype(BF16))

    _pipelined(_tile_order(nq), scores, finish)


def _band_prompt(bq, bk, bv, bias, nb, seq):
    t = ATT_T
    hps = ATT_HEADS_PER_STEP
    blk = pl.BlockSpec((seq, hps * BAND_DIM), lambda b, h: (b, h))
    return pl.pallas_call(
        functools.partial(_band_prompt_kernel, nq=seq // t),
        grid=(nb, BAND_HEADS // hps),
        in_specs=[blk, blk, blk,
                  pl.BlockSpec((hps, t, BAND_ROWS + t), lambda b, h: (h, 0, 0))],
        out_specs=blk,
        out_shape=jax.ShapeDtypeStruct((nb * seq, BAND_HEADS * BAND_DIM), BF16),
        compiler_params=_cparams(("arbitrary",) * 2),
        name="band_prompt",
    )(bq, bk, bv, bias)


def _band_sample_kernel(q_ref, kp_ref, vp_ref, kn_ref, vn_ref, bias_ref, o_ref, *, past):
    def scores(h):
        cols = slice(h * 128, (h + 1) * 128)
        kp = kp_ref[h].astype(BF16)
        vp = vp_ref[h].astype(BF16)
        q = q_ref[:, cols]
        return [(_dot_nt(q, kp) + bias_ref[h, :, 0:past], vp),
                (_dot_nt(q, kn_ref[:, cols]) + bias_ref[h, :, past:], vn_ref[:, cols])]

    def finish(h, blocks):
        o_ref[:, h * 128:(h + 1) * 128] = _softmax_pv(blocks).astype(BF16)

    _pipelined(list(range(BAND_HEADS)), scores, finish)


def _band_sample(bq, bk_new, bv_new, k_past, v_past, bias, l, nb, seq, past):
    width = BAND_HEADS * BAND_DIM
    new = pl.BlockSpec((seq, width), lambda b: (b, 0))
    cache = pl.BlockSpec((None, None, BAND_HEADS, past, 128), lambda b: (l, b, 0, 0, 0))
    return pl.pallas_call(
        functools.partial(_band_sample_kernel, past=past),
        grid=(nb,),
        in_specs=[new, cache, cache, new, new, pl.BlockSpec(bias.shape, lambda b: (0, 0, 0))],
        out_specs=new,
        out_shape=jax.ShapeDtypeStruct((nb * seq, width), BF16),
        compiler_params=_cparams(("arbitrary",)),
        name="band_sample",
    )(bq, k_past, v_past, bk_new, bv_new, bias)


def _oproj_kernel(a_ref, b_ref, c_ref, x_ref, wo_ref, g_ref, beta_ref, o_ref):
    def mix(r):
        rows = slice(r, r + OPROJ_SUB)
        return (_dot(a_ref[rows, :], wo_ref[0:_O_MIX_DIFF, :])
                + _dot(b_ref[rows, :], wo_ref[_O_MIX_DIFF:_O_MIX_BAND, :])
                + _dot(c_ref[rows, :], wo_ref[_O_MIX_BAND:, :]))

    def finish(r, m):
        rows = slice(r, r + OPROJ_SUB)
        o_ref[rows, :] = _layer_norm(DEEPNORM_ALPHA * x_ref[rows, :] + m,
                                     g_ref[...], beta_ref[...])

    _pipelined(list(range(0, x_ref.shape[0], OPROJ_SUB)), mix, finish)


def _oproj_ln(o_a, o_b, o_c, x2, wts, l, tm):
    m = x2.shape[0]
    row = lambda n: pl.BlockSpec((tm, n), lambda t: (t, 0))
    return pl.pallas_call(
        _oproj_kernel,
        grid=(m // tm,),
        in_specs=[row(o_a.shape[1]), row(o_b.shape[1]), row(o_c.shape[1]), row(D_MODEL),
                  _layer_block(wts['w_o'], l), _layer_block(wts['ln1_g'], l),
                  _layer_block(wts['ln1_b'], l)],
        out_specs=row(D_MODEL),
        out_shape=jax.ShapeDtypeStruct((m, D_MODEL), F32),
        compiler_params=_cparams(("arbitrary",)),
        name="oproj_ln",
    )(o_a, o_b, o_c, x2, wts['w_o'], wts['ln1_g'], wts['ln1_b'])


def _ffn_kernel(x_ref, prev_ref, wg_ref, wu_ref, wd_ref, cw_ref, cb_ref, g_ref, beta_ref,
                o_ref, st_ref, carry_ref, *, tf, seg, tiles_per_seq):
    i = pl.program_id(0)
    j = pl.program_id(1)
    nj = pl.num_programs(1)
    tm = x_ref.shape[0]
    col = pl.ds(pl.multiple_of(j * tf, tf), tf)

    xb = x_ref[...].astype(BF16)
    g = _dot(xb, wg_ref[...])
    u = _dot(xb, wu_ref[...])
    cw = cw_ref[...]
    cb = cb_ref[...]
    rows = lax.broadcasted_iota(jnp.int32, (seg, tf), 0)
    seq_start = (i % tiles_per_seq) == 0

    hs = []
    for s in range(tm // seg):
        gs = g[s * seg:(s + 1) * seg]
        state = prev_ref[s]
        if tiles_per_seq > 1:
            state = jnp.where(seq_start, state, carry_ref[:, col])
        pm2, pm1 = state[0:1], state[1:2]
        gm1 = jnp.where(rows == 0, pm1, pltpu.roll(gs, 1, 0))
        gm2 = jnp.where(rows == 0, pm2, jnp.where(rows == 1, pm1, pltpu.roll(gs, 2, 0)))
        gc = cb + cw[0:1] * gm2 + cw[1:2] * gm1 + cw[2:3] * gs
        hs.append(gc * (1.0 / (1.0 + jnp.exp(-gc))) * u[s * seg:(s + 1) * seg])
        last = gs[seg - 2:seg]
        st_ref[s, :, col] = last
        if tiles_per_seq > 1:
            carry_ref[:, col] = last
    h = hs[0] if len(hs) == 1 else jnp.concatenate(hs, axis=0)
    o_ref[...] = jnp.where(j == 0, 0.0, o_ref[...]) + _dot(h.astype(BF16), wd_ref[...])

    @pl.when(j == nj - 1)
    def _():
        o_ref[...] = _layer_norm(DEEPNORM_ALPHA * x_ref[...] + o_ref[...],
                                 g_ref[...], beta_ref[...])


def _ffn(x2, prev, lp, wts, l, seq, tm, tf):
    m = x2.shape[0]
    nseq = m // seq
    seg = min(seq, tm)
    spt = tm // seg
    tps = seq // seg
    kern = functools.partial(_ffn_kernel, tf=tf, seg=seg, tiles_per_seq=tps)
    vec = lambda n: pl.BlockSpec((None, n, tf), lambda i, j: (l, 0, j))
    ln = lambda arr: pl.BlockSpec((None, 1, D_MODEL), lambda i, j: (l, 0, 0))
    return pl.pallas_call(
        kern,
        grid=(m // tm, D_FF // tf),
        in_specs=[pl.BlockSpec((tm, D_MODEL), lambda i, j: (i, 0)),
                  pl.BlockSpec((None, spt, CONV_W - 1, tf), lambda i, j: (lp, i // tps, 0, j)),
                  pl.BlockSpec((None, D_MODEL, tf), lambda i, j: (l, 0, j)),
                  pl.BlockSpec((None, D_MODEL, tf), lambda i, j: (l, 0, j)),
                  pl.BlockSpec((None, tf, D_MODEL), lambda i, j: (l, j, 0)),
                  vec(CONV_W), vec(1), ln(wts['ln2_g']), ln(wts['ln2_b'])],
        out_specs=[pl.BlockSpec((tm, D_MODEL), lambda i, j: (i, 0)),
                   pl.BlockSpec((spt, CONV_W - 1, D_FF), lambda i, j: (i // tps, 0, 0))],
        out_shape=[jax.ShapeDtypeStruct((m, D_MODEL), F32),
                   jax.ShapeDtypeStruct((nseq, CONV_W - 1, D_FF), F32)],
        scratch_shapes=[pltpu.VMEM((CONV_W - 1, D_FF), F32)],
        compiler_params=_cparams(("arbitrary", "arbitrary")),
        name="ffn",
    )(x2, prev, wts['w_gate'], wts['w_up'], wts['w_down'], wts['conv_w'], wts['conv_b'],
      wts['ln2_g'], wts['ln2_b'])


def _win_kernel(wt_ref, o_ref):
    half = MLA_ROPE // 2
    lo, hi = _O_CKV + MLA_KV_RANK, _O_CKV + MLA_KV_RANK + MLA_ROPE

    def cols(a, b):
        return wt_ref[a:b, :].T.astype(BF16)

    o_ref[:, 0:lo] = cols(0, lo)
    o_ref[:, lo:_O_KR] = cols(hi, wt_ref.shape[0])
    kr = cols(lo, lo + 128)[:, 0:MLA_ROPE]
    zeros = jnp.zeros((kr.shape[0], 128 - MLA_ROPE), BF16)
    o_ref[:, _O_KR:_O_KR + MLA_ROPE] = kr
    o_ref[:, _O_KR + MLA_ROPE:_O_KRS] = zeros
    o_ref[:, _O_KRS:_O_KRS + half] = kr[:, half:]
    o_ref[:, _O_KRS + half:_O_KRS + MLA_ROPE] = kr[:, :half]
    o_ref[:, _O_KRS + MLA_ROPE:_IN_P] = zeros


def _win_relayout(w_in, tk=256):
    w_t = jnp.transpose(w_in, (0, 2, 1))
    depth, n, k = w_t.shape
    return pl.pallas_call(
        _win_kernel,
        grid=(depth, k // tk),
        in_specs=[pl.BlockSpec((None, n, tk), lambda d, t: (d, 0, t))],
        out_specs=pl.BlockSpec((None, tk, _IN_P), lambda d, t: (d, t, 0)),
        out_shape=jax.ShapeDtypeStruct((depth, k, _IN_P), BF16),
        compiler_params=_cparams(("arbitrary", "arbitrary")),
        name="win_relayout",
    )(w_t)


def _prep_weights(w_in, mla_w_uq, mla_w_ukv):
    half = MLA_ROPE // 2
    swap = lambda w: jnp.concatenate([w[..., half:], w[..., :half]], axis=-1)
    pad = lambda w: jnp.concatenate([w, jnp.zeros_like(w)], axis=-1)
    w_in_p = _win_relayout(w_in)
    uq = mla_w_uq.reshape(DEPTH, MLA_Q_RANK, MLA_HEADS, MLA_NOPE + MLA_ROPE)
    nope, rope = uq[..., :MLA_NOPE], uq[..., MLA_NOPE:]
    flat = lambda w: w.reshape(DEPTH, w.shape[1], -1)
    w_uq_p = jnp.concatenate([flat(nope), flat(pad(rope)), flat(pad(swap(rope)))],
                             axis=-1).astype(BF16)
    ukv = mla_w_ukv.reshape(DEPTH, MLA_KV_RANK, MLA_HEADS, MLA_NOPE + MLA_V)
    w_ukv_p = jnp.concatenate([flat(ukv[..., :MLA_NOPE]), flat(ukv[..., MLA_NOPE:])],
                              axis=-1).astype(BF16)
    return w_in_p, w_uq_p, w_ukv_p


def _rope_tables(pos):
    half = MLA_ROPE // 2
    inv = ROPE_THETA ** (-jnp.arange(half, dtype=F32) / half)
    ang = pos.astype(F32)[:, None] * inv
    cos, sin = jnp.cos(ang), jnp.sin(ang)
    z = jnp.zeros((pos.shape[0], 128 - MLA_ROPE), F32)
    return jnp.concatenate([cos, cos, z], -1), jnp.concatenate([-sin, sin, z], -1)


def _t5_bucket(rel):
    half = T5_BUCKETS // 2
    exact = half // 2
    n = jnp.abs(rel)
    nf = jnp.maximum(n, 1).astype(F32)
    large = exact + (jnp.log(nf / exact) / math.log(T5_MAX_DIST / exact)
                     * (half - exact)).astype(jnp.int32)
    large = jnp.minimum(large, half - 1)
    return jnp.where(rel > 0, half, 0) + jnp.where(n < exact, n, large)


def _toeplitz_kernel(u_ref, o_ref):
    rows, cols = o_ref.shape
    period = jnp.broadcast_to(u_ref[...], (rows, u_ref.shape[-1]))
    o_ref[...] = pltpu.roll(period, 0, 1, stride=1, stride_axis=0)[:, :cols]


def _toeplitz(fn, rows, cols, shift):
    p = rows + cols
    assert p % 128 == 0
    k = jnp.arange(p, dtype=jnp.int32)
    u = fn(jnp.where(k < cols, k, k - p) + shift).astype(F32)
    heads = u.shape[0]
    return pl.pallas_call(
        _toeplitz_kernel,
        grid=(heads,),
        in_specs=[pl.BlockSpec((None, 1, p), lambda h: (h, 0, 0))],
        out_specs=pl.BlockSpec((None, rows, cols), lambda h: (h, 0, 0)),
        out_shape=jax.ShapeDtypeStruct((heads, rows, cols), F32),
        compiler_params=_cparams(("arbitrary",)),
        name="toeplitz",
    )(u[:, None, :])


def _t5_bias(t5_table, rows, cols, shift):
    return _toeplitz(lambda rel: t5_table[_t5_bucket(rel)].T, rows, cols, shift)


def _band_bias(rel_table, t):
    def fn(d):
        rel = jnp.clip(BAND_ROWS - d, -BAND_REL_CLIP, BAND_REL_CLIP) + BAND_REL_CLIP
        return rel_table[:, rel]
    bias = _toeplitz(fn, t, BAND_ROWS + t, 0)
    qc = (jnp.arange(t, dtype=jnp.int32)[:, None] + BAND_ROWS) // CHUNK
    kc = jnp.arange(BAND_ROWS + t, dtype=jnp.int32)[None, :] // CHUNK
    ok = (kc <= qc) & (kc >= qc - BAND_PREV_CHUNKS)
    return jnp.where(ok[None], bias, NEG)


def _layer(x2, nb, seq, wts, l, tabs, caches, state_bufs, lam_init, tm_proj, tm_mid, tm_ffn, tf_ffn):
    cos_t, sin_t = tabs['rope']
    state_bufs, (dq_b, dk_b, dv_b, bq_b, bk_b, bv_b, qf, kf, v_b) = _proj(
        x2, wts, l, cos_t, sin_t, tm_proj, seq, state_bufs)
    if caches is None:
        o_a = _mla_prompt(qf, kf, v_b, nb, seq)
        o_b = _diff_prompt(dq_b, dk_b, dv_b, tabs['t5'], wts, l, nb, seq, lam_init)
        o_c = _band_prompt(bq_b, bk_b, bv_b, tabs['band'], nb, seq)
        prev, lp = jnp.zeros((1, nb, CONV_W - 1, D_FF), F32), 0
    else:
        c_ckv, c_krope, c_dk, c_dv, c_bk, c_bv, prev = caches
        plen = c_ckv.shape[1] // nb
        o_a = _mla_sample(qf, c_ckv, c_krope, state_bufs[0], state_bufs[1], wts['w_ukv'], l,
                          nb, seq, plen)
        o_b = _diff_sample(dq_b, dk_b, dv_b, c_dk, c_dv, tabs['t5'], wts, l, nb, seq, plen,
                           lam_init)
        o_c = _band_sample(bq_b, bk_b, bv_b, c_bk, c_bv, tabs['band'], l, nb, seq,
                           c_bk.shape[3])
        lp = l
    x1 = _oproj_ln(o_a, o_b, o_c, x2, wts, l, tm_mid)
    x_out, conv_state = _ffn(x1, prev, lp, wts, l, seq, tm_ffn, tf_ffn)
    return x_out, state_bufs, conv_state


def _final_states(bufs, conv, nb, seq):
    ckv, krope_t, dk, dv, bk, bv = bufs
    return (ckv.reshape(DEPTH, nb, seq, MLA_KV_RANK), jnp.transpose(krope_t, (0, 1, 3, 2)),
            dk.reshape(DEPTH, nb, seq, DIFF_HEADS, 2 * DIFF_QK),
            dv.reshape(DEPTH, nb, seq, DIFF_HEADS, DIFF_V),
            jnp.transpose(bk, (0, 1, 3, 2, 4)), jnp.transpose(bv, (0, 1, 3, 2, 4)),
            jnp.stack(conv))


def kernel(x_prompt, x_sample, cache_mla_ckv, cache_mla_krope, cache_diff_k, cache_diff_v, cache_band_k, cache_band_v, state_ffn_conv, t5_table, w_in, mla_q_norm, mla_w_uq, mla_kv_norm, mla_w_ukv, diff_lq1, diff_lk1, diff_lq2, diff_lk2, diff_subln, band_rel_table, w_o, ln1_g, ln1_b, ffn_w_gate, ffn_w_up, ffn_conv_w, ffn_conv_b, ffn_w_down, ln2_g, ln2_b):
    nb_p, seq_p, _ = x_prompt.shape
    nb_s, seq_s, _ = x_sample.shape
    past_len = cache_mla_ckv.shape[2]
    band_len = cache_band_k.shape[2]
    assert seq_p % ATT_T == 0 and seq_s == CHUNK and past_len % CHUNK == 0
    assert band_len == BAND_ROWS

    w_in_p, w_uq_p, w_ukv_p = _prep_weights(w_in, mla_w_uq, mla_w_ukv)
    vec = lambda a: a.reshape(DEPTH, 1, -1)
    wts = {'w_in': w_in_p, 'w_uq': w_uq_p, 'w_ukv': w_ukv_p,
           'q_norm': vec(mla_q_norm), 'kv_norm': vec(mla_kv_norm),
           'lam': jnp.stack([diff_lq1, diff_lk1, diff_lq2, diff_lk2], axis=1),
           'subln': vec(diff_subln), 'w_o': w_o.astype(BF16),
           'ln1_g': vec(ln1_g), 'ln1_b': vec(ln1_b),
           'w_gate': ffn_w_gate.astype(BF16), 'w_up': ffn_w_up.astype(BF16),
           'w_down': ffn_w_down.astype(BF16), 'conv_w': ffn_conv_w, 'conv_b': vec(ffn_conv_b),
           'ln2_g': vec(ln2_g), 'ln2_b': vec(ln2_b)}
    caches = (cache_mla_ckv.reshape(DEPTH, nb_s * past_len, MLA_KV_RANK),
              jnp.transpose(cache_mla_krope, (0, 1, 3, 2)),
              cache_diff_k.reshape(DEPTH, nb_s, past_len * DIFF_HEADS, 128),
              cache_diff_v.reshape(DEPTH, nb_s, past_len * DIFF_HEADS, 128),
              jnp.transpose(cache_band_k, (0, 1, 3, 2, 4)),
              jnp.transpose(cache_band_v, (0, 1, 3, 2, 4)),
              state_ffn_conv)

    t = ATT_T
    pos_p = jnp.arange(seq_p, dtype=jnp.int32)
    pos_s = past_len + jnp.arange(seq_s, dtype=jnp.int32)
    tm_s = nb_s * seq_s
    rope_s = tuple(jnp.tile(tb, (nb_s, 1)) for tb in _rope_tables(pos_s))
    tabs_p = {'rope': _rope_tables(pos_p),
              't5': _t5_bias(t5_table, ATT_T, seq_p, ATT_T - seq_p)}
    tabs_s = {'rope': rope_s, 't5': _t5_bias(t5_table, seq_s, past_len + seq_s, -past_len)}

    y_p = x_prompt.reshape(nb_p * seq_p, D_MODEL)
    y_s = x_sample.reshape(tm_s, D_MODEL)
    bufs_p = bufs_s = None
    conv_p, conv_s = [], []
    for l in range(DEPTH):
        lam_init = 0.8 - 0.6 * math.exp(-0.3 * l)
        band_p = _band_bias(band_rel_table[l], t)
        band_s = _band_bias(band_rel_table[l], seq_s)
        y_p, bufs_p, cs = _layer(y_p, nb_p, seq_p, wts, l, dict(tabs_p, band=band_p), None,
                                 bufs_p, lam_init, tm_proj=256, tm_mid=512, tm_ffn=1024, tf_ffn=512)
        conv_p.append(cs)
        y_s, bufs_s, cs = _layer(y_s, nb_s, seq_s, wts, l, dict(tabs_s, band=band_s), caches,
                                 bufs_s, lam_init, tm_proj=256, tm_mid=512, tm_ffn=tm_s, tf_ffn=512)
        conv_s.append(cs)
    return (y_p.reshape(nb_p, seq_p, D_MODEL), y_s.reshape(nb_s, seq_s, D_MODEL),
            *_final_states(bufs_p, conv_p, nb_p, seq_p),
            *_final_states(bufs_s, conv_s, nb_s, seq_s))
```

```python
import functools
import math

import jax
import jax.numpy as jnp
from jax import lax
from jax.experimental import pallas as pl
from jax.experimental.pallas import tpu as pltpu

F32 = jnp.float32
BF16 = jnp.bfloat16

D_MODEL = 2048
DEPTH = 2
CHUNK = 64
MLA_HEADS = 6
MLA_Q_RANK = 512
MLA_KV_RANK = 256
MLA_NOPE = 128
MLA_ROPE = 64
MLA_V = 128
ROPE_THETA = 10000.0
DIFF_HEADS = 4
DIFF_QK = 64
DIFF_V = 128
BAND_HEADS = 6
BAND_DIM = 128
BAND_PREV_CHUNKS = 8
BAND_REL_CLIP = 256
T5_BUCKETS = 32
T5_MAX_DIST = 128
D_FF = 5632
CONV_W = 3
LN_EPS = 1e-5
RMS_EPS = 1e-6
DEEPNORM_ALPHA = (2 * DEPTH) ** 0.25

NEG = -1e30
MLA_QK = 256
MLA_SCALE = (MLA_NOPE + MLA_ROPE) ** -0.5
DIFF_SCALE = DIFF_QK ** -0.5
BAND_SCALE = BAND_DIM ** -0.5
BAND_ROWS = BAND_PREV_CHUNKS * CHUNK
ATT_T = 256
ATT_HEADS_PER_STEP = 2
OPROJ_SUB = 128
VMEM_LIMIT = 60 * 1024 * 1024

_O_CQ, _O_CKV, _O_DQ, _O_DK, _O_DV = 0, 512, 768, 1280, 1792
_O_BQ, _O_BK, _O_BV, _O_KR, _O_KRS, _IN_P = 2304, 3072, 3840, 4608, 4736, 4864
_O_MIX_DIFF = MLA_HEADS * MLA_V
_O_MIX_BAND = _O_MIX_DIFF + DIFF_HEADS * DIFF_V
_N_STATE = 6


def _cparams(sem):
    return pltpu.CompilerParams(dimension_semantics=sem, vmem_limit_bytes=VMEM_LIMIT)


def _dot(a, b):
    return jnp.dot(a, b, preferred_element_type=F32)


def _dot_nt(a, b):
    return lax.dot_general(a, b, (((1,), (1,)), ((), ())), preferred_element_type=F32)


def _rms(x, g):
    return x * lax.rsqrt(jnp.mean(x * x, axis=-1, keepdims=True) + RMS_EPS) * g


def _layer_norm(x, g, b):
    mu = jnp.mean(x, axis=-1, keepdims=True)
    xc = x - mu
    var = jnp.mean(xc * xc, axis=-1, keepdims=True)
    return xc * lax.rsqrt(var + LN_EPS) * g + b


def _layer_block(arr, l):
    zeros = (0,) * (arr.ndim - 1)
    return pl.BlockSpec((None,) + arr.shape[1:], lambda *_: (l,) + zeros,
                        pipeline_mode=pl.Buffered(1))


def _heads_to_rows(o_ref, val, heads):
    rows = val.shape[0]
    for h in range(heads):
        o_ref[pl.ds(h, rows, stride=heads), :] = val[:, h * 128:(h + 1) * 128]


def _head_rows(ref, h, rows, heads):
    return ref[pl.ds(h, rows, stride=heads), :]


def _proj_kernel(*refs, layer, n_alias, streams, tiles_per_seq, band_from):
    (x_ref, win_ref, wuq_ref, wukv_ref, qn_ref, kvn_ref, cos_ref, sin_ref) = refs[:8]
    (ckv_o, krope_o, dk_o, dv_o, bk_o, bv_o,
     dq_b, dk_b, dv_b, bq_b, bk_b, bv_b, qf_b, kf_b, v_b) = refs[8 + n_alias:]
    rows = x_ref.shape[0] // streams

    def put(ref, write):
        if n_alias:
            write(ref)
            return
        for d in range(DEPTH):
            if d == layer:
                write(ref.at[d])
            else:
                ref[d] = jnp.zeros(ref.shape[1:], F32)

    def store(val):
        def write(r):
            r[...] = val
        return write

    def store_heads(val, heads):
        return lambda r: _heads_to_rows(r, val, heads)

    def store_rope_t(val):
        vt = val.T[0:MLA_ROPE]
        def write(r):
            for s in range(streams):
                r[s] = vt[:, s * rows:(s + 1) * rows]
        return write

    def store_head_major(val, heads):
        def write(r):
            for s in range(streams):
                for h in range(heads):
                    r[s, h] = val[s * rows:(s + 1) * rows, h * 128:(h + 1) * 128]
        return write

    xb = x_ref[...].astype(BF16)
    cos = cos_ref[...]
    sin = sin_ref[...]

    def seg(off, n):
        return _dot(xb, win_ref[:, off:off + n])

    cq = _rms(seg(_O_CQ, MLA_Q_RANK), qn_ref[...]).astype(BF16)
    ckv = _rms(seg(_O_CKV, MLA_KV_RANK), kvn_ref[...])
    put(ckv_o, store(ckv))
    ckvb = ckv.astype(BF16)
    krs = seg(_O_KR, 256)
    kr = krs[:, 0:128] * cos + krs[:, 128:256] * sin
    put(krope_o, store_rope_t(kr))
    krb = kr.astype(BF16)

    dq_b[...] = (seg(_O_DQ, 512) * DIFF_SCALE).astype(BF16)
    dk = seg(_O_DK, 512)
    put(dk_o, store_heads(dk, DIFF_HEADS))
    dk_b[...] = dk.astype(BF16)
    dv = seg(_O_DV, 512)
    put(dv_o, store_heads(dv, DIFF_HEADS))
    dv_b[...] = dv.astype(BF16)

    nh = MLA_HEADS * 128
    q = _dot(cq, wuq_ref[...])
    for h in range(MLA_HEADS):
        lo = h * 128
        qr = q[:, nh + lo:nh + lo + 128] * cos + q[:, 2 * nh + lo:2 * nh + lo + 128] * sin
        qf_b[h, :, 0:128] = (q[:, lo:lo + 128] * MLA_SCALE).astype(BF16)
        qf_b[h, :, 128:256] = (qr * MLA_SCALE).astype(BF16)

    kv = _dot(ckvb, wukv_ref[...])
    for h in range(MLA_HEADS):
        lo = h * 128
        kf_b[h, :, 0:128] = kv[:, lo:lo + 128].astype(BF16)
        kf_b[h, :, 128:256] = krb
    v_b[...] = kv[:, nh:2 * nh].astype(BF16)

    bq_b[...] = (seg(_O_BQ, 768) * BAND_SCALE).astype(BF16)
    bk = seg(_O_BK, 768)
    bk_b[...] = bk.astype(BF16)
    bv = seg(_O_BV, 768)
    bv_b[...] = bv.astype(BF16)

    def band_state():
        put(bk_o, store_head_major(bk, BAND_HEADS))
        put(bv_o, store_head_major(bv, BAND_HEADS))

    if band_from == 0:
        band_state()
    else:
        pl.when(pl.program_id(0) % tiles_per_seq >= band_from)(band_state)


def _proj(x2, wts, l, cos_t, sin_t, tm, seq, state_bufs):
    m = x2.shape[0]
    nb = m // seq
    period = cos_t.shape[0] // tm
    keep = min(BAND_ROWS, seq)
    n_alias = 0 if state_bufs is None else _N_STATE
    lead, li = (DEPTH, 0) if state_bufs is None else (None, l)
    if seq > tm:
        tps, kt, spt = seq // tm, keep // tm, 1
        band_from = tps - kt
        band_tile = lambda t: jnp.maximum(t % tps - band_from, 0)
    else:
        assert keep == seq
        tps, band_from, spt = 1, 0, tm // seq
        band_tile = lambda t: 0
    rows = tm // spt
    row = lambda n: pl.BlockSpec((tm, n), lambda t: (t, 0))
    srow = lambda n, k=1: pl.BlockSpec((lead, tm * k, n), lambda t: (li, t, 0))
    rope_t = pl.BlockSpec((lead, spt, MLA_ROPE, rows), lambda t: (li, t // tps, 0, t % tps))
    band = pl.BlockSpec((lead, spt, BAND_HEADS, rows, 128),
                        lambda t: (li, t // tps, 0, band_tile(t), 0))
    tab = pl.BlockSpec((tm, 128), lambda t: (t % period, 0))
    head = pl.BlockSpec((MLA_HEADS, tm, MLA_QK), lambda t: (0, t, 0))
    st = lambda *shape: jax.ShapeDtypeStruct((DEPTH,) + shape, F32)
    b16 = lambda n: jax.ShapeDtypeStruct((m, n), BF16)
    hb16 = jax.ShapeDtypeStruct((MLA_HEADS, m, MLA_QK), BF16)
    kern = functools.partial(_proj_kernel, layer=l, n_alias=n_alias, streams=spt,
                             tiles_per_seq=tps, band_from=band_from)
    ins = [x2, wts['w_in'], wts['w_uq'], wts['w_ukv'], wts['q_norm'], wts['kv_norm'],
           cos_t, sin_t]
    in_specs = [row(D_MODEL), _layer_block(wts['w_in'], l), _layer_block(wts['w_uq'], l),
                _layer_block(wts['w_ukv'], l), _layer_block(wts['q_norm'], l),
                _layer_block(wts['kv_norm'], l), tab, tab]
    if n_alias:
        ins += list(state_bufs)
        in_specs += [pl.BlockSpec(memory_space=pl.ANY)] * n_alias
    outs = pl.pallas_call(
        kern,
        grid=(m // tm,),
        in_specs=in_specs,
        out_specs=[srow(256), rope_t, srow(128, DIFF_HEADS), srow(128, DIFF_HEADS), band, band,
                   row(512), row(512), row(512), row(768), row(768), row(768),
                   head, head, row(768)],
        out_shape=[st(m, 256), st(nb, MLA_ROPE, seq), st(m * DIFF_HEADS, 128),
                   st(m * DIFF_HEADS, 128), st(nb, BAND_HEADS, keep, 128),
                   st(nb, BAND_HEADS, keep, 128),
                   b16(512), b16(512), b16(512), b16(768), b16(768), b16(768),
                   hb16, hb16, b16(768)],
        input_output_aliases={8 + k: k for k in range(n_alias)},
        compiler_params=_cparams(("arbitrary",)),
        name="proj",
    )(*ins)
    return outs[:_N_STATE], outs[_N_STATE:]


def _chunk_mask(t):
    r = lax.broadcasted_iota(jnp.int32, (t, t), 0) // CHUNK
    c = lax.broadcasted_iota(jnp.int32, (t, t), 1) // CHUNK
    return r >= c


def _softmax_pv(blocks):
    m = None
    for s, _ in blocks:
        bm = jnp.max(s, axis=-1, keepdims=True)
        m = bm if m is None else jnp.maximum(m, bm)
    l = None
    acc = None
    for s, vb in blocks:
        p = jnp.exp(s - m)
        bl = jnp.sum(p, axis=-1, keepdims=True)
        ba = _dot(p.astype(BF16), vb)
        l = bl if l is None else l + bl
        acc = ba if acc is None else acc + ba
    return acc / l


def _head_cols(ref, hh):
    return lambda a, b: ref[a:b, hh * 128:(hh + 1) * 128]


def _causal_scores(qs, k, values, n, t, mask, bias=None):
    out = []
    for q in qs:
        s = _dot_nt(q, k)
        if bias is not None:
            s = s + bias
        blocks = [(jnp.where(mask, s[:, n - t:], NEG), values(n - t, n))]
        if n > t:
            blocks.append((s[:, :n - t], values(0, n - t)))
        out.append(blocks)
    return out


def _tile_order(nq):
    return [(tile, hh) for tile in reversed(range(nq)) for hh in range(ATT_HEADS_PER_STEP)]


def _pipelined(order, produce, consume):
    nxt = produce(order[0])
    for pos, i in enumerate(order):
        cur = nxt
        if pos + 1 < len(order):
            nxt = produce(order[pos + 1])
        consume(i, cur)


def _mla_prompt_kernel(q_ref, k_ref, v_ref, o_ref, *, nq):
    t = ATT_T
    mask = _chunk_mask(t)

    def scores(item):
        tile, hh = item
        n = (tile + 1) * t
        return _causal_scores([q_ref[hh, n - t:n, :]], k_ref[hh, 0:n, :],
                              _head_cols(v_ref, hh), n, t, mask)

    def finish(item, blocks):
        tile, hh = item
        o_ref[tile * t:(tile + 1) * t, hh * 128:(hh + 1) * 128] = (
            _softmax_pv(blocks[0]).astype(BF16))

    _pipelined(_tile_order(nq), scores, finish)


def _mla_prompt(qf, kf, v, nb, seq):
    hps = ATT_HEADS_PER_STEP
    return pl.pallas_call(
        functools.partial(_mla_prompt_kernel, nq=seq // ATT_T),
        grid=(nb, MLA_HEADS // hps),
        in_specs=[pl.BlockSpec((hps, seq, MLA_QK), lambda b, h: (h, b, 0)),
                  pl.BlockSpec((hps, seq, MLA_QK), lambda b, h: (h, b, 0)),
                  pl.BlockSpec((seq, hps * MLA_V), lambda b, h: (b, h))],
        out_specs=pl.BlockSpec((seq, hps * MLA_V), lambda b, h: (b, h)),
        out_shape=jax.ShapeDtypeStruct((nb * seq, MLA_HEADS * MLA_V), BF16),
        compiler_params=_cparams(("arbitrary",) * 2),
        name="mla_prompt",
    )(qf, kf, v)


def _mla_sample_kernel(q_ref, cp_ref, rp_ref, cn_ref, rn_ref, wukv_ref, o_ref, q_scr, *, seq):
    nh = MLA_HEADS * 128
    lat = MLA_KV_RANK
    for h in range(MLA_HEADS):
        rows = slice(h * seq, (h + 1) * seq)
        w_k = wukv_ref[:, h * 128:(h + 1) * 128]
        q_scr[rows, 0:lat] = _dot_nt(q_ref[h, :, 0:MLA_NOPE], w_k).astype(BF16)
        q_scr[rows, lat:lat + 128] = q_ref[h, :, MLA_NOPE:MLA_QK]
    q_lat = q_scr[:, 0:lat]
    q_rope = q_scr[:, lat:lat + MLA_ROPE]
    blocks = []
    for c_ref, r_ref in ((cp_ref, rp_ref), (cn_ref, rn_ref)):
        c = c_ref[...].astype(BF16)
        r_t = r_ref[...].astype(BF16)
        blocks.append((_dot_nt(q_lat, c) + _dot(q_rope, r_t), c))
    o_lat = _softmax_pv(blocks).astype(BF16)
    for h in range(MLA_HEADS):
        w_v = wukv_ref[:, nh + h * 128:nh + (h + 1) * 128]
        o_ref[:, h * 128:(h + 1) * 128] = _dot(o_lat[h * seq:(h + 1) * seq], w_v).astype(BF16)


def _mla_sample(qf, c_past, r_past, c_new, r_new, w_ukv, l, nb, seq, past):
    lat = lambda rows: pl.BlockSpec((None, rows, MLA_KV_RANK), lambda b: (l, b, 0))
    rope = lambda frames: pl.BlockSpec((None, None, MLA_ROPE, frames), lambda b: (l, b, 0, 0))
    return pl.pallas_call(
        functools.partial(_mla_sample_kernel, seq=seq),
        grid=(nb,),
        in_specs=[pl.BlockSpec((MLA_HEADS, seq, MLA_QK), lambda b: (0, b, 0)),
                  lat(past), rope(past), lat(seq), rope(seq), _layer_block(w_ukv, l)],
        out_specs=pl.BlockSpec((seq, MLA_HEADS * MLA_V), lambda b: (b, 0)),
        out_shape=jax.ShapeDtypeStruct((nb * seq, MLA_HEADS * MLA_V), BF16),
        scratch_shapes=[pltpu.VMEM((MLA_HEADS * seq, MLA_KV_RANK + 128), BF16)],
        compiler_params=_cparams(("arbitrary",)),
        name="mla_sample",
    )(qf, c_past, r_past, c_new, r_new, w_ukv)


def _diff_lambda(lam_ref, lam_init):
    v = lam_ref[...]
    a = jnp.sum(v[0:1] * v[1:2], axis=-1, keepdims=True)
    b = jnp.sum(v[2:3] * v[3:4], axis=-1, keepdims=True)
    return jnp.exp(a) - jnp.exp(b) + lam_init


def _split_q(q):
    lane = lax.broadcasted_iota(jnp.int32, q.shape, 1)
    qf = q.astype(F32)
    return (jnp.where(lane < DIFF_QK, qf, 0.0).astype(BF16),
            jnp.where(lane >= DIFF_QK, qf, 0.0).astype(BF16))


def _diff_finish(o0, o1, lam, g, lam_init):
    o = o0 - lam * o1
    return (_rms(o, g) * (1.0 - lam_init)).astype(BF16)


def _diff_prompt_kernel(q_ref, k_ref, v_ref, bias_ref, lam_ref, g_ref, o_ref, *, lam_init, nq):
    t = ATT_T
    seq = nq * t
    lam = _diff_lambda(lam_ref, lam_init)
    mask = _chunk_mask(t)

    def scores(item):
        tile, hh, half = item
        n = (tile + 1) * t
        cols = slice(hh * 128, (hh + 1) * 128)
        q = _split_q(q_ref[n - t:n, cols])[half]
        return _causal_scores([q], k_ref[0:n, cols], _head_cols(v_ref, hh), n, t, mask,
                              bias=bias_ref[hh, :, seq - n:seq])[0]

    first_half = {}

    def finish(item, blocks):
        tile, hh, half = item
        o = _softmax_pv(blocks)
        if half == 0:
            first_half[tile, hh] = o
        else:
            o_ref[tile * t:(tile + 1) * t, hh * 128:(hh + 1) * 128] = _diff_finish(
                first_half.pop((tile, hh)), o, lam, g_ref[...], lam_init)

    _pipelined([(tile, hh, half) for tile, hh in _tile_order(nq) for half in (0, 1)],
               scores, finish)


def _diff_prompt(dq, dk, dv, bias, wts, l, nb, seq, lam_init):
    t = ATT_T
    hps = ATT_HEADS_PER_STEP
    blk = pl.BlockSpec((seq, hps * 128), lambda b, h: (b, h))
    return pl.pallas_call(
        functools.partial(_diff_prompt_kernel, lam_init=lam_init, nq=seq // t),
        grid=(nb, DIFF_HEADS // hps),
        in_specs=[blk, blk, blk,
                  pl.BlockSpec((hps, t, seq), lambda b, h: (h, 0, 0)),
                  _layer_block(wts['lam'], l), _layer_block(wts['subln'], l)],
        out_specs=blk,
        out_shape=jax.ShapeDtypeStruct((nb * seq, DIFF_HEADS * DIFF_V), BF16),
        compiler_params=_cparams(("arbitrary",) * 2),
        name="diff_prompt",
    )(dq, dk, dv, bias, wts['lam'], wts['subln'])


def _diff_sample_kernel(q_ref, kp_ref, vp_ref, kn_ref, vn_ref, bias_ref, lam_ref, g_ref,
                        o_ref, *, lam_init, past):
    lam = _diff_lambda(lam_ref, lam_init)
    head_ops = {}

    def operands(h):
        if h not in head_ops:
            cols = slice(h * 128, (h + 1) * 128)
            head_ops[h] = (_split_q(q_ref[:, cols]),
                           _head_rows(kp_ref, h, past, DIFF_HEADS).astype(BF16),
                           _head_rows(vp_ref, h, past, DIFF_HEADS).astype(BF16),
                           kn_ref[:, cols], vn_ref[:, cols])
        return head_ops[h]

    def scores(item):
        h, half = item
        qs, kp, vp, kn, vn = operands(h)
        return [(_dot_nt(qs[half], kp) + bias_ref[h, :, 0:past], vp),
                (_dot_nt(qs[half], kn) + bias_ref[h, :, past:], vn)]

    first_half = {}

    def finish(item, blocks):
        h, half = item
        o = _softmax_pv(blocks)
        if half == 0:
            first_half[h] = o
        else:
            o_ref[:, h * 128:(h + 1) * 128] = _diff_finish(first_half.pop(h), o, lam,
                                                           g_ref[...], lam_init)

    _pipelined([(h, half) for h in range(DIFF_HEADS) for half in (0, 1)], scores, finish)


def _diff_sample(dq, dk_new, dv_new, k_past, v_past, bias, wts, l, nb, seq, past, lam_init):
    width = DIFF_HEADS * 128
    new = pl.BlockSpec((seq, width), lambda b: (b, 0))
    cache = pl.BlockSpec((None, None, past * DIFF_HEADS, 128), lambda b: (l, b, 0, 0))
    return pl.pallas_call(
        functools.partial(_diff_sample_kernel, lam_init=lam_init, past=past),
        grid=(nb,),
        in_specs=[new, cache, cache, new, new,
                  pl.BlockSpec(bias.shape, lambda b: (0, 0, 0)),
                  _layer_block(wts['lam'], l), _layer_block(wts['subln'], l)],
        out_specs=new,
        out_shape=jax.ShapeDtypeStruct((nb * seq, width), BF16),
        compiler_params=_cparams(("arbitrary",)),
        name="diff_sample",
    )(dq, k_past, v_past, dk_new, dv_new, bias, wts['lam'], wts['subln'])


def _band_prompt_kernel(q_ref, k_ref, v_ref, bias_ref, o_ref, *, nq):
    t = ATT_T
    full = BAND_ROWS + t

    def scores(item):
        tile, hh = item
        cols = slice(hh * 128, (hh + 1) * 128)
        start = max(0, tile * t - BAND_ROWS)
        width = (tile + 1) * t - start
        s = (_dot_nt(q_ref[tile * t:(tile + 1) * t, cols], k_ref[start:start + width, cols])
             + bias_ref[hh, :, full - width:full])
        return [(s, v_ref[start:start + width, cols])]

    def finish(item, blocks):
        tile, hh = item
        o_ref[tile * t:(tile + 1) * t, hh * 128:(hh + 1) * 128] = (
            _softmax_pv(blocks).astype(BF16))

    _pipelined(_tile_order(nq), scores, finish)


def _band_prompt(bq, bk, bv, bias, nb, seq):
    t = ATT_T
    hps = ATT_HEADS_PER_STEP
    blk = pl.BlockSpec((seq, hps * BAND_DIM), lambda b, h: (b, h))
    return pl.pallas_call(
        functools.partial(_band_prompt_kernel, nq=seq // t),
        grid=(nb, BAND_HEADS // hps),
        in_specs=[blk, blk, blk,
                  pl.BlockSpec((hps, t, BAND_ROWS + t), lambda b, h: (h, 0, 0))],
        out_specs=blk,
        out_shape=jax.ShapeDtypeStruct((nb * seq, BAND_HEADS * BAND_DIM), BF16),
        compiler_params=_cparams(("arbitrary",) * 2),
        name="band_prompt",
    )(bq, bk, bv, bias)


def _band_sample_kernel(q_ref, kp_ref, vp_ref, kn_ref, vn_ref, bias_ref, o_ref, *, past):
    def scores(h):
        cols = slice(h * 128, (h + 1) * 128)
        kp = kp_ref[h].astype(BF16)
        vp = vp_ref[h].astype(BF16)
        q = q_ref[:, cols]
        return [(_dot_nt(q, kp) + bias_ref[h, :, 0:past], vp),
                (_dot_nt(q, kn_ref[:, cols]) + bias_ref[h, :, past:], vn_ref[:, cols])]

    def finish(h, blocks):
        o_ref[:, h * 128:(h + 1) * 128] = _softmax_pv(blocks).astype(BF16)

    _pipelined(list(range(BAND_HEADS)), scores, finish)


def _band_sample(bq, bk_new, bv_new, k_past, v_past, bias, l, nb, seq, past):
    width = BAND_HEADS * BAND_DIM
    new = pl.BlockSpec((seq, width), lambda b: (b, 0))
    cache = pl.BlockSpec((None, None, BAND_HEADS, past, 128), lambda b: (l, b, 0, 0, 0))
    return pl.pallas_call(
        functools.partial(_band_sample_kernel, past=past),
        grid=(nb,),
        in_specs=[new, cache, cache, new, new, pl.BlockSpec(bias.shape, lambda b: (0, 0, 0))],
        out_specs=new,
        out_shape=jax.ShapeDtypeStruct((nb * seq, width), BF16),
        compiler_params=_cparams(("arbitrary",)),
        name="band_sample",
    )(bq, k_past, v_past, bk_new, bv_new, bias)


def _oproj_kernel(a_ref, b_ref, c_ref, x_ref, wo_ref, g_ref, beta_ref, o_ref):
    def mix(r):
        rows = slice(r, r + OPROJ_SUB)
        return (_dot(a_ref[rows, :], wo_ref[0:_O_MIX_DIFF, :])
                + _dot(b_ref[rows, :], wo_ref[_O_MIX_DIFF:_O_MIX_BAND, :])
                + _dot(c_ref[rows, :], wo_ref[_O_MIX_BAND:, :]))

    def finish(r, m):
        rows = slice(r, r + OPROJ_SUB)
        o_ref[rows, :] = _layer_norm(DEEPNORM_ALPHA * x_ref[rows, :] + m,
                                     g_ref[...], beta_ref[...])

    _pipelined(list(range(0, x_ref.shape[0], OPROJ_SUB)), mix, finish)


def _oproj_ln(o_a, o_b, o_c, x2, wts, l, tm):
    m = x2.shape[0]
    row = lambda n: pl.BlockSpec((tm, n), lambda t: (t, 0))
    return pl.pallas_call(
        _oproj_kernel,
        grid=(m // tm,),
        in_specs=[row(o_a.shape[1]), row(o_b.shape[1]), row(o_c.shape[1]), row(D_MODEL),
                  _layer_block(wts['w_o'], l), _layer_block(wts['ln1_g'], l),
                  _layer_block(wts['ln1_b'], l)],
        out_specs=row(D_MODEL),
        out_shape=jax.ShapeDtypeStruct((m, D_MODEL), F32),
        compiler_params=_cparams(("arbitrary",)),
        name="oproj_ln",
    )(o_a, o_b, o_c, x2, wts['w_o'], wts['ln1_g'], wts['ln1_b'])


def _ffn_kernel(x_ref, prev_ref, wg_ref, wu_ref, wd_ref, cw_ref, cb_ref, g_ref, beta_ref,
                o_ref, st_ref, carry_ref, *, tf, seg, tiles_per_seq):
    i = pl.program_id(0)
    j = pl.program_id(1)
    nj = pl.num_programs(1)
    tm = x_ref.shape[0]
    col = pl.ds(pl.multiple_of(j * tf, tf), tf)

    xb = x_ref[...].astype(BF16)
    g = _dot(xb, wg_ref[...])
    u = _dot(xb, wu_ref[...])
    cw = cw_ref[...]
    cb = cb_ref[...]
    rows = lax.broadcasted_iota(jnp.int32, (seg, tf), 0)
    seq_start = (i % tiles_per_seq) == 0

    hs = []
    for s in range(tm // seg):
        gs = g[s * seg:(s + 1) * seg]
        state = prev_ref[s]
        if tiles_per_seq > 1:
            state = jnp.where(seq_start, state, carry_ref[:, col])
        pm2, pm1 = state[0:1], state[1:2]
        gm1 = jnp.where(rows == 0, pm1, pltpu.roll(gs, 1, 0))
        gm2 = jnp.where(rows == 0, pm2, jnp.where(rows == 1, pm1, pltpu.roll(gs, 2, 0)))
        gc = cb + cw[0:1] * gm2 + cw[1:2] * gm1 + cw[2:3] * gs
        hs.append(gc * (1.0 / (1.0 + jnp.exp(-gc))) * u[s * seg:(s + 1) * seg])
        last = gs[seg - 2:seg]
        st_ref[s, :, col] = last
        if tiles_per_seq > 1:
            carry_ref[:, col] = last
    h = hs[0] if len(hs) == 1 else jnp.concatenate(hs, axis=0)
    o_ref[...] = jnp.where(j == 0, 0.0, o_ref[...]) + _dot(h.astype(BF16), wd_ref[...])

    @pl.when(j == nj - 1)
    def _():
        o_ref[...] = _layer_norm(DEEPNORM_ALPHA * x_ref[...] + o_ref[...],
                                 g_ref[...], beta_ref[...])


def _ffn(x2, prev, lp, wts, l, seq, tm, tf):
    m = x2.shape[0]
    nseq = m // seq
    seg = min(seq, tm)
    spt = tm // seg
    tps = seq // seg
    kern = functools.partial(_ffn_kernel, tf=tf, seg=seg, tiles_per_seq=tps)
    vec = lambda n: pl.BlockSpec((None, n, tf), lambda i, j: (l, 0, j))
    ln = lambda arr: pl.BlockSpec((None, 1, D_MODEL), lambda i, j: (l, 0, 0))
    return pl.pallas_call(
        kern,
        grid=(m // tm, D_FF // tf),
        in_specs=[pl.BlockSpec((tm, D_MODEL), lambda i, j: (i, 0)),
                  pl.BlockSpec((None, spt, CONV_W - 1, tf), lambda i, j: (lp, i // tps, 0, j)),
                  pl.BlockSpec((None, D_MODEL, tf), lambda i, j: (l, 0, j)),
                  pl.BlockSpec((None, D_MODEL, tf), lambda i, j: (l, 0, j)),
                  pl.BlockSpec((None, tf, D_MODEL), lambda i, j: (l, j, 0)),
                  vec(CONV_W), vec(1), ln(wts['ln2_g']), ln(wts['ln2_b'])],
        out_specs=[pl.BlockSpec((tm, D_MODEL), lambda i, j: (i, 0)),
                   pl.BlockSpec((spt, CONV_W - 1, D_FF), lambda i, j: (i // tps, 0, 0))],
        out_shape=[jax.ShapeDtypeStruct((m, D_MODEL), F32),
                   jax.ShapeDtypeStruct((nseq, CONV_W - 1, D_FF), F32)],
        scratch_shapes=[pltpu.VMEM((CONV_W - 1, D_FF), F32)],
        compiler_params=_cparams(("arbitrary", "arbitrary")),
        name="ffn",
    )(x2, prev, wts['w_gate'], wts['w_up'], wts['w_down'], wts['conv_w'], wts['conv_b'],
      wts['ln2_g'], wts['ln2_b'])


def _win_kernel(wt_ref, o_ref):
    half = MLA_ROPE // 2
    lo, hi = _O_CKV + MLA_KV_RANK, _O_CKV + MLA_KV_RANK + MLA_ROPE

    def cols(a, b):
        return wt_ref[a:b, :].T.astype(BF16)

    o_ref[:, 0:lo] = cols(0, lo)
    o_ref[:, lo:_O_KR] = cols(hi, wt_ref.shape[0])
    kr = cols(lo, lo + 128)[:, 0:MLA_ROPE]
    zeros = jnp.zeros((kr.shape[0], 128 - MLA_ROPE), BF16)
    o_ref[:, _O_KR:_O_KR + MLA_ROPE] = kr
    o_ref[:, _O_KR + MLA_ROPE:_O_KRS] = zeros
    o_ref[:, _O_KRS:_O_KRS + half] = kr[:, half:]
    o_ref[:, _O_KRS + half:_O_KRS + MLA_ROPE] = kr[:, :half]
    o_ref[:, _O_KRS + MLA_ROPE:_IN_P] = zeros


def _win_relayout(w_in, tk=256):
    w_t = jnp.transpose(w_in, (0, 2, 1))
    depth, n, k = w_t.shape
    return pl.pallas_call(
        _win_kernel,
        grid=(depth, k // tk),
        in_specs=[pl.BlockSpec((None, n, tk), lambda d, t: (d, 0, t))],
        out_specs=pl.BlockSpec((None, tk, _IN_P), lambda d, t: (d, t, 0)),
        out_shape=jax.ShapeDtypeStruct((depth, k, _IN_P), BF16),
        compiler_params=_cparams(("arbitrary", "arbitrary")),
        name="win_relayout",
    )(w_t)


def _prep_weights(w_in, mla_w_uq, mla_w_ukv):
    half = MLA_ROPE // 2
    swap = lambda w: jnp.concatenate([w[..., half:], w[..., :half]], axis=-1)
    pad = lambda w: jnp.concatenate([w, jnp.zeros_like(w)], axis=-1)
    w_in_p = _win_relayout(w_in)
    uq = mla_w_uq.reshape(DEPTH, MLA_Q_RANK, MLA_HEADS, MLA_NOPE + MLA_ROPE)
    nope, rope = uq[..., :MLA_NOPE], uq[..., MLA_NOPE:]
    flat = lambda w: w.reshape(DEPTH, w.shape[1], -1)
    w_uq_p = jnp.concatenate([flat(nope), flat(pad(rope)), flat(pad(swap(rope)))],
                             axis=-1).astype(BF16)
    ukv = mla_w_ukv.reshape(DEPTH, MLA_KV_RANK, MLA_HEADS, MLA_NOPE + MLA_V)
    w_ukv_p = jnp.concatenate([flat(ukv[..., :MLA_NOPE]), flat(ukv[..., MLA_NOPE:])],
                              axis=-1).astype(BF16)
    return w_in_p, w_uq_p, w_ukv_p


def _rope_tables(pos):
    half = MLA_ROPE // 2
    inv = ROPE_THETA ** (-jnp.arange(half, dtype=F32) / half)
    ang = pos.astype(F32)[:, None] * inv
    cos, sin = jnp.cos(ang), jnp.sin(ang)
    z = jnp.zeros((pos.shape[0], 128 - MLA_ROPE), F32)
    return jnp.concatenate([cos, cos, z], -1), jnp.concatenate([-sin, sin, z], -1)


def _t5_bucket(rel):
    half = T5_BUCKETS // 2
    exact = half // 2
    n = jnp.abs(rel)
    nf = jnp.maximum(n, 1).astype(F32)
    large = exact + (jnp.log(nf / exact) / math.log(T5_MAX_DIST / exact)
                     * (half - exact)).astype(jnp.int32)
    large = jnp.minimum(large, half - 1)
    return jnp.where(rel > 0, half, 0) + jnp.where(n < exact, n, large)


def _toeplitz_kernel(u_ref, o_ref):
    rows, cols = o_ref.shape
    period = jnp.broadcast_to(u_ref[...], (rows, u_ref.shape[-1]))
    o_ref[...] = pltpu.roll(period, 0, 1, stride=1, stride_axis=0)[:, :cols]


def _toeplitz(fn, rows, cols, shift):
    p = rows + cols
    assert p % 128 == 0
    k = jnp.arange(p, dtype=jnp.int32)
    u = fn(jnp.where(k < cols, k, k - p) + shift).astype(F32)
    heads = u.shape[0]
    return pl.pallas_call(
        _toeplitz_kernel,
        grid=(heads,),
        in_specs=[pl.BlockSpec((None, 1, p), lambda h: (h, 0, 0))],
        out_specs=pl.BlockSpec((None, rows, cols), lambda h: (h, 0, 0)),
        out_shape=jax.ShapeDtypeStruct((heads, rows, cols), F32),
        compiler_params=_cparams(("arbitrary",)),
        name="toeplitz",
    )(u[:, None, :])


def _t5_bias(t5_table, rows, cols, shift):
    return _toeplitz(lambda rel: t5_table[_t5_bucket(rel)].T, rows, cols, shift)


def _band_bias(rel_table, t):
    def fn(d):
        rel = jnp.clip(BAND_ROWS - d, -BAND_REL_CLIP, BAND_REL_CLIP) + BAND_REL_CLIP
        return rel_table[:, rel]
    bias = _toeplitz(fn, t, BAND_ROWS + t, 0)
    qc = (jnp.arange(t, dtype=jnp.int32)[:, None] + BAND_ROWS) // CHUNK
    kc = jnp.arange(BAND_ROWS + t, dtype=jnp.int32)[None, :] // CHUNK
    ok = (kc <= qc) & (kc >= qc - BAND_PREV_CHUNKS)
    return jnp.where(ok[None], bias, NEG)


def _layer(x2, nb, seq, wts, l, tabs, caches, state_bufs, lam_init, tm_proj, tm_mid, tm_ffn, tf_ffn):
    cos_t, sin_t = tabs['rope']
    state_bufs, (dq_b, dk_b, dv_b, bq_b, bk_b, bv_b, qf, kf, v_b) = _proj(
        x2, wts, l, cos_t, sin_t, tm_proj, seq, state_bufs)
    if caches is None:
        o_a = _mla_prompt(qf, kf, v_b, nb, seq)
        o_b = _diff_prompt(dq_b, dk_b, dv_b, tabs['t5'], wts, l, nb, seq, lam_init)
        o_c = _band_prompt(bq_b, bk_b, bv_b, tabs['band'], nb, seq)
        prev, lp = jnp.zeros((1, nb, CONV_W - 1, D_FF), F32), 0
    else:
        c_ckv, c_krope, c_dk, c_dv, c_bk, c_bv, prev = caches
        plen = c_ckv.shape[1] // nb
        o_a = _mla_sample(qf, c_ckv, c_krope, state_bufs[0], state_bufs[1], wts['w_ukv'], l,
                          nb, seq, plen)
        o_b = _diff_sample(dq_b, dk_b, dv_b, c_dk, c_dv, tabs['t5'], wts, l, nb, seq, plen,
                           lam_init)
        o_c = _band_sample(bq_b, bk_b, bv_b, c_bk, c_bv, tabs['band'], l, nb, seq,
                           c_bk.shape[3])
        lp = l
    x1 = _oproj_ln(o_a, o_b, o_c, x2, wts, l, tm_mid)
    x_out, conv_state = _ffn(x1, prev, lp, wts, l, seq, tm_ffn, tf_ffn)
    return x_out, state_bufs, conv_state


def _final_states(bufs, conv, nb, seq):
    ckv, krope_t, dk, dv, bk, bv = bufs
    return (ckv.reshape(DEPTH, nb, seq, MLA_KV_RANK), jnp.transpose(krope_t, (0, 1, 3, 2)),
            dk.reshape(DEPTH, nb, seq, DIFF_HEADS, 2 * DIFF_QK),
            dv.reshape(DEPTH, nb, seq, DIFF_HEADS, DIFF_V),
            jnp.transpose(bk, (0, 1, 3, 2, 4)), jnp.transpose(bv, (0, 1, 3, 2, 4)),
            jnp.stack(conv))


def kernel(x_prompt, x_sample, cache_mla_ckv, cache_mla_krope, cache_diff_k, cache_diff_v, cache_band_k, cache_band_v, state_ffn_conv, t5_table, w_in, mla_q_norm, mla_w_uq, mla_kv_norm, mla_w_ukv, diff_lq1, diff_lk1, diff_lq2, diff_lk2, diff_subln, band_rel_table, w_o, ln1_g, ln1_b, ffn_w_gate, ffn_w_up, ffn_conv_w, ffn_conv_b, ffn_w_down, ln2_g, ln2_b):
    nb_p, seq_p, _ = x_prompt.shape
    nb_s, seq_s, _ = x_sample.shape
    past_len = cache_mla_ckv.shape[2]
    band_len = cache_band_k.shape[2]
    assert seq_p % ATT_T == 0 and seq_s == CHUNK and past_len % CHUNK == 0
    assert band_len == BAND_ROWS

    w_in_p, w_uq_p, w_ukv_p = _prep_weights(w_in, mla_w_uq, mla_w_ukv)
    vec = lambda a: a.reshape(DEPTH, 1, -1)
    wts = {'w_in': w_in_p, 'w_uq': w_uq_p, 'w_ukv': w_ukv_p,
           'q_norm': vec(mla_q_norm), 'kv_norm': vec(mla_kv_norm),
           'lam': jnp.stack([diff_lq1, diff_lk1, diff_lq2, diff_lk2], axis=1),
           'subln': vec(diff_subln), 'w_o': w_o.astype(BF16),
           'ln1_g': vec(ln1_g), 'ln1_b': vec(ln1_b),
           'w_gate': ffn_w_gate.astype(BF16), 'w_up': ffn_w_up.astype(BF16),
           'w_down': ffn_w_down.astype(BF16), 'conv_w': ffn_conv_w, 'conv_b': vec(ffn_conv_b),
           'ln2_g': vec(ln2_g), 'ln2_b': vec(ln2_b)}
    caches = (cache_mla_ckv.reshape(DEPTH, nb_s * past_len, MLA_KV_RANK),
              jnp.transpose(cache_mla_krope, (0, 1, 3, 2)),
              cache_diff_k.reshape(DEPTH, nb_s, past_len * DIFF_HEADS, 128),
              cache_diff_v.reshape(DEPTH, nb_s, past_len * DIFF_HEADS, 128),
              jnp.transpose(cache_band_k, (0, 1, 3, 2, 4)),
              jnp.transpose(cache_band_v, (0, 1, 3, 2, 4)),
              state_ffn_conv)

    t = ATT_T
    pos_p = jnp.arange(seq_p, dtype=jnp.int32)
    pos_s = past_len + jnp.arange(seq_s, dtype=jnp.int32)
    tm_s = nb_s * seq_s
    rope_s = tuple(jnp.tile(tb, (nb_s, 1)) for tb in _rope_tables(pos_s))
    tabs_p = {'rope': _rope_tables(pos_p),
              't5': _t5_bias(t5_table, ATT_T, seq_p, ATT_T - seq_p)}
    tabs_s = {'rope': rope_s, 't5': _t5_bias(t5_table, seq_s, past_len + seq_s, -past_len)}

    y_p = x_prompt.reshape(nb_p * seq_p, D_MODEL)
    y_s = x_sample.reshape(tm_s, D_MODEL)
    bufs_p = bufs_s = None
    conv_p, conv_s = [], []
    for l in range(DEPTH):
        lam_init = 0.8 - 0.6 * math.exp(-0.3 * l)
        band_p = _band_bias(band_rel_table[l], t)
        band_s = _band_bias(band_rel_table[l], seq_s)
        y_p, bufs_p, cs = _layer(y_p, nb_p, seq_p, wts, l, dict(tabs_p, band=band_p), None,
                                 bufs_p, lam_init, tm_proj=256, tm_mid=1024, tm_ffn=1024, tf_ffn=512)
        conv_p.append(cs)
        y_s, bufs_s, cs = _layer(y_s, nb_s, seq_s, wts, l, dict(tabs_s, band=band_s), caches,
                                 bufs_s, lam_init, tm_proj=256, tm_mid=1024, tm_ffn=tm_s, tf_ffn=512)
        conv_s.append(cs)
    return (y_p.reshape(nb_p, seq_p, D_MODEL), y_s.reshape(nb_s, seq_s, D_MODEL),
            *_final_states(bufs_p, conv_p, nb_p, seq_p),
            *_final_states(bufs_s, conv_s, nb_s, seq_s))
```
